```python
import jax, jax.numpy as jnp
from jax import lax
import numpy as np

D_MODEL = 2048
BATCH = 2
SEQ = 4096
DEPTH = 4

GRID_W = 64
CTX_LEN = 256
CHUNK = 128
SGU_GROUPS = 8
SGU_WIDTH = 2048
RET_HEADS = 8
RET_QK_DIM = 256
RET_V_DIM = 512
RET_QK = RET_HEADS * RET_QK_DIM
RET_V = RET_HEADS * RET_V_DIM
ROPE_BASE = 10000.0
D_FF = 5632
N_EXPERTS = 8
TOP_K = 2
D_FF_EXPERT = 4096
N_DENSE = (DEPTH + 1) // 2
N_MOE = DEPTH // 2
N_MOD = 6
NORM_EPS = 1e-6

Q0 = 0
K0 = Q0 + RET_QK
V0 = K0 + RET_QK
G0 = V0 + RET_V
UA0 = G0 + RET_V
VA0 = UA0 + SGU_WIDTH
GA0 = VA0 + SGU_WIDTH
GR0 = GA0 + D_MODEL
IN_COLS = GR0 + D_MODEL

kernel_name = "hybrid_sgu_retention_moe_dit"


def rms_norm(x, g):
    xf = x.astype(jnp.float32)
    y = xf * lax.rsqrt(jnp.mean(xf * xf, axis=-1, keepdims=True) + NORM_EPS)
    return (y * g.astype(jnp.float32)).astype(x.dtype)


def layer_norm(x):
    xf = x.astype(jnp.float32)
    mu = jnp.mean(xf, axis=-1, keepdims=True)
    var = jnp.mean(jnp.square(xf - mu), axis=-1, keepdims=True)
    return ((xf - mu) * lax.rsqrt(var + NORM_EPS)).astype(x.dtype)


def modulate(x, g, shift, scale):
    return rms_norm(x, g) * (1 + scale) + shift


def rope_1d(x, pos):
    half = x.shape[-1] // 2
    freqs = ROPE_BASE ** (-jnp.arange(half, dtype=jnp.float32) / half)
    ang = pos.astype(jnp.float32)[:, None] * freqs[None, :]
    cos = jnp.cos(ang).astype(x.dtype)
    sin = jnp.sin(ang).astype(x.dtype)
    x1, x2 = x[..., :half], x[..., half:]
    return jnp.concatenate([x1 * cos - x2 * sin, x1 * sin + x2 * cos], axis=-1)


def axial_rope(x, rows, cols):
    half = x.shape[-1] // 2
    return jnp.concatenate([rope_1d(x[..., :half], rows), rope_1d(x[..., half:], cols)], axis=-1)


def spatial_gating(u, v, ln_g, ln_b, w_s, b_s):
    bsz, length, width = v.shape
    vn = layer_norm(v) * ln_g + ln_b
    vh = vn.reshape(bsz, length // CHUNK, CHUNK, SGU_GROUPS, width // SGU_GROUPS)
    s = jnp.einsum('gpq,bnqgc->bnpgc', w_s, vh) + b_s.T[:, :, None]
    return u * s.reshape(bsz, length, width)


def retention_scan(q, k, v, log_g, s0):
    bsz, heads, length, _ = q.shape
    dv = v.shape[-1]
    n_chunks = length // CHUNK
    idx = jnp.arange(CHUNK, dtype=jnp.float32)
    diff = idx[:, None] - idx[None, :]
    intra = jnp.where(diff >= 0, jnp.exp(jnp.maximum(diff, 0.0)[None] * log_g[:, None, None]), 0.0)
    q_decay = jnp.exp((idx + 1.0)[None, :] * log_g[:, None])
    k_decay = jnp.exp((CHUNK - 1.0 - idx)[None, :] * log_g[:, None])
    chunk_decay = jnp.exp(CHUNK * log_g)

    def to_chunks(t):
        return jnp.moveaxis(t.reshape(bsz, heads, n_chunks, CHUNK, t.shape[-1]), 2, 0)

    def step(s, qkv):
        qc, kc, vc = qkv
        scores = jnp.einsum('bhqd,bhkd->bhqk', qc, kc) * intra
        o = (jnp.einsum('bhqk,bhkv->bhqv', scores, vc)
             + jnp.einsum('bhqd,bhdv->bhqv', qc, s) * q_decay[..., None])
        s = s * chunk_decay[:, None, None] + jnp.einsum('bhkd,bhkv->bhdv', kc * k_decay[..., None], vc)
        return s, o

    s_fin, o = lax.scan(step, s0, (to_chunks(q), to_chunks(k), to_chunks(v)))
    o = jnp.moveaxis(o, 0, 2).reshape(bsz, heads, length, dv)
    return o, s_fin


def retention_state(k, v, log_g):
    length = k.shape[2]
    w = jnp.exp((length - 1.0 - jnp.arange(length, dtype=jnp.float32))[None, :] * log_g[:, None])
    return jnp.einsum('bhld,bhlv->bhdv', k * w[..., None], v)


def bidir_retention(q_c, k_c, v_c, q_l, k_l, v_l, log_decay, ctx_out):
    flip = lambda t: jnp.flip(t, axis=2)
    g_f, g_b = log_decay[0].astype(jnp.float32), log_decay[1].astype(jnp.float32)
    zero = jnp.zeros(k_c.shape[:2] + (k_c.shape[-1], v_c.shape[-1]), jnp.float32)
    if ctx_out:
        o_cf, s_f = retention_scan(q_c, k_c, v_c, g_f, zero)
        o_cb_r, s_b = retention_scan(flip(q_c), flip(k_c), flip(v_c), g_b, zero)
        o_ctx = o_cf + flip(o_cb_r)
    else:
        s_f = retention_state(k_c, v_c, g_f)
        s_b = retention_state(flip(k_c), flip(v_c), g_b)
        o_ctx = None
    o_lf, _ = retention_scan(q_l, k_l, v_l, g_f, s_f)
    o_lb_r, _ = retention_scan(flip(q_l), flip(k_l), flip(v_l), g_b, s_b)
    return o_lf + flip(o_lb_r), o_ctx


def to_heads(t, head_dim):
    bsz, length, _ = t.shape
    return jnp.transpose(t.reshape(bsz, length, RET_HEADS, head_dim), (0, 2, 1, 3)).astype(jnp.float32)


def hybrid_mixer(h_lat, h_ctx, rows, cols, w_in, sgu_ln_g, sgu_ln_b, sgu_w, sgu_b, log_decay,
                 w_proj_a, w_proj_r, w_out, ctx_out):
    k_scale = RET_QK_DIM ** -0.5
    p_lat = h_lat @ w_in
    q_l = axial_rope(to_heads(p_lat[..., Q0:K0], RET_QK_DIM), rows, cols)
    k_l = axial_rope(to_heads(p_lat[..., K0:V0], RET_QK_DIM), rows, cols) * k_scale
    v_l = to_heads(p_lat[..., V0:G0], RET_V_DIM)
    if ctx_out:
        p_ctx = h_ctx @ w_in
        q_c = to_heads(p_ctx[..., Q0:K0], RET_QK_DIM)
        kv_c = p_ctx[..., K0:G0]
    else:
        p_ctx = None
        q_c = None
        kv_c = h_ctx @ w_in[:, K0:G0]
    k_c = to_heads(kv_c[..., :RET_QK], RET_QK_DIM) * k_scale
    v_c = to_heads(kv_c[..., RET_QK:], RET_V_DIM)
    o_lat, o_ctx = bidir_retention(q_c, k_c, v_c, q_l, k_l, v_l, log_decay, ctx_out)

    def merge(p, o_ret):
        bsz, length, _ = p.shape
        ret = jnp.transpose(layer_norm(o_ret), (0, 2, 1, 3)).reshape(bsz, length, RET_V).astype(p.dtype)
        ret = ret * jax.nn.silu(p[..., G0:UA0])
        u = jax.nn.gelu(p[..., UA0:VA0])
        v = jax.nn.gelu(p[..., VA0:GA0])
        sgu = spatial_gating(u, v, sgu_ln_g, sgu_ln_b, sgu_w, sgu_b)
        y = (jax.nn.sigmoid(p[..., GA0:GR0]) * (sgu @ w_proj_a)
             + jax.nn.sigmoid(p[..., GR0:]) * (ret @ w_proj_r))
        return y @ w_out

    out_lat = merge(p_lat, o_lat)
    out_ctx = merge(p_ctx, o_ctx) if ctx_out else None
    return out_lat, out_ctx


def swiglu(h, w1, w3, w2):
    return (jax.nn.silu(h @ w1) * (h @ w3)) @ w2


def moe_swiglu(h, router_w, router_b, w1, w3, w2):
    logits = (h @ router_w).astype(jnp.float32) + router_b.astype(jnp.float32)
    top_v, top_i = lax.top_k(logits, TOP_K)
    probs = jax.nn.softmax(top_v, axis=-1)
    gates = jnp.sum(jax.nn.one_hot(top_i, N_EXPERTS, dtype=jnp.float32) * probs[..., None], axis=-2)
    gates = gates.astype(h.dtype)
    y = jnp.zeros_like(h)
    for e in range(N_EXPERTS):
        y = y + gates[..., e:e + 1] * swiglu(h, w1[e], w3[e], w2[e])
    return y


def setup_inputs(seed: int = 0) -> dict:
    key = jax.random.key(seed)
    ks = jax.random.split(key, 26)
    f32 = jnp.float32

    def nrm(k, shape, scale):
        return jax.random.normal(k, shape, f32) * scale

    base = jnp.log1p(-(2.0 ** (-5.0 - jnp.arange(RET_HEADS, dtype=f32))))
    ret_log_decay = base * jnp.exp(0.1 * jax.random.normal(ks[13], (DEPTH, 2, RET_HEADS), f32))
    return {
        'x': nrm(ks[0], (BATCH, SEQ, D_MODEL), 1.0),
        'c': nrm(ks[1], (BATCH, D_MODEL), 1.0),
        'ctx': nrm(ks[2], (BATCH, CTX_LEN, D_MODEL), 1.0),
        'c_ctx': nrm(ks[3], (D_MODEL,), 1.0),
        'w_mod': nrm(ks[4], (DEPTH, D_MODEL, N_MOD * D_MODEL), 0.5 * D_MODEL ** -0.5),
        'b_mod': nrm(ks[5], (DEPTH, N_MOD * D_MODEL), 0.02),
        'norm1_g': 1.0 + nrm(ks[6], (DEPTH, D_MODEL), 0.02),
        'norm2_g': 1.0 + nrm(ks[7], (DEPTH, D_MODEL), 0.02),
        'w_in': nrm(ks[8], (DEPTH, D_MODEL, IN_COLS), D_MODEL ** -0.5),
        'sgu_ln_g': 1.0 + nrm(ks[9], (DEPTH, SGU_WIDTH), 0.02),
        'sgu_ln_b': nrm(ks[10], (DEPTH, SGU_WIDTH), 0.02),
        'sgu_w': nrm(ks[11], (DEPTH, SGU_GROUPS, CHUNK, CHUNK), CHUNK ** -0.5),
        'sgu_b': 1.0 + nrm(ks[12], (DEPTH, SGU_GROUPS, CHUNK), 0.02),
        'ret_log_decay': ret_log_decay,
        'w_proj_a': nrm(ks[14], (DEPTH, SGU_WIDTH, D_MODEL), SGU_WIDTH ** -0.5),
        'w_proj_r': nrm(ks[15], (DEPTH, RET_V, D_MODEL), RET_V ** -0.5),
        'w_out': nrm(ks[16], (DEPTH, D_MODEL, D_MODEL), D_MODEL ** -0.5),
        'ffn_w1': nrm(ks[17], (N_DENSE, D_MODEL, D_FF), D_MODEL ** -0.5),
        'ffn_w3': nrm(ks[18], (N_DENSE, D_MODEL, D_FF), D_MODEL ** -0.5),
        'ffn_w2': nrm(ks[19], (N_DENSE, D_FF, D_MODEL), D_FF ** -0.5),
        'router_w': nrm(ks[20], (N_MOE, D_MODEL, N_EXPERTS), D_MODEL ** -0.5),
        'router_b': nrm(ks[21], (N_MOE, N_EXPERTS), 0.01),
        'moe_w1': nrm(ks[22], (N_MOE, N_EXPERTS, D_MODEL, D_FF_EXPERT), D_MODEL ** -0.5),
        'moe_w3': nrm(ks[23], (N_MOE, N_EXPERTS, D_MODEL, D_FF_EXPERT), D_MODEL ** -0.5),
        'moe_w2': nrm(ks[24], (N_MOE, N_EXPERTS, D_FF_EXPERT, D_MODEL), D_FF_EXPERT ** -0.5),
        'final_norm_g': 1.0 + nrm(ks[25], (D_MODEL,), 0.02),
    }


def reference(x, c, ctx, c_ctx, w_mod, b_mod, norm1_g, norm2_g, w_in, sgu_ln_g, sgu_ln_b, sgu_w, sgu_b,
              ret_log_decay, w_proj_a, w_proj_r, w_out, ffn_w1, ffn_w3, ffn_w2, router_w, router_b,
              moe_w1, moe_w3, moe_w2, final_norm_g):
    n_lat = x.shape[1]
    t = jnp.arange(n_lat)
    rows = t // GRID_W
    cols = t % GRID_W
    sc = jax.nn.silu(c)
    sc_ctx = jax.nn.silu(c_ctx)
    for layer in range(DEPTH):
        last = layer == DEPTH - 1
        m = [t_[:, None, :] for t_ in jnp.split(sc @ w_mod[layer] + b_mod[layer], N_MOD, axis=-1)]
        mc = jnp.split(sc_ctx @ w_mod[layer] + b_mod[layer], N_MOD, axis=-1)

        h_lat = modulate(x, norm1_g[layer], m[0], m[1])
        h_ctx = modulate(ctx, norm1_g[layer], mc[0], mc[1])
        out_lat, out_ctx = hybrid_mixer(h_lat, h_ctx, rows, cols, w_in[layer], sgu_ln_g[layer],
                                        sgu_ln_b[layer], sgu_w[layer], sgu_b[layer], ret_log_decay[layer],
                                        w_proj_a[layer], w_proj_r[layer], w_out[layer], not last)
        x = x + m[2] * out_lat
        h2 = modulate(x, norm2_g[layer], m[3], m[4])
        if not last:
            ctx = ctx + mc[2] * out_ctx
            h2 = jnp.concatenate([modulate(ctx, norm2_g[layer], mc[3], mc[4]), h2], axis=1)

        if layer % 2 == 0:
            i = layer // 2
            f = swiglu(h2, ffn_w1[i], ffn_w3[i], ffn_w2[i])
        else:
            i = layer // 2
            f = moe_swiglu(h2, router_w[i], router_b[i], moe_w1[i], moe_w3[i], moe_w2[i])

        if not last:
            n_ctx = ctx.shape[1]
            ctx = ctx + mc[5] * f[:, :n_ctx]
            f = f[:, n_ctx:]
        x = x + m[5] * f
    return rms_norm(x, final_norm_g)
```

```python
import functools

import jax
import jax.numpy as jnp
from jax import lax
from jax.experimental import pallas as pl
from jax.experimental.pallas import tpu as pltpu

F32 = jnp.float32
BF16 = jnp.bfloat16

GRID_W = 64
CHUNK = 128
SGU_GROUPS = 8
SGU_WIDTH = 2048
RET_HEADS = 8
RET_QK_DIM = 256
RET_V_DIM = 512
RET_QK = RET_HEADS * RET_QK_DIM
RET_V = RET_HEADS * RET_V_DIM
ROPE_BASE = 10000.0
N_EXPERTS = 8
N_MOD = 6
NORM_EPS = 1e-6
MOD_ROWS = 8
LANES = 128

VMEM_LIMIT_BYTES = 52 * 1024 * 1024

Q0 = 0
K0 = Q0 + RET_QK
V0 = K0 + RET_QK
G0 = V0 + RET_V
UA0 = G0 + RET_V
VA0 = UA0 + SGU_WIDTH
GA0 = VA0 + SGU_WIDTH
GR0 = GA0 + 2048
IN_COLS = GR0 + 2048


def _params(n_grid_dims):
    return pltpu.CompilerParams(
        dimension_semantics=("arbitrary",) * n_grid_dims,
        vmem_limit_bytes=VMEM_LIMIT_BYTES,
    )


def _gelu_tanh(x):
    return 0.5 * x * (1.0 + jnp.tanh(0.7978845608028654 * (x + 0.044715 * (x * x * x))))


def _silu(x):
    return x * jax.nn.sigmoid(x)


def _mod_kernel(c_ref, w_ref, b_ref, o_ref):
    sc = _silu(c_ref[...]).astype(BF16)
    o_ref[...] = jnp.dot(sc, w_ref[...].astype(BF16), preferred_element_type=F32) + b_ref[...]


def _modulation(cvec, w_mod, b_mod):
    depth, d, n = w_mod.shape
    tn = 1024
    return pl.pallas_call(
        _mod_kernel,
        grid=(depth, n // tn),
        in_specs=[
            pl.BlockSpec((MOD_ROWS, d), lambda l, j: (0, 0)),
            pl.BlockSpec((None, d, tn), lambda l, j: (l, 0, j)),
            pl.BlockSpec((None, 1, tn), lambda l, j: (l, 0, j)),
        ],
        out_specs=pl.BlockSpec((None, MOD_ROWS, tn), lambda l, j: (l, 0, j)),
        out_shape=jax.ShapeDtypeStruct((depth, MOD_ROWS, n), F32),
        compiler_params=_params(2),
        name="modulation",
    )(cvec, w_mod, b_mod.reshape(depth, 1, n))


def _norm_mod_kernel(x_ref, g_ref, shift_ref, scale_ref, o_ref):
    x = x_ref[...]
    y = x * lax.rsqrt(jnp.mean(x * x, axis=-1, keepdims=True) + NORM_EPS) * g_ref[...]
    o_ref[...] = (y * (1.0 + scale_ref[0]) + shift_ref[0]).astype(o_ref.dtype)


def _norm_kernel(x_ref, g_ref, o_ref):
    x = x_ref[...]
    y = x * lax.rsqrt(jnp.mean(x * x, axis=-1, keepdims=True) + NORM_EPS) * g_ref[...]
    o_ref[...] = y.astype(o_ref.dtype)


class _Layout:
    def __init__(self, batch, seq, ctx_len):
        self.batch, self.seq, self.ctx_len = batch, seq, ctx_len
        self.tm = batch * ctx_len
        assert seq % self.tm == 0 and ctx_len % CHUNK == 0 and seq % CHUNK == 0
        self.tiles_per_batch = seq // self.tm
        self.n_tiles = 1 + batch * self.tiles_per_batch
        self.rows = self.n_tiles * self.tm
        self.ctx_chunks = ctx_len // CHUNK
        self.lat_chunks = seq // CHUNK

    def mod_row(self, tile):
        return jnp.where(tile == 0, self.batch, (tile - 1) // self.tiles_per_batch)


def _mod_index(lay, layer, k, tile):
    return layer * MOD_ROWS * N_MOD + lay.mod_row(tile) * N_MOD + k


def _norm_modulate(lay, x_all, g, mod3, layer, k_shift, k_scale, out_dtype):
    d = x_all.shape[1]
    return pl.pallas_call(
        _norm_mod_kernel,
        grid=(lay.n_tiles,),
        in_specs=[
            pl.BlockSpec((lay.tm, d), lambda i: (i, 0)),
            pl.BlockSpec((1, d), lambda i: (0, 0)),
            pl.BlockSpec((1, 1, d), lambda i: (_mod_index(lay, layer, k_shift, i), 0, 0)),
            pl.BlockSpec((1, 1, d), lambda i: (_mod_index(lay, layer, k_scale, i), 0, 0)),
        ],
        out_specs=pl.BlockSpec((lay.tm, d), lambda i: (i, 0)),
        out_shape=jax.ShapeDtypeStruct((lay.rows, d), out_dtype),
        compiler_params=_params(1),
        name="norm_modulate",
    )(x_all, g.reshape(1, d), mod3, mod3)


def _final_norm(lay, x_all, g):
    d = x_all.shape[1]
    n_lat_tiles = lay.n_tiles - 1
    return pl.pallas_call(
        _norm_kernel,
        grid=(n_lat_tiles,),
        in_specs=[
            pl.BlockSpec((lay.tm, d), lambda i: (i + 1, 0)),
            pl.BlockSpec((1, d), lambda i: (0, 0)),
        ],
        out_specs=pl.BlockSpec((lay.tm, d), lambda i: (i, 0)),
        out_shape=jax.ShapeDtypeStruct((n_lat_tiles * lay.tm, d), F32),
        compiler_params=_params(1),
        name="final_norm",
    )(x_all, g.reshape(1, d))


def _in_proj_kernel(h_ref, w_ref, cos_ref, sin_ref, o_ref, *, tn):
    j = pl.program_id(0)
    col = j * tn
    acc = jnp.dot(h_ref[...], w_ref[...], preferred_element_type=F32)

    def rope(scale):
        for s in range(tn // LANES):
            t = (s % 2) * LANES
            xs = acc[:, s * LANES:(s + 1) * LANES]
            r = xs * cos_ref[:, t:t + LANES] + pltpu.roll(xs, LANES // 2, 1) * sin_ref[:, t:t + LANES]
            if scale != 1.0:
                r = r * scale
            o_ref[:, s * LANES:(s + 1) * LANES] = r.astype(o_ref.dtype)

    @pl.when(col < K0)
    def _():
        rope(1.0)

    @pl.when((col >= K0) & (col < V0))
    def _():
        rope(RET_QK_DIM ** -0.5)

    @pl.when((col >= V0) & (col < G0))
    def _():
        o_ref[...] = acc.astype(o_ref.dtype)

    @pl.when((col >= G0) & (col < UA0))
    def _():
        o_ref[...] = _silu(acc).astype(o_ref.dtype)

    @pl.when((col >= UA0) & (col < GA0))
    def _():
        o_ref[...] = _gelu_tanh(acc).astype(o_ref.dtype)

    @pl.when(col >= GA0)
    def _():
        o_ref[...] = jax.nn.sigmoid(acc).astype(o_ref.dtype)


def _in_proj(lay, h, w_in, layer, cos_t, sin_t):
    d = h.shape[1]
    n = w_in.shape[2]
    tn = 2048
    n_rope_tiles = V0 // tn
    tab = lambda j, i: (jnp.where(j < n_rope_tiles, i, 0), 0)
    return pl.pallas_call(
        functools.partial(_in_proj_kernel, tn=tn),
        grid=(n // tn, lay.n_tiles),
        in_specs=[
            pl.BlockSpec((lay.tm, d), lambda j, i: (i, 0)),
            pl.BlockSpec((None, d, tn), lambda j, i: (layer, 0, j)),
            pl.BlockSpec((lay.tm, RET_QK_DIM), tab),
            pl.BlockSpec((lay.tm, RET_QK_DIM), tab),
        ],
        out_specs=pl.BlockSpec((lay.tm, tn), lambda j, i: (i, j)),
        out_shape=jax.ShapeDtypeStruct((lay.rows, n), BF16),
        compiler_params=_params(2),
        name="in_proj",
    )(h, w_in, cos_t, sin_t)


def _retention_kernel(lg_ref, q_ref, k_ref, v_ref, *rest, reverse, final):
    if final:
        other_ref, gate_ref, o_ref, s_ref = rest
    else:
        o_ref, s_ref = rest

    @pl.when(pl.program_id(1) == 0)
    def _():
        s_ref[...] = jnp.zeros_like(s_ref)

    c = CHUNK
    qi = lax.broadcasted_iota(jnp.int32, (c, c), 0)
    kj = lax.broadcasted_iota(jnp.int32, (c, c), 1)
    diff = ((kj - qi) if reverse else (qi - kj)).astype(F32)
    pos = lax.broadcasted_iota(jnp.int32, (c, 1), 0).astype(F32)
    q_steps = (c - pos) if reverse else (pos + 1.0)
    k_steps = pos if reverse else (c - 1.0 - pos)

    for h in range(RET_HEADS):
        log_g = lg_ref[h]
        intra = jnp.where(diff >= 0, jnp.exp(jnp.maximum(diff, 0.0) * log_g), 0.0)
        q_decay = jnp.exp(q_steps * log_g)
        k_decay = jnp.exp(k_steps * log_g)
        chunk_decay = jnp.exp(jnp.full((1, 1), float(c), F32) * log_g)

        qh = q_ref[:, h * RET_QK_DIM:(h + 1) * RET_QK_DIM]
        kh = k_ref[:, h * RET_QK_DIM:(h + 1) * RET_QK_DIM]
        vh = v_ref[:, h * RET_V_DIM:(h + 1) * RET_V_DIM]
        s = s_ref[h]

        scores = lax.dot_general(qh, kh, (((1,), (1,)), ((), ())), preferred_element_type=F32) * intra
        o = (jnp.dot(scores.astype(BF16), vh, preferred_element_type=F32)
             + jnp.dot(qh, s.astype(BF16), preferred_element_type=F32) * q_decay)
        k_dec_t = (kh.astype(F32) * k_decay).T.astype(BF16)
        s_ref[h] = s * chunk_decay + jnp.dot(k_dec_t, vh, preferred_element_type=F32)

        sl = slice(h * RET_V_DIM, (h + 1) * RET_V_DIM)
        if final:
            o = o + other_ref[:, sl]
            mu = jnp.mean(o, axis=-1, keepdims=True)
            oc = o - mu
            var = jnp.mean(oc * oc, axis=-1, keepdims=True)
            o = oc * lax.rsqrt(var + NORM_EPS) * gate_ref[:, sl].astype(F32)
        o_ref[:, sl] = o.astype(o_ref.dtype)


def _retention_pass(lay, p, log_g, other, *, reverse):
    final = other is not None
    n_chunks = lay.ctx_chunks + lay.lat_chunks

    def row_block(b, n):
        ctx_blk = b * lay.ctx_chunks + ((lay.ctx_chunks - 1 - n) if reverse else n)
        m = n - lay.ctx_chunks
        lat_blk = lay.batch * lay.ctx_chunks + b * lay.lat_chunks + ((lay.lat_chunks - 1 - m) if reverse else m)
        return jnp.where(n < lay.ctx_chunks, ctx_blk, lat_blk)

    in_specs = [
        pl.BlockSpec((CHUNK, RET_QK), lambda b, n, lg: (row_block(b, n), Q0 // RET_QK)),
        pl.BlockSpec((CHUNK, RET_QK), lambda b, n, lg: (row_block(b, n), K0 // RET_QK)),
        pl.BlockSpec((CHUNK, RET_V), lambda b, n, lg: (row_block(b, n), V0 // RET_V)),
    ]
    args = [p, p, p]
    if final:
        in_specs += [
            pl.BlockSpec((CHUNK, RET_V), lambda b, n, lg: (row_block(b, n), 0)),
            pl.BlockSpec((CHUNK, RET_V), lambda b, n, lg: (row_block(b, n), G0 // RET_V)),
        ]
        args += [other, p]
    return pl.pallas_call(
        functools.partial(_retention_kernel, reverse=reverse, final=final),
        grid_spec=pltpu.PrefetchScalarGridSpec(
            num_scalar_prefetch=1,
            grid=(lay.batch, n_chunks),
            in_specs=in_specs,
            out_specs=pl.BlockSpec((CHUNK, RET_V), lambda b, n, lg: (row_block(b, n), 0)),
            scratch_shapes=[pltpu.VMEM((RET_HEADS, RET_QK_DIM, RET_V_DIM), F32)],
        ),
        out_shape=jax.ShapeDtypeStruct((lay.rows, RET_V), BF16 if final else F32),
        compiler_params=_params(2),
        name="retention_fwd_merge" if final else "retention_bwd",
    )(log_g, *args)


def _sgu_kernel(u_ref, v_ref, lng_ref, lnb_ref, ws_ref, bst_ref, o_ref):
    v = v_ref[...].astype(F32)
    mu = jnp.mean(v, axis=-1, keepdims=True)
    vc = v - mu
    var = jnp.mean(vc * vc, axis=-1, keepdims=True)
    vn = (vc * lax.rsqrt(var + NORM_EPS) * lng_ref[...] + lnb_ref[...]).astype(BF16)
    gw = SGU_WIDTH // SGU_GROUPS
    for g in range(SGU_GROUPS):
        sl = slice(g * gw, (g + 1) * gw)
        s = jnp.dot(ws_ref[g], vn[:, sl], preferred_element_type=F32) + bst_ref[:, g:g + 1]
        o_ref[:, sl] = (u_ref[:, sl].astype(F32) * s).astype(o_ref.dtype)


def _sgu(lay, p, ln_g, ln_b, w_s, b_s_t):
    return pl.pallas_call(
        _sgu_kernel,
        grid=(lay.rows // CHUNK,),
        in_specs=[
            pl.BlockSpec((CHUNK, SGU_WIDTH), lambda i: (i, UA0 // SGU_WIDTH)),
            pl.BlockSpec((CHUNK, SGU_WIDTH), lambda i: (i, VA0 // SGU_WIDTH)),
            pl.BlockSpec((1, SGU_WIDTH), lambda i: (0, 0)),
            pl.BlockSpec((1, SGU_WIDTH), lambda i: (0, 0)),
            pl.BlockSpec((SGU_GROUPS, CHUNK, CHUNK), lambda i: (0, 0, 0)),
            pl.BlockSpec((CHUNK, SGU_GROUPS), lambda i: (0, 0)),
        ],
        out_specs=pl.BlockSpec((CHUNK, SGU_WIDTH), lambda i: (i, 0)),
        out_shape=jax.ShapeDtypeStruct((lay.rows, SGU_WIDTH), BF16),
        compiler_params=_params(1),
        name="sgu",
    )(p, p, ln_g.reshape(1, -1), ln_b.reshape(1, -1), w_s, b_s_t)


def _merge_kernel(sgu_ref, ret_ref, wa_ref, wr_ref, ga_ref, gr_ref, o_ref):
    a = jnp.dot(sgu_ref[...], wa_ref[...], preferred_element_type=F32)
    r = jnp.dot(ret_ref[...], wr_ref[...], preferred_element_type=F32)
    o_ref[...] = (ga_ref[...].astype(F32) * a + gr_ref[...].astype(F32) * r).astype(o_ref.dtype)


def _merge(lay, sgu, ret, p, w_proj_a, w_proj_r, layer):
    n = w_proj_a.shape[2]
    tn = 512
    return pl.pallas_call(
        _merge_kernel,
        grid=(n // tn, lay.n_tiles),
        in_specs=[
            pl.BlockSpec((lay.tm, SGU_WIDTH), lambda j, i: (i, 0)),
            pl.BlockSpec((lay.tm, RET_V), lambda j, i: (i, 0)),
            pl.BlockSpec((None, SGU_WIDTH, tn), lambda j, i: (layer, 0, j)),
            pl.BlockSpec((None, RET_V, tn), lambda j, i: (layer, 0, j)),
            pl.BlockSpec((lay.tm, tn), lambda j, i: (i, GA0 // tn + j)),
            pl.BlockSpec((lay.tm, tn), lambda j, i: (i, GR0 // tn + j)),
        ],
        out_specs=pl.BlockSpec((lay.tm, tn), lambda j, i: (i, j)),
        out_shape=jax.ShapeDtypeStruct((lay.rows, n), BF16),
        compiler_params=_params(2),
        name="merge",
    )(sgu, ret, w_proj_a, w_proj_r, p, p)


def _proj_resid_kernel(a_ref, w_ref, resid_ref, gate_ref, o_ref):
    acc = jnp.dot(a_ref[...], w_ref[...], preferred_element_type=F32)
    o_ref[...] = resid_ref[...] + gate_ref[0] * acc


def _proj_rowgate_kernel(a_ref, w_ref, resid_ref, gate_ref, o_ref):
    acc = jnp.dot(a_ref[...], w_ref[...], preferred_element_type=F32)
    o_ref[...] = resid_ref[...] + gate_ref[...] * acc


def _proj_resid(lay, a, w, w_index, resid, mod3, layer, k_gate):
    kdim = a.shape[1]
    n = w.shape[-1]
    tn = 512
    w_spec = pl.BlockSpec((None,) * len(w_index) + (kdim, tn), lambda j, i: (*w_index, 0, j))
    return pl.pallas_call(
        _proj_resid_kernel,
        grid=(n // tn, lay.n_tiles),
        in_specs=[
            pl.BlockSpec((lay.tm, kdim), lambda j, i: (i, 0)),
            w_spec,
            pl.BlockSpec((lay.tm, tn), lambda j, i: (i, j)),
            pl.BlockSpec((1, 1, tn), lambda j, i: (_mod_index(lay, layer, k_gate, i), 0, j)),
        ],
        out_specs=pl.BlockSpec((lay.tm, tn), lambda j, i: (i, j)),
        out_shape=jax.ShapeDtypeStruct((lay.rows, n), F32),
        compiler_params=_params(2),
        name="proj_resid",
    )(a, w, resid, mod3)


def _proj_rowgate(lay, a, w, w_index, resid, row_gate):
    kdim = a.shape[1]
    n = w.shape[-1]
    tn = 512
    w_spec = pl.BlockSpec((None,) * len(w_index) + (kdim, tn), lambda j, i: (*w_index, 0, j))
    return pl.pallas_call(
        _proj_rowgate_kernel,
        grid=(n // tn, lay.n_tiles),
        in_specs=[
            pl.BlockSpec((lay.tm, kdim), lambda j, i: (i, 0)),
            w_spec,
            pl.BlockSpec((lay.tm, tn), lambda j, i: (i, j)),
            pl.BlockSpec((lay.tm, 1), lambda j, i: (i, 0)),
        ],
        out_specs=pl.BlockSpec((lay.tm, tn), lambda j, i: (i, j)),
        out_shape=jax.ShapeDtypeStruct((lay.rows, n), F32),
        compiler_params=_params(2),
        name="proj_rowgate",
    )(a, w, resid, row_gate)


def _scale_add_kernel(x_ref, f_ref, gate_ref, o_ref):
    o_ref[...] = x_ref[...] + gate_ref[0] * f_ref[...]


def _scale_add(lay, x_all, f, mod3, layer, k_gate):
    d = x_all.shape[1]
    row = pl.BlockSpec((lay.tm, d), lambda i: (i, 0))
    return pl.pallas_call(
        _scale_add_kernel,
        grid=(lay.n_tiles,),
        in_specs=[row, row, pl.BlockSpec((1, 1, d), lambda i: (_mod_index(lay, layer, k_gate, i), 0, 0))],
        out_specs=row,
        out_shape=jax.ShapeDtypeStruct((lay.rows, d), F32),
        compiler_params=_params(1),
        name="scale_add",
    )(x_all, f, mod3)


def _swiglu_up_kernel(h_ref, w1_ref, w3_ref, o_ref):
    h = h_ref[...]
    a = jnp.dot(h, w1_ref[...], preferred_element_type=F32)
    b = jnp.dot(h, w3_ref[...], preferred_element_type=F32)
    o_ref[...] = (_silu(a) * b).astype(o_ref.dtype)


def _swiglu_up(lay, h, w1, w3, w_index):
    d = h.shape[1]
    f = w1.shape[-1]
    tn = 512
    w_spec = pl.BlockSpec((None,) * len(w_index) + (d, tn), lambda j, i: (*w_index, 0, j))
    return pl.pallas_call(
        _swiglu_up_kernel,
        grid=(f // tn, lay.n_tiles),
        in_specs=[pl.BlockSpec((lay.tm, d), lambda j, i: (i, 0)), w_spec, w_spec],
        out_specs=pl.BlockSpec((lay.tm, tn), lambda j, i: (i, j)),
        out_shape=jax.ShapeDtypeStruct((lay.rows, f), BF16),
        compiler_params=_params(2),
        name="swiglu_up",
    )(h, w1, w3)


def _router_kernel(h_ref, w_ref, b_ref, o_ref):
    logits = jnp.dot(h_ref[...], w_ref[...], preferred_element_type=F32) + b_ref[...]
    lane = lax.broadcasted_iota(jnp.int32, logits.shape, 1)
    neg = jnp.float32(-jnp.inf)
    lg = jnp.where(lane < N_EXPERTS, logits, neg)
    m1 = jnp.max(lg, axis=-1, keepdims=True)
    i1 = jnp.min(jnp.where(lg == m1, lane, LANES), axis=-1, keepdims=True)
    lg2 = jnp.where(lane == i1, neg, lg)
    m2 = jnp.max(lg2, axis=-1, keepdims=True)
    i2 = jnp.min(jnp.where(lg2 == m2, lane, LANES), axis=-1, keepdims=True)
    e2 = jnp.exp(m2 - m1)
    den = 1.0 + e2
    o_ref[...] = jnp.where(lane == i1, 1.0 / den, 0.0) + jnp.where(lane == i2, e2 / den, 0.0)


def _router(lay, h, router_w, router_b):
    d = h.shape[1]
    w = jnp.zeros((d, LANES), BF16).at[:, :N_EXPERTS].set(router_w.astype(BF16))
    b = jnp.zeros((1, LANES), F32).at[0, :N_EXPERTS].set(router_b.astype(F32))
    return pl.pallas_call(
        _router_kernel,
        grid=(lay.n_tiles,),
        in_specs=[
            pl.BlockSpec((lay.tm, d), lambda i: (i, 0)),
            pl.BlockSpec((d, LANES), lambda i: (0, 0)),
            pl.BlockSpec((1, LANES), lambda i: (0, 0)),
        ],
        out_specs=pl.BlockSpec((lay.tm, LANES), lambda i: (i, 0)),
        out_shape=jax.ShapeDtypeStruct((lay.rows, LANES), F32),
        compiler_params=_params(1),
        name="router",
    )(h, w, b)


def _rope_tables(lay):
    t = jnp.arange(lay.seq)
    quarter = RET_QK_DIM // 4
    freqs = ROPE_BASE ** (-jnp.arange(quarter, dtype=F32) / quarter)

    def tables(pos):
        ang = pos.astype(F32)[:, None] * freqs[None, :]
        cos, sin = jnp.cos(ang), jnp.sin(ang)
        return jnp.concatenate([cos, cos], axis=-1), jnp.concatenate([-sin, sin], axis=-1)

    cos_r, sin_r = tables(t // GRID_W)
    cos_c, sin_c = tables(t % GRID_W)
    cos_l = jnp.concatenate([cos_r, cos_c], axis=-1)
    sin_l = jnp.concatenate([sin_r, sin_c], axis=-1)
    n_ctx = lay.batch * lay.ctx_len
    cos_t = jnp.concatenate([jnp.ones((n_ctx, RET_QK_DIM), F32)] + [cos_l] * lay.batch, axis=0)
    sin_t = jnp.concatenate([jnp.zeros((n_ctx, RET_QK_DIM), F32)] + [sin_l] * lay.batch, axis=0)
    return cos_t, sin_t


def kernel(x, c, ctx, c_ctx, w_mod, b_mod, norm1_g, norm2_g, w_in, sgu_ln_g, sgu_ln_b, sgu_w, sgu_b,
           ret_log_decay, w_proj_a, w_proj_r, w_out, ffn_w1, ffn_w3, ffn_w2, router_w, router_b,
           moe_w1, moe_w3, moe_w2, final_norm_g):
    batch, seq, d = x.shape
    ctx_len = ctx.shape[1]
    depth = w_mod.shape[0]
    lay = _Layout(batch, seq, ctx_len)

    cvec = jnp.zeros((MOD_ROWS, d), F32).at[:batch].set(c).at[batch].set(c_ctx)
    mod = _modulation(cvec, w_mod, b_mod)
    mod3 = mod.reshape(depth * MOD_ROWS * N_MOD, 1, d)

    cos_t, sin_t = _rope_tables(lay)
    x_all = jnp.concatenate([ctx.reshape(batch * ctx_len, d), x.reshape(batch * seq, d)], axis=0)

    w_in_b = w_in.astype(BF16)
    w_proj_a_b = w_proj_a.astype(BF16)
    w_proj_r_b = w_proj_r.astype(BF16)
    w_out_b = w_out.astype(BF16)
    sgu_w_b = sgu_w.astype(BF16)
    ffn_w1_b, ffn_w3_b, ffn_w2_b = ffn_w1.astype(BF16), ffn_w3.astype(BF16), ffn_w2.astype(BF16)
    moe_w1_b, moe_w3_b, moe_w2_b = moe_w1.astype(BF16), moe_w3.astype(BF16), moe_w2.astype(BF16)
    log_decay = ret_log_decay.astype(F32)

    for layer in range(depth):
        h = _norm_modulate(lay, x_all, norm1_g[layer], mod3, layer, 0, 1, BF16)
        p = _in_proj(lay, h, w_in_b, layer, cos_t, sin_t)
        o_bwd = _retention_pass(lay, p, log_decay[layer, 1], None, reverse=True)
        ret = _retention_pass(lay, p, log_decay[layer, 0], o_bwd, reverse=False)
        sgu = _sgu(lay, p, sgu_ln_g[layer], sgu_ln_b[layer], sgu_w_b[layer], sgu_b[layer].T)
        y = _merge(lay, sgu, ret, p, w_proj_a_b, w_proj_r_b, layer)
        x_all = _proj_resid(lay, y, w_out_b, (layer,), x_all, mod3, layer, 2)
        h2 = _norm_modulate(lay, x_all, norm2_g[layer], mod3, layer, 3, 4, BF16)
        i = layer // 2
        if layer % 2 == 0:
            act = _swiglu_up(lay, h2, ffn_w1_b, ffn_w3_b, (i,))
            x_all = _proj_resid(lay, act, ffn_w2_b, (i,), x_all, mod3, layer, 5)
        else:
            gates = _router(lay, h2, router_w[i], router_b[i])
            f = jnp.zeros_like(x_all)
            for e in range(N_EXPERTS):
                act = _swiglu_up(lay, h2, moe_w1_b, moe_w3_b, (i, e))
                f = _proj_rowgate(lay, act, moe_w2_b, (i, e), f, gates[:, e:e + 1])
            x_all = _scale_add(lay, x_all, f, mod3, layer, 5)

    out = _final_norm(lay, x_all, final_norm_g)
    return out.reshape(batch, seq, d)
```

```python
import functools

import jax
import jax.numpy as jnp
from jax import lax
from jax.experimental import pallas as pl
from jax.experimental.pallas import tpu as pltpu

F32 = jnp.float32
BF16 = jnp.bfloat16

GRID_W = 64
CHUNK = 128
SGU_GROUPS = 8
SGU_WIDTH = 2048
RET_HEADS = 8
RET_QK_DIM = 256
RET_V_DIM = 512
RET_QK = RET_HEADS * RET_QK_DIM
RET_V = RET_HEADS * RET_V_DIM
ROPE_BASE = 10000.0
N_EXPERTS = 8
N_MOD = 6
NORM_EPS = 1e-6
MOD_ROWS = 8
LANES = 128
MOE_TILE = 512
DMA_LAG = 32

VMEM_LIMIT_BYTES = 52 * 1024 * 1024

Q0 = 0
K0 = Q0 + RET_QK
V0 = K0 + RET_QK
G0 = V0 + RET_V
UA0 = G0 + RET_V
VA0 = UA0 + SGU_WIDTH
GA0 = VA0 + SGU_WIDTH
GR0 = GA0 + 2048
IN_COLS = GR0 + 2048


def _params(n_grid_dims):
    return pltpu.CompilerParams(
        dimension_semantics=("arbitrary",) * n_grid_dims,
        vmem_limit_bytes=VMEM_LIMIT_BYTES,
    )


def _gelu_tanh(x):
    return 0.5 * x * (1.0 + jnp.tanh(0.7978845608028654 * (x + 0.044715 * (x * x * x))))


def _silu(x):
    return x * jax.nn.sigmoid(x)


def _mod_kernel(c_ref, w_ref, b_ref, o_ref):
    sc = _silu(c_ref[...]).astype(BF16)
    o_ref[...] = jnp.dot(sc, w_ref[...].astype(BF16), preferred_element_type=F32) + b_ref[...]


def _modulation(cvec, w_mod, b_mod):
    depth, d, n = w_mod.shape
    tn = 1024
    return pl.pallas_call(
        _mod_kernel,
        grid=(depth, n // tn),
        in_specs=[
            pl.BlockSpec((MOD_ROWS, d), lambda l, j: (0, 0)),
            pl.BlockSpec((None, d, tn), lambda l, j: (l, 0, j)),
            pl.BlockSpec((None, 1, tn), lambda l, j: (l, 0, j)),
        ],
        out_specs=pl.BlockSpec((None, MOD_ROWS, tn), lambda l, j: (l, 0, j)),
        out_shape=jax.ShapeDtypeStruct((depth, MOD_ROWS, n), F32),
        compiler_params=_params(2),
        name="modulation",
    )(cvec, w_mod, b_mod.reshape(depth, 1, n))


def _norm_mod_kernel(x_ref, g_ref, shift_ref, scale_ref, o_ref):
    x = x_ref[...]
    y = x * lax.rsqrt(jnp.mean(x * x, axis=-1, keepdims=True) + NORM_EPS) * g_ref[...]
    o_ref[...] = (y * (1.0 + scale_ref[0]) + shift_ref[0]).astype(o_ref.dtype)


def _norm_kernel(x_ref, g_ref, o_ref):
    x = x_ref[...]
    y = x * lax.rsqrt(jnp.mean(x * x, axis=-1, keepdims=True) + NORM_EPS) * g_ref[...]
    o_ref[...] = y.astype(o_ref.dtype)


class _Layout:
    def __init__(self, batch, seq, ctx_len):
        self.batch, self.seq, self.ctx_len = batch, seq, ctx_len
        self.tm = batch * ctx_len
        assert seq % self.tm == 0 and ctx_len % CHUNK == 0 and seq % CHUNK == 0
        self.tiles_per_batch = seq // self.tm
        self.n_tiles = 1 + batch * self.tiles_per_batch
        self.rows = self.n_tiles * self.tm
        self.ctx_chunks = ctx_len // CHUNK
        self.lat_chunks = seq // CHUNK

    def mod_row(self, tile):
        return jnp.where(tile == 0, self.batch, (tile - 1) // self.tiles_per_batch)


def _mod_index(lay, layer, k, tile):
    return layer * MOD_ROWS * N_MOD + lay.mod_row(tile) * N_MOD + k


def _norm_modulate(lay, x_all, g, mod3, layer, k_shift, k_scale, out_dtype):
    d = x_all.shape[1]
    return pl.pallas_call(
        _norm_mod_kernel,
        grid=(lay.n_tiles,),
        in_specs=[
            pl.BlockSpec((lay.tm, d), lambda i: (i, 0)),
            pl.BlockSpec((1, d), lambda i: (0, 0)),
            pl.BlockSpec((1, 1, d), lambda i: (_mod_index(lay, layer, k_shift, i), 0, 0)),
            pl.BlockSpec((1, 1, d), lambda i: (_mod_index(lay, layer, k_scale, i), 0, 0)),
        ],
        out_specs=pl.BlockSpec((lay.tm, d), lambda i: (i, 0)),
        out_shape=jax.ShapeDtypeStruct((lay.rows, d), out_dtype),
        compiler_params=_params(1),
        name="norm_modulate",
    )(x_all, g.reshape(1, d), mod3, mod3)


def _final_norm(lay, x_all, g):
    d = x_all.shape[1]
    n_lat_tiles = lay.n_tiles - 1
    return pl.pallas_call(
        _norm_kernel,
        grid=(n_lat_tiles,),
        in_specs=[
            pl.BlockSpec((lay.tm, d), lambda i: (i + 1, 0)),
            pl.BlockSpec((1, d), lambda i: (0, 0)),
        ],
        out_specs=pl.BlockSpec((lay.tm, d), lambda i: (i, 0)),
        out_shape=jax.ShapeDtypeStruct((n_lat_tiles * lay.tm, d), F32),
        compiler_params=_params(1),
        name="final_norm",
    )(x_all, g.reshape(1, d))


def _in_proj_kernel(h_ref, w_ref, cos_ref, sin_ref, o_ref, *, tn):
    j = pl.program_id(0)
    col = j * tn
    acc = jnp.dot(h_ref[...], w_ref[...], preferred_element_type=F32)

    def rope(scale):
        for s in range(tn // LANES):
            t = (s % 2) * LANES
            xs = acc[:, s * LANES:(s + 1) * LANES]
            r = xs * cos_ref[:, t:t + LANES] + pltpu.roll(xs, LANES // 2, 1) * sin_ref[:, t:t + LANES]
            if scale != 1.0:
                r = r * scale
            o_ref[:, s * LANES:(s + 1) * LANES] = r.astype(o_ref.dtype)

    @pl.when(col < K0)
    def _():
        rope(1.0)

    @pl.when((col >= K0) & (col < V0))
    def _():
        rope(RET_QK_DIM ** -0.5)

    @pl.when((col >= V0) & (col < G0))
    def _():
        o_ref[...] = acc.astype(o_ref.dtype)

    @pl.when((col >= G0) & (col < UA0))
    def _():
        o_ref[...] = _silu(acc).astype(o_ref.dtype)

    @pl.when((col >= UA0) & (col < GA0))
    def _():
        o_ref[...] = _gelu_tanh(acc).astype(o_ref.dtype)

    @pl.when(col >= GA0)
    def _():
        o_ref[...] = jax.nn.sigmoid(acc).astype(o_ref.dtype)


def _in_proj(lay, h, w_in, layer, cos_t, sin_t):
    d = h.shape[1]
    n = w_in.shape[2]
    tn = 2048
    n_rope_tiles = V0 // tn
    tab = lambda j, i: (jnp.where(j < n_rope_tiles, i, 0), 0)
    return pl.pallas_call(
        functools.partial(_in_proj_kernel, tn=tn),
        grid=(n // tn, lay.n_tiles),
        in_specs=[
            pl.BlockSpec((lay.tm, d), lambda j, i: (i, 0)),
            pl.BlockSpec((None, d, tn), lambda j, i: (layer, 0, j)),
            pl.BlockSpec((lay.tm, RET_QK_DIM), tab),
            pl.BlockSpec((lay.tm, RET_QK_DIM), tab),
        ],
        out_specs=pl.BlockSpec((lay.tm, tn), lambda j, i: (i, j)),
        out_shape=jax.ShapeDtypeStruct((lay.rows, n), BF16),
        compiler_params=_params(2),
        name="in_proj",
    )(h, w_in, cos_t, sin_t)


def _retention_kernel(lg_ref, q_ref, k_ref, v_ref, *rest, reverse, final):
    if final:
        other_ref, gate_ref, o_ref, s_ref = rest
    else:
        o_ref, s_ref = rest

    @pl.when(pl.program_id(1) == 0)
    def _():
        s_ref[...] = jnp.zeros_like(s_ref)

    c = CHUNK
    qi = lax.broadcasted_iota(jnp.int32, (c, c), 0)
    kj = lax.broadcasted_iota(jnp.int32, (c, c), 1)
    diff = ((kj - qi) if reverse else (qi - kj)).astype(F32)
    pos = lax.broadcasted_iota(jnp.int32, (c, 1), 0).astype(F32)
    q_steps = (c - pos) if reverse else (pos + 1.0)
    k_steps = pos if reverse else (c - 1.0 - pos)

    for h in range(RET_HEADS):
        log_g = lg_ref[h]
        intra = jnp.where(diff >= 0, jnp.exp(jnp.maximum(diff, 0.0) * log_g), 0.0)
        q_decay = jnp.exp(q_steps * log_g)
        k_decay = jnp.exp(k_steps * log_g)
        chunk_decay = jnp.exp(jnp.full((1, 1), float(c), F32) * log_g)

        qh = q_ref[:, h * RET_QK_DIM:(h + 1) * RET_QK_DIM]
        kh = k_ref[:, h * RET_QK_DIM:(h + 1) * RET_QK_DIM]
        vh = v_ref[:, h * RET_V_DIM:(h + 1) * RET_V_DIM]
        s = s_ref[h]

        scores = lax.dot_general(qh, kh, (((1,), (1,)), ((), ())), preferred_element_type=F32) * intra
        o = (jnp.dot(scores.astype(BF16), vh, preferred_element_type=F32)
             + jnp.dot(qh, s.astype(BF16), preferred_element_type=F32) * q_decay)
        k_dec_t = (kh.astype(F32) * k_decay).T.astype(BF16)
        s_ref[h] = s * chunk_decay + jnp.dot(k_dec_t, vh, preferred_element_type=F32)

        sl = slice(h * RET_V_DIM, (h + 1) * RET_V_DIM)
        if final:
            o = o + other_ref[:, sl]
            mu = jnp.mean(o, axis=-1, keepdims=True)
            oc = o - mu
            var = jnp.mean(oc * oc, axis=-1, keepdims=True)
            o = oc * lax.rsqrt(var + NORM_EPS) * gate_ref[:, sl].astype(F32)
        o_ref[:, sl] = o.astype(o_ref.dtype)


def _retention_pass(lay, p, log_g, other, *, reverse):
    final = other is not None
    n_chunks = lay.ctx_chunks + lay.lat_chunks

    def row_block(b, n):
        ctx_blk = b * lay.ctx_chunks + ((lay.ctx_chunks - 1 - n) if reverse else n)
        m = n - lay.ctx_chunks
        lat_blk = lay.batch * lay.ctx_chunks + b * lay.lat_chunks + ((lay.lat_chunks - 1 - m) if reverse else m)
        return jnp.where(n < lay.ctx_chunks, ctx_blk, lat_blk)

    in_specs = [
        pl.BlockSpec((CHUNK, RET_QK), lambda b, n, lg: (row_block(b, n), Q0 // RET_QK)),
        pl.BlockSpec((CHUNK, RET_QK), lambda b, n, lg: (row_block(b, n), K0 // RET_QK)),
        pl.BlockSpec((CHUNK, RET_V), lambda b, n, lg: (row_block(b, n), V0 // RET_V)),
    ]
    args = [p, p, p]
    if final:
        in_specs += [
            pl.BlockSpec((CHUNK, RET_V), lambda b, n, lg: (row_block(b, n), 0)),
            pl.BlockSpec((CHUNK, RET_V), lambda b, n, lg: (row_block(b, n), G0 // RET_V)),
        ]
        args += [other, p]
    return pl.pallas_call(
        functools.partial(_retention_kernel, reverse=reverse, final=final),
        grid_spec=pltpu.PrefetchScalarGridSpec(
            num_scalar_prefetch=1,
            grid=(lay.batch, n_chunks),
            in_specs=in_specs,
            out_specs=pl.BlockSpec((CHUNK, RET_V), lambda b, n, lg: (row_block(b, n), 0)),
            scratch_shapes=[pltpu.VMEM((RET_HEADS, RET_QK_DIM, RET_V_DIM), F32)],
        ),
        out_shape=jax.ShapeDtypeStruct((lay.rows, RET_V), BF16 if final else F32),
        compiler_params=_params(2),
        name="retention_fwd_merge" if final else "retention_bwd",
    )(log_g, *args)


def _sgu_kernel(u_ref, v_ref, lng_ref, lnb_ref, ws_ref, bst_ref, o_ref):
    v = v_ref[...].astype(F32)
    mu = jnp.mean(v, axis=-1, keepdims=True)
    vc = v - mu
    var = jnp.mean(vc * vc, axis=-1, keepdims=True)
    vn = (vc * lax.rsqrt(var + NORM_EPS) * lng_ref[...] + lnb_ref[...]).astype(BF16)
    gw = SGU_WIDTH // SGU_GROUPS
    for g in range(SGU_GROUPS):
        sl = slice(g * gw, (g + 1) * gw)
        s = jnp.dot(ws_ref[g], vn[:, sl], preferred_element_type=F32) + bst_ref[:, g:g + 1]
        o_ref[:, sl] = (u_ref[:, sl].astype(F32) * s).astype(o_ref.dtype)


def _sgu(lay, p, ln_g, ln_b, w_s, b_s_t):
    return pl.pallas_call(
        _sgu_kernel,
        grid=(lay.rows // CHUNK,),
        in_specs=[
            pl.BlockSpec((CHUNK, SGU_WIDTH), lambda i: (i, UA0 // SGU_WIDTH)),
            pl.BlockSpec((CHUNK, SGU_WIDTH), lambda i: (i, VA0 // SGU_WIDTH)),
            pl.BlockSpec((1, SGU_WIDTH), lambda i: (0, 0)),
            pl.BlockSpec((1, SGU_WIDTH), lambda i: (0, 0)),
            pl.BlockSpec((SGU_GROUPS, CHUNK, CHUNK), lambda i: (0, 0, 0)),
            pl.BlockSpec((CHUNK, SGU_GROUPS), lambda i: (0, 0)),
        ],
        out_specs=pl.BlockSpec((CHUNK, SGU_WIDTH), lambda i: (i, 0)),
        out_shape=jax.ShapeDtypeStruct((lay.rows, SGU_WIDTH), BF16),
        compiler_params=_params(1),
        name="sgu",
    )(p, p, ln_g.reshape(1, -1), ln_b.reshape(1, -1), w_s, b_s_t)


def _merge_kernel(sgu_ref, ret_ref, wa_ref, wr_ref, ga_ref, gr_ref, o_ref):
    a = jnp.dot(sgu_ref[...], wa_ref[...], preferred_element_type=F32)
    r = jnp.dot(ret_ref[...], wr_ref[...], preferred_element_type=F32)
    o_ref[...] = (ga_ref[...].astype(F32) * a + gr_ref[...].astype(F32) * r).astype(o_ref.dtype)


def _merge(lay, sgu, ret, p, w_proj_a, w_proj_r, layer):
    n = w_proj_a.shape[2]
    tn = 512
    return pl.pallas_call(
        _merge_kernel,
        grid=(n // tn, lay.n_tiles),
        in_specs=[
            pl.BlockSpec((lay.tm, SGU_WIDTH), lambda j, i: (i, 0)),
            pl.BlockSpec((lay.tm, RET_V), lambda j, i: (i, 0)),
            pl.BlockSpec((None, SGU_WIDTH, tn), lambda j, i: (layer, 0, j)),
            pl.BlockSpec((None, RET_V, tn), lambda j, i: (layer, 0, j)),
            pl.BlockSpec((lay.tm, tn), lambda j, i: (i, GA0 // tn + j)),
            pl.BlockSpec((lay.tm, tn), lambda j, i: (i, GR0 // tn + j)),
        ],
        out_specs=pl.BlockSpec((lay.tm, tn), lambda j, i: (i, j)),
        out_shape=jax.ShapeDtypeStruct((lay.rows, n), BF16),
        compiler_params=_params(2),
        name="merge",
    )(sgu, ret, w_proj_a, w_proj_r, p, p)


def _proj_resid_kernel(a_ref, w_ref, resid_ref, gate_ref, o_ref):
    acc = jnp.dot(a_ref[...], w_ref[...], preferred_element_type=F32)
    o_ref[...] = resid_ref[...] + gate_ref[0] * acc


def _proj_resid(lay, a, w, w_index, resid, mod3, layer, k_gate):
    kdim = a.shape[1]
    n = w.shape[-1]
    tn = 512
    w_spec = pl.BlockSpec((None,) * len(w_index) + (kdim, tn), lambda j, i: (*w_index, 0, j))
    return pl.pallas_call(
        _proj_resid_kernel,
        grid=(n // tn, lay.n_tiles),
        in_specs=[
            pl.BlockSpec((lay.tm, kdim), lambda j, i: (i, 0)),
            w_spec,
            pl.BlockSpec((lay.tm, tn), lambda j, i: (i, j)),
            pl.BlockSpec((1, 1, tn), lambda j, i: (_mod_index(lay, layer, k_gate, i), 0, j)),
        ],
        out_specs=pl.BlockSpec((lay.tm, tn), lambda j, i: (i, j)),
        out_shape=jax.ShapeDtypeStruct((lay.rows, n), F32),
        compiler_params=_params(2),
        name="proj_resid",
    )(a, w, resid, mod3)


def _swiglu_up_kernel(h_ref, w1_ref, w3_ref, o_ref):
    h = h_ref[...]
    a = jnp.dot(h, w1_ref[...], preferred_element_type=F32)
    b = jnp.dot(h, w3_ref[...], preferred_element_type=F32)
    o_ref[...] = (_silu(a) * b).astype(o_ref.dtype)


def _swiglu_up(lay, h, w1, w3, w_index):
    d = h.shape[1]
    f = w1.shape[-1]
    tn = 512
    w_spec = pl.BlockSpec((None,) * len(w_index) + (d, tn), lambda j, i: (*w_index, 0, j))
    return pl.pallas_call(
        _swiglu_up_kernel,
        grid=(f // tn, lay.n_tiles),
        in_specs=[pl.BlockSpec((lay.tm, d), lambda j, i: (i, 0)), w_spec, w_spec],
        out_specs=pl.BlockSpec((lay.tm, tn), lambda j, i: (i, j)),
        out_shape=jax.ShapeDtypeStruct((lay.rows, f), BF16),
        compiler_params=_params(2),
        name="swiglu_up",
    )(h, w1, w3)


def _pack_bf16_pairs(y):
    half = y.shape[1] // 2
    lo = lax.bitcast_convert_type(y[:, :half].astype(BF16).astype(F32), jnp.uint32) >> 16
    hi = lax.bitcast_convert_type(y[:, half:].astype(BF16).astype(F32), jnp.uint32)
    return hi | lo


def _unpack_bf16_pairs(w):
    lo = lax.bitcast_convert_type(w << 16, F32).astype(BF16)
    hi = lax.bitcast_convert_type(w & jnp.uint32(0xFFFF0000), F32).astype(BF16)
    return jnp.concatenate([lo, hi], axis=1)


def _norm_router_kernel(x_ref, g_ref, shift_ref, scale_ref, rw_ref, rb_ref,
                        hp_ref, sel_ref, prob_ref, cnt_ref, run_ref):
    @pl.when(pl.program_id(0) == 0)
    def _():
        run_ref[...] = jnp.zeros_like(run_ref)

    x = x_ref[...]
    y = x * lax.rsqrt(jnp.mean(x * x, axis=-1, keepdims=True) + NORM_EPS) * g_ref[...]
    h = y * (1.0 + scale_ref[0]) + shift_ref[0]
    hp_ref[...] = _pack_bf16_pairs(h)

    logits = jnp.dot(h.astype(BF16), rw_ref[...], preferred_element_type=F32) + rb_ref[...]
    lane = lax.broadcasted_iota(jnp.int32, logits.shape, 1)
    neg = jnp.float32(-jnp.inf)
    lg = jnp.where(lane < N_EXPERTS, logits, neg)
    m1 = jnp.max(lg, axis=-1, keepdims=True)
    i1 = jnp.min(jnp.where(lg == m1, lane, LANES), axis=-1, keepdims=True)
    lg2 = jnp.where(lane == i1, neg, lg)
    m2 = jnp.max(lg2, axis=-1, keepdims=True)
    i2 = jnp.min(jnp.where(lg2 == m2, lane, LANES), axis=-1, keepdims=True)
    e2 = jnp.exp(m2 - m1)
    den = 1.0 + e2

    tm = x.shape[0]
    onehot = jnp.where(lane == i1, 1.0, 0.0) + jnp.where(lane == i2, 1.0, 0.0)
    earlier = jnp.where(lax.broadcasted_iota(jnp.int32, (tm, tm), 0) > lax.broadcasted_iota(jnp.int32, (tm, tm), 1),
                        1.0, 0.0).astype(BF16)
    before = jnp.dot(earlier, onehot.astype(BF16), preferred_element_type=F32) + run_ref[0:1, :]
    pos1 = jnp.sum(jnp.where(lane == i1, before, 0.0), axis=-1, keepdims=True).astype(jnp.int32)
    pos2 = jnp.sum(jnp.where(lane == i2, before, 0.0), axis=-1, keepdims=True).astype(jnp.int32)
    run_ref[...] = run_ref[...] + jnp.sum(onehot, axis=0, keepdims=True)
    cnt_ref[...] = run_ref[...]

    sel_ref[...] = jnp.where(lane == 0, i1, jnp.where(lane == 1, i2, jnp.where(lane == 2, pos1,
                                                                                jnp.where(lane == 3, pos2, 0))))
    prob_ref[...] = jnp.where(lane == 0, 1.0 / den, jnp.where(lane == 1, e2 / den, 0.0))


def _norm_router(lay, x_all, g, mod3, layer, k_shift, k_scale, router_w, router_b):
    d = x_all.shape[1]
    w = jnp.zeros((d, LANES), BF16).at[:, :N_EXPERTS].set(router_w.astype(BF16))
    b = jnp.zeros((1, LANES), F32).at[0, :N_EXPERTS].set(router_b.astype(F32))
    lane_block = pl.BlockSpec((lay.tm, LANES), lambda i: (i, 0))
    return pl.pallas_call(
        _norm_router_kernel,
        grid=(lay.n_tiles,),
        in_specs=[
            pl.BlockSpec((lay.tm, d), lambda i: (i, 0)),
            pl.BlockSpec((1, d), lambda i: (0, 0)),
            pl.BlockSpec((1, 1, d), lambda i: (_mod_index(lay, layer, k_shift, i), 0, 0)),
            pl.BlockSpec((1, 1, d), lambda i: (_mod_index(lay, layer, k_scale, i), 0, 0)),
            pl.BlockSpec((d, LANES), lambda i: (0, 0)),
            pl.BlockSpec((1, LANES), lambda i: (0, 0)),
        ],
        out_specs=[
            pl.BlockSpec((lay.tm, d // 2), lambda i: (i, 0)),
            lane_block,
            lane_block,
            pl.BlockSpec((8, LANES), lambda i: (0, 0)),
        ],
        out_shape=[
            jax.ShapeDtypeStruct((lay.rows, d // 2), jnp.uint32),
            jax.ShapeDtypeStruct((lay.rows, LANES), jnp.int32),
            jax.ShapeDtypeStruct((lay.rows, LANES), F32),
            jax.ShapeDtypeStruct((8, LANES), F32),
        ],
        scratch_shapes=[pltpu.VMEM((8, LANES), F32)],
        compiler_params=_params(1),
        name="norm_router",
    )(x_all, g.reshape(1, d), mod3, mod3, w, b)


def _routing_plan(sel, cnt, n_sorted_rows):
    counts = cnt[0, :N_EXPERTS].astype(jnp.int32)
    padded = (counts + MOE_TILE - 1) // MOE_TILE * MOE_TILE
    ends = jnp.cumsum(padded)
    starts = ends - padded
    dest1 = starts[sel[:, 0]] + sel[:, 2]
    dest2 = starts[sel[:, 1]] + sel[:, 3]
    tile_start = jnp.arange(n_sorted_rows // MOE_TILE, dtype=jnp.int32) * MOE_TILE
    tile_expert = jnp.minimum(jnp.sum(tile_start[:, None] >= ends[None, :], axis=1), N_EXPERTS - 1)
    n_used_tiles = (ends[-1:] // MOE_TILE).astype(jnp.int32)
    fill_lo = jnp.concatenate([starts + counts, ends[-1:]]).astype(jnp.int32)
    fill_hi = jnp.concatenate([ends, jnp.full((1,), n_sorted_rows)]).astype(jnp.int32)
    return dest1, dest2, tile_expert.astype(jnp.int32), n_used_tiles, fill_lo, fill_hi


def _lagged_dma_loop(lo, hi, start, wait):
    def body(i, carry):
        start(i)

        @pl.when(i - lo >= DMA_LAG)
        def _():
            wait()

        return carry

    def drain(i, carry):
        wait()
        return carry

    lax.fori_loop(lo, hi, body, 0)
    lax.fori_loop(0, jnp.minimum(hi - lo, DMA_LAG), drain, 0)


def _dispatch_kernel(dest1_ref, dest2_ref, fill_lo_ref, fill_hi_ref, h_ref, zero_ref, xs_ref, sem, zero_sem):
    n_tok = h_ref.shape[0]

    def token_copy(t, d):
        return pltpu.make_async_copy(h_ref.at[pl.ds(t, 1)], xs_ref.at[pl.ds(d, 1)], sem)

    def start_token(t):
        token_copy(t, dest1_ref[t]).start()
        token_copy(t, dest2_ref[t]).start()

    def wait_token():
        token_copy(0, 0).wait()
        token_copy(0, 0).wait()

    _lagged_dma_loop(0, n_tok, start_token, wait_token)

    def zero_copy(d):
        return pltpu.make_async_copy(zero_ref, xs_ref.at[pl.ds(d, 1)], zero_sem)

    for r in range(N_EXPERTS + 1):
        _lagged_dma_loop(fill_lo_ref[r], fill_hi_ref[r], lambda d: zero_copy(d).start(), lambda: zero_copy(0).wait())


def _dispatch(hp, dest1, dest2, fill_lo, fill_hi, n_sorted_rows):
    width = hp.shape[1]
    return pl.pallas_call(
        _dispatch_kernel,
        grid_spec=pltpu.PrefetchScalarGridSpec(
            num_scalar_prefetch=4,
            grid=(1,),
            in_specs=[pl.BlockSpec(memory_space=pl.ANY), pl.BlockSpec(memory_space=pl.ANY)],
            out_specs=pl.BlockSpec(memory_space=pl.ANY),
            scratch_shapes=[pltpu.SemaphoreType.DMA(()), pltpu.SemaphoreType.DMA(())],
        ),
        out_shape=jax.ShapeDtypeStruct((n_sorted_rows, width), hp.dtype),
        compiler_params=_params(1),
        name="moe_dispatch",
    )(dest1, dest2, fill_lo, fill_hi, hp, jnp.zeros((1, width), hp.dtype))


def _grouped_up_kernel(tile_expert_ref, n_used_ref, a_ref, w1_ref, w3_ref, o_ref):
    i = pl.program_id(1)

    @pl.when(i < n_used_ref[0])
    def _():
        h = _unpack_bf16_pairs(a_ref[...])
        a = jnp.dot(h, w1_ref[...], preferred_element_type=F32)
        b = jnp.dot(h, w3_ref[...], preferred_element_type=F32)
        o_ref[...] = (_silu(a) * b).astype(o_ref.dtype)

    @pl.when(i >= n_used_ref[0])
    def _():
        o_ref[...] = jnp.zeros_like(o_ref)


def _grouped_up(xs, w1, w3, moe_layer, tile_expert, n_used):
    n_rows, half = xs.shape
    d, f = w1.shape[-2:]
    tn = 1024
    w_spec = pl.BlockSpec((None, None, d, tn), lambda j, i, te, nu: (moe_layer, te[i], 0, j))
    return pl.pallas_call(
        _grouped_up_kernel,
        grid_spec=pltpu.PrefetchScalarGridSpec(
            num_scalar_prefetch=2,
            grid=(f // tn, n_rows // MOE_TILE),
            in_specs=[pl.BlockSpec((MOE_TILE, half), lambda j, i, te, nu: (i, 0)), w_spec, w_spec],
            out_specs=pl.BlockSpec((MOE_TILE, tn), lambda j, i, te, nu: (i, j)),
        ),
        out_shape=jax.ShapeDtypeStruct((n_rows, f), BF16),
        compiler_params=_params(2),
        name="moe_up",
    )(tile_expert, n_used, xs, w1, w3)


def _grouped_down_kernel(tile_expert_ref, n_used_ref, a_ref, w_ref, o_ref):
    i = pl.program_id(1)

    @pl.when(i < n_used_ref[0])
    def _():
        o_ref[...] = jnp.dot(a_ref[...], w_ref[...], preferred_element_type=F32)

    @pl.when(i >= n_used_ref[0])
    def _():
        o_ref[...] = jnp.zeros_like(o_ref)


def _grouped_down(act, w2, moe_layer, tile_expert, n_used):
    n_rows, f = act.shape
    n = w2.shape[-1]
    tn = 512
    return pl.pallas_call(
        _grouped_down_kernel,
        grid_spec=pltpu.PrefetchScalarGridSpec(
            num_scalar_prefetch=2,
            grid=(n // tn, n_rows // MOE_TILE),
            in_specs=[
                pl.BlockSpec((MOE_TILE, f), lambda j, i, te, nu: (i, 0)),
                pl.BlockSpec((None, None, f, tn), lambda j, i, te, nu: (moe_layer, te[i], 0, j)),
            ],
            out_specs=pl.BlockSpec((MOE_TILE, tn), lambda j, i, te, nu: (i, j)),
        ),
        out_shape=jax.ShapeDtypeStruct((n_rows, n), F32),
        compiler_params=_params(2),
        name="moe_down",
    )(tile_expert, n_used, act, w2)


def _combine_kernel(dest1_ref, dest2_ref, x_ref, prob_ref, gate_ref, ys_ref, o_ref, buf1, buf2, sem1, sem2):
    tm = x_ref.shape[0]
    base = pl.program_id(0) * tm

    def copy1(t, d):
        return pltpu.make_async_copy(ys_ref.at[pl.ds(d, 1)], buf1.at[pl.ds(t, 1)], sem1)

    def copy2(t, d):
        return pltpu.make_async_copy(ys_ref.at[pl.ds(d, 1)], buf2.at[pl.ds(t, 1)], sem2)

    def start(t, carry):
        copy1(t, dest1_ref[base + t]).start()
        copy2(t, dest2_ref[base + t]).start()
        return carry

    def wait(t, carry):
        copy1(0, 0).wait()
        copy2(0, 0).wait()
        return carry

    lax.fori_loop(0, tm, start, 0)
    lax.fori_loop(0, tm, wait, 0)
    p = prob_ref[...]
    f = p[:, 0:1] * buf1[...] + p[:, 1:2] * buf2[...]
    o_ref[...] = x_ref[...] + gate_ref[0] * f


def _combine(lay, x_all, prob, ys, dest1, dest2, mod3, layer, k_gate):
    d = x_all.shape[1]
    return pl.pallas_call(
        _combine_kernel,
        grid_spec=pltpu.PrefetchScalarGridSpec(
            num_scalar_prefetch=2,
            grid=(lay.n_tiles,),
            in_specs=[
                pl.BlockSpec((lay.tm, d), lambda i, d1, d2: (i, 0)),
                pl.BlockSpec((lay.tm, LANES), lambda i, d1, d2: (i, 0)),
                pl.BlockSpec((1, 1, d), lambda i, d1, d2: (_mod_index(lay, layer, k_gate, i), 0, 0)),
                pl.BlockSpec(memory_space=pl.ANY),
            ],
            out_specs=pl.BlockSpec((lay.tm, d), lambda i, d1, d2: (i, 0)),
            scratch_shapes=[
                pltpu.VMEM((lay.tm, d), F32),
                pltpu.VMEM((lay.tm, d), F32),
                pltpu.SemaphoreType.DMA(()),
                pltpu.SemaphoreType.DMA(()),
            ],
        ),
        out_shape=jax.ShapeDtypeStruct((lay.rows, d), F32),
        compiler_params=_params(1),
        name="moe_combine",
    )(dest1, dest2, x_all, prob, mod3, ys)


def _moe_ffn(lay, x_all, norm_g, mod3, layer, router_w, router_b, w1, w3, w2, moe_layer):
    hp, sel, prob, cnt = _norm_router(lay, x_all, norm_g, mod3, layer, 3, 4, router_w, router_b)
    n_sorted_rows = 2 * lay.rows + N_EXPERTS * MOE_TILE
    dest1, dest2, tile_expert, n_used, fill_lo, fill_hi = _routing_plan(sel, cnt, n_sorted_rows)
    xs = _dispatch(hp, dest1, dest2, fill_lo, fill_hi, n_sorted_rows)
    act = _grouped_up(xs, w1, w3, moe_layer, tile_expert, n_used)
    ys = _grouped_down(act, w2, moe_layer, tile_expert, n_used)
    return _combine(lay, x_all, prob, ys, dest1, dest2, mod3, layer, 5)


def _rope_tables(lay):
    t = jnp.arange(lay.seq)
    quarter = RET_QK_DIM // 4
    freqs = ROPE_BASE ** (-jnp.arange(quarter, dtype=F32) / quarter)

    def tables(pos):
        ang = pos.astype(F32)[:, None] * freqs[None, :]
        cos, sin = jnp.cos(ang), jnp.sin(ang)
        return jnp.concatenate([cos, cos], axis=-1), jnp.concatenate([-sin, sin], axis=-1)

    cos_r, sin_r = tables(t // GRID_W)
    cos_c, sin_c = tables(t % GRID_W)
    cos_l = jnp.concatenate([cos_r, cos_c], axis=-1)
    sin_l = jnp.concatenate([sin_r, sin_c], axis=-1)
    n_ctx = lay.batch * lay.ctx_len
    cos_t = jnp.concatenate([jnp.ones((n_ctx, RET_QK_DIM), F32)] + [cos_l] * lay.batch, axis=0)
    sin_t = jnp.concatenate([jnp.zeros((n_ctx, RET_QK_DIM), F32)] + [sin_l] * lay.batch, axis=0)
    return cos_t, sin_t


def kernel(x, c, ctx, c_ctx, w_mod, b_mod, norm1_g, norm2_g, w_in, sgu_ln_g, sgu_ln_b, sgu_w, sgu_b,
           ret_log_decay, w_proj_a, w_proj_r, w_out, ffn_w1, ffn_w3, ffn_w2, router_w, router_b,
           moe_w1, moe_w3, moe_w2, final_norm_g):
    batch, seq, d = x.shape
    ctx_len = ctx.shape[1]
    depth = w_mod.shape[0]
    lay = _Layout(batch, seq, ctx_len)

    cvec = jnp.zeros((MOD_ROWS, d), F32).at[:batch].set(c).at[batch].set(c_ctx)
    mod = _modulation(cvec, w_mod, b_mod)
    mod3 = mod.reshape(depth * MOD_ROWS * N_MOD, 1, d)

    cos_t, sin_t = _rope_tables(lay)
    x_all = jnp.concatenate([ctx.reshape(batch * ctx_len, d), x.reshape(batch * seq, d)], axis=0)

    w_in_b = w_in.astype(BF16)
    w_proj_a_b = w_proj_a.astype(BF16)
    w_proj_r_b = w_proj_r.astype(BF16)
    w_out_b = w_out.astype(BF16)
    sgu_w_b = sgu_w.astype(BF16)
    ffn_w1_b, ffn_w3_b, ffn_w2_b = ffn_w1.astype(BF16), ffn_w3.astype(BF16), ffn_w2.astype(BF16)
    moe_w1_b, moe_w3_b, moe_w2_b = moe_w1.astype(BF16), moe_w3.astype(BF16), moe_w2.astype(BF16)
    log_decay = ret_log_decay.astype(F32)

    for layer in range(depth):
        h = _norm_modulate(lay, x_all, norm1_g[layer], mod3, layer, 0, 1, BF16)
        p = _in_proj(lay, h, w_in_b, layer, cos_t, sin_t)
        o_bwd = _retention_pass(lay, p, log_decay[layer, 1], None, reverse=True)
        ret = _retention_pass(lay, p, log_decay[layer, 0], o_bwd, reverse=False)
        sgu = _sgu(lay, p, sgu_ln_g[layer], sgu_ln_b[layer], sgu_w_b[layer], sgu_b[layer].T)
        y = _merge(lay, sgu, ret, p, w_proj_a_b, w_proj_r_b, layer)
        x_all = _proj_resid(lay, y, w_out_b, (layer,), x_all, mod3, layer, 2)
        i = layer // 2
        if layer % 2 == 0:
            h2 = _norm_modulate(lay, x_all, norm2_g[layer], mod3, layer, 3, 4, BF16)
            act = _swiglu_up(lay, h2, ffn_w1_b, ffn_w3_b, (i,))
            x_all = _proj_resid(lay, act, ffn_w2_b, (i,), x_all, mod3, layer, 5)
        else:
            x_all = _moe_ffn(lay, x_all, norm2_g[layer], mod3, layer, router_w[i], router_b[i],
                             moe_w1_b, moe_w3_b, moe_w2_b, i)

    out = _final_norm(lay, x_all, final_norm_g)
    return out.reshape(batch, seq, d)
```

```python
import functools

import jax
import jax.numpy as jnp
from jax import lax
from jax.experimental import pallas as pl
from jax.experimental.pallas import tpu as pltpu

F32 = jnp.float32
BF16 = jnp.bfloat16

GRID_W = 64
CHUNK = 128
SGU_GROUPS = 8
SGU_WIDTH = 2048
RET_HEADS = 8
RET_QK_DIM = 256
RET_V_DIM = 512
RET_QK = RET_HEADS * RET_QK_DIM
RET_V = RET_HEADS * RET_V_DIM
ROPE_BASE = 10000.0
N_EXPERTS = 8
N_MOD = 6
NORM_EPS = 1e-6
MOD_ROWS = 8
LANES = 128
MOE_TILE = 512

VMEM_LIMIT_BYTES = 52 * 1024 * 1024

Q0 = 0
K0 = Q0 + RET_QK
V0 = K0 + RET_QK
G0 = V0 + RET_V
UA0 = G0 + RET_V
VA0 = UA0 + SGU_WIDTH
GA0 = VA0 + SGU_WIDTH
GR0 = GA0 + 2048
IN_COLS = GR0 + 2048


def _params(n_grid_dims):
    return pltpu.CompilerParams(
        dimension_semantics=("arbitrary",) * n_grid_dims,
        vmem_limit_bytes=VMEM_LIMIT_BYTES,
    )


def _gelu_tanh(x):
    return 0.5 * x * (1.0 + jnp.tanh(0.7978845608028654 * (x + 0.044715 * (x * x * x))))


def _silu(x):
    return x * jax.nn.sigmoid(x)


def _mod_kernel(c_ref, w_ref, b_ref, o_ref):
    sc = _silu(c_ref[...]).astype(BF16)
    o_ref[...] = jnp.dot(sc, w_ref[...].astype(BF16), preferred_element_type=F32) + b_ref[...]


def _modulation(cvec, w_mod, b_mod):
    depth, d, n = w_mod.shape
    tn = 1024
    return pl.pallas_call(
        _mod_kernel,
        grid=(depth, n // tn),
        in_specs=[
            pl.BlockSpec((MOD_ROWS, d), lambda l, j: (0, 0)),
            pl.BlockSpec((None, d, tn), lambda l, j: (l, 0, j)),
            pl.BlockSpec((None, 1, tn), lambda l, j: (l, 0, j)),
        ],
        out_specs=pl.BlockSpec((None, MOD_ROWS, tn), lambda l, j: (l, 0, j)),
        out_shape=jax.ShapeDtypeStruct((depth, MOD_ROWS, n), F32),
        compiler_params=_params(2),
        name="modulation",
    )(cvec, w_mod, b_mod.reshape(depth, 1, n))


def _norm_mod_kernel(x_ref, g_ref, shift_ref, scale_ref, o_ref):
    x = x_ref[...]
    y = x * lax.rsqrt(jnp.mean(x * x, axis=-1, keepdims=True) + NORM_EPS) * g_ref[...]
    o_ref[...] = (y * (1.0 + scale_ref[0]) + shift_ref[0]).astype(o_ref.dtype)


def _norm_kernel(x_ref, g_ref, o_ref):
    x = x_ref[...]
    y = x * lax.rsqrt(jnp.mean(x * x, axis=-1, keepdims=True) + NORM_EPS) * g_ref[...]
    o_ref[...] = y.astype(o_ref.dtype)


class _Layout:
    def __init__(self, batch, seq, ctx_len):
        self.batch, self.seq, self.ctx_len = batch, seq, ctx_len
        self.tm = batch * ctx_len
        assert seq % self.tm == 0 and ctx_len % CHUNK == 0 and seq % CHUNK == 0
        self.tiles_per_batch = seq // self.tm
        self.n_tiles = 1 + batch * self.tiles_per_batch
        self.rows = self.n_tiles * self.tm
        self.ctx_chunks = ctx_len // CHUNK
        self.lat_chunks = seq // CHUNK

    def mod_row(self, tile):
        return jnp.where(tile == 0, self.batch, (tile - 1) // self.tiles_per_batch)


def _mod_index(lay, layer, k, tile):
    return layer * MOD_ROWS * N_MOD + lay.mod_row(tile) * N_MOD + k


def _norm_modulate(lay, x_all, g, mod3, layer, k_shift, k_scale, out_dtype):
    d = x_all.shape[1]
    return pl.pallas_call(
        _norm_mod_kernel,
        grid=(lay.n_tiles,),
        in_specs=[
            pl.BlockSpec((lay.tm, d), lambda i: (i, 0)),
            pl.BlockSpec((1, d), lambda i: (0, 0)),
            pl.BlockSpec((1, 1, d), lambda i: (_mod_index(lay, layer, k_shift, i), 0, 0)),
            pl.BlockSpec((1, 1, d), lambda i: (_mod_index(lay, layer, k_scale, i), 0, 0)),
        ],
        out_specs=pl.BlockSpec((lay.tm, d), lambda i: (i, 0)),
        out_shape=jax.ShapeDtypeStruct((lay.rows, d), out_dtype),
        compiler_params=_params(1),
        name="norm_modulate",
    )(x_all, g.reshape(1, d), mod3, mod3)


def _final_norm(lay, x_all, g):
    d = x_all.shape[1]
    n_lat_tiles = lay.n_tiles - 1
    return pl.pallas_call(
        _norm_kernel,
        grid=(n_lat_tiles,),
        in_specs=[
            pl.BlockSpec((lay.tm, d), lambda i: (i + 1, 0)),
            pl.BlockSpec((1, d), lambda i: (0, 0)),
        ],
        out_specs=pl.BlockSpec((lay.tm, d), lambda i: (i, 0)),
        out_shape=jax.ShapeDtypeStruct((n_lat_tiles * lay.tm, d), F32),
        compiler_params=_params(1),
        name="final_norm",
    )(x_all, g.reshape(1, d))


IN_PROJ_TN = 2048
IN_PROJ_SUB = 512


def _in_proj_kernel(h_ref, w_ref, *rest, kind):
    o_ref = rest[-1]
    h = h_ref[...]
    if kind == "rope":
        cos_ref, sin_ref = rest[:2]
        scale = jnp.where(pl.program_id(0) == 0, 1.0, RET_QK_DIM ** -0.5).astype(F32)
    for c0 in range(0, IN_PROJ_TN, IN_PROJ_SUB):
        acc = jnp.dot(h, w_ref[:, c0:c0 + IN_PROJ_SUB], preferred_element_type=F32)
        if kind == "rope":
            for s in range(0, IN_PROJ_SUB, LANES):
                t = (c0 + s) % RET_QK_DIM
                xs = acc[:, s:s + LANES]
                r = xs * cos_ref[:, t:t + LANES] + pltpu.roll(xs, LANES // 2, 1) * sin_ref[:, t:t + LANES]
                o_ref[:, c0 + s:c0 + s + LANES] = (r * scale).astype(o_ref.dtype)
        else:
            if kind == "silu":
                acc = _silu(acc)
            elif kind == "gelu":
                acc = _gelu_tanh(acc)
            elif kind == "sigmoid":
                acc = jax.nn.sigmoid(acc)
            else:
                assert kind == "plain"
            o_ref[:, c0:c0 + IN_PROJ_SUB] = acc.astype(o_ref.dtype)


def _in_proj(lay, h, w_in, layer, col0, n_cols, kind, tables=()):
    d = h.shape[1]
    tn = IN_PROJ_TN
    table_spec = pl.BlockSpec((lay.tm, RET_QK_DIM), lambda j, i: (i, 0))
    return pl.pallas_call(
        functools.partial(_in_proj_kernel, kind=kind),
        grid=(n_cols // tn, lay.n_tiles),
        in_specs=[
            pl.BlockSpec((lay.tm, d), lambda j, i: (i, 0)),
            pl.BlockSpec((None, d, tn), lambda j, i: (layer, 0, col0 // tn + j)),
        ] + [table_spec] * len(tables),
        out_specs=pl.BlockSpec((lay.tm, tn), lambda j, i: (i, j)),
        out_shape=jax.ShapeDtypeStruct((lay.rows, n_cols), BF16),
        compiler_params=_params(2),
        name="in_proj_" + kind,
    )(h, w_in, *tables)


def _retention_kernel(lg_ref, q_ref, k_ref, v_ref, *rest, reverse, final):
    if final:
        other_ref, gate_ref, o_ref, s_ref = rest
    else:
        o_ref, s_ref = rest

    @pl.when(pl.program_id(1) == 0)
    def _():
        s_ref[...] = jnp.zeros_like(s_ref)

    c = CHUNK
    qi = lax.broadcasted_iota(jnp.int32, (c, c), 0)
    kj = lax.broadcasted_iota(jnp.int32, (c, c), 1)
    diff = ((kj - qi) if reverse else (qi - kj)).astype(F32)
    pos = lax.broadcasted_iota(jnp.int32, (c, 1), 0).astype(F32)
    q_steps = (c - pos) if reverse else (pos + 1.0)
    k_steps = pos if reverse else (c - 1.0 - pos)

    for h in range(RET_HEADS):
        log_g = lg_ref[h]
        intra = jnp.where(diff >= 0, jnp.exp(jnp.maximum(diff, 0.0) * log_g), 0.0)
        q_decay = jnp.exp(q_steps * log_g)
        k_decay = jnp.exp(k_steps * log_g)
        chunk_decay = jnp.exp(jnp.full((1, 1), float(c), F32) * log_g)

        qh = q_ref[:, h * RET_QK_DIM:(h + 1) * RET_QK_DIM]
        kh = k_ref[:, h * RET_QK_DIM:(h + 1) * RET_QK_DIM]
        vh = v_ref[:, h * RET_V_DIM:(h + 1) * RET_V_DIM]
        s = s_ref[h]

        scores = lax.dot_general(qh, kh, (((1,), (1,)), ((), ())), preferred_element_type=F32) * intra
        o = (jnp.dot(scores.astype(BF16), vh, preferred_element_type=F32)
             + jnp.dot(qh, s.astype(BF16), preferred_element_type=F32) * q_decay)
        k_dec_t = (kh.astype(F32) * k_decay).T.astype(BF16)
        s_ref[h] = s * chunk_decay + jnp.dot(k_dec_t, vh, preferred_element_type=F32)

        sl = slice(h * RET_V_DIM, (h + 1) * RET_V_DIM)
        if final:
            o = o + other_ref[:, sl]
            mu = jnp.mean(o, axis=-1, keepdims=True)
            oc = o - mu
            var = jnp.mean(oc * oc, axis=-1, keepdims=True)
            o = oc * lax.rsqrt(var + NORM_EPS) * gate_ref[:, sl].astype(F32)
        o_ref[:, sl] = o.astype(o_ref.dtype)


def _retention_pass(lay, qk, v, log_g, other=None, gate=None, *, reverse):
    final = other is not None
    n_chunks = lay.ctx_chunks + lay.lat_chunks

    def row_block(b, n):
        ctx_blk = b * lay.ctx_chunks + ((lay.ctx_chunks - 1 - n) if reverse else n)
        m = n - lay.ctx_chunks
        lat_blk = lay.batch * lay.ctx_chunks + b * lay.lat_chunks + ((lay.lat_chunks - 1 - m) if reverse else m)
        return jnp.where(n < lay.ctx_chunks, ctx_blk, lat_blk)

    in_specs = [
        pl.BlockSpec((CHUNK, RET_QK), lambda b, n, lg: (row_block(b, n), 0)),
        pl.BlockSpec((CHUNK, RET_QK), lambda b, n, lg: (row_block(b, n), 1)),
        pl.BlockSpec((CHUNK, RET_V), lambda b, n, lg: (row_block(b, n), 0)),
    ]
    args = [qk, qk, v]
    if final:
        in_specs += [pl.BlockSpec((CHUNK, RET_V), lambda b, n, lg: (row_block(b, n), 0))] * 2
        args += [other, gate]
    return pl.pallas_call(
        functools.partial(_retention_kernel, reverse=reverse, final=final),
        grid_spec=pltpu.PrefetchScalarGridSpec(
            num_scalar_prefetch=1,
            grid=(lay.batch, n_chunks),
            in_specs=in_specs,
            out_specs=pl.BlockSpec((CHUNK, RET_V), lambda b, n, lg: (row_block(b, n), 0)),
            scratch_shapes=[pltpu.VMEM((RET_HEADS, RET_QK_DIM, RET_V_DIM), F32)],
        ),
        out_shape=jax.ShapeDtypeStruct((lay.rows, RET_V), BF16 if final else F32),
        compiler_params=_params(2),
        name="retention_fwd_merge" if final else "retention_bwd",
    )(log_g, *args)


def _sgu_kernel(u_ref, v_ref, lng_ref, lnb_ref, ws_ref, bst_ref, o_ref):
    v = v_ref[...].astype(F32)
    mu = jnp.mean(v, axis=-1, keepdims=True)
    vc = v - mu
    var = jnp.mean(vc * vc, axis=-1, keepdims=True)
    vn = (vc * lax.rsqrt(var + NORM_EPS) * lng_ref[...] + lnb_ref[...]).astype(BF16)
    gw = SGU_WIDTH // SGU_GROUPS
    for g in range(SGU_GROUPS):
        sl = slice(g * gw, (g + 1) * gw)
        s = jnp.dot(ws_ref[g], vn[:, sl], preferred_element_type=F32) + bst_ref[:, g:g + 1]
        o_ref[:, sl] = (u_ref[:, sl].astype(F32) * s).astype(o_ref.dtype)


def _sgu(lay, uv, ln_g, ln_b, w_s, b_s_t):
    return pl.pallas_call(
        _sgu_kernel,
        grid=(lay.rows // CHUNK,),
        in_specs=[
            pl.BlockSpec((CHUNK, SGU_WIDTH), lambda i: (i, 0)),
            pl.BlockSpec((CHUNK, SGU_WIDTH), lambda i: (i, 1)),
            pl.BlockSpec((1, SGU_WIDTH), lambda i: (0, 0)),
            pl.BlockSpec((1, SGU_WIDTH), lambda i: (0, 0)),
            pl.BlockSpec((SGU_GROUPS, CHUNK, CHUNK), lambda i: (0, 0, 0)),
            pl.BlockSpec((CHUNK, SGU_GROUPS), lambda i: (0, 0)),
        ],
        out_specs=pl.BlockSpec((CHUNK, SGU_WIDTH), lambda i: (i, 0)),
        out_shape=jax.ShapeDtypeStruct((lay.rows, SGU_WIDTH), BF16),
        compiler_params=_params(1),
        name="sgu",
    )(uv, uv, ln_g.reshape(1, -1), ln_b.reshape(1, -1), w_s, b_s_t)


def _merge_kernel(sgu_ref, ret_ref, wa_ref, wr_ref, ga_ref, gr_ref, o_ref):
    a = jnp.dot(sgu_ref[...], wa_ref[...], preferred_element_type=F32)
    r = jnp.dot(ret_ref[...], wr_ref[...], preferred_element_type=F32)
    o_ref[...] = (ga_ref[...].astype(F32) * a + gr_ref[...].astype(F32) * r).astype(o_ref.dtype)


def _merge(lay, sgu, ret, gates, w_proj_a, w_proj_r, layer):
    n = w_proj_a.shape[2]
    tn = 512
    return pl.pallas_call(
        _merge_kernel,
        grid=(n // tn, lay.n_tiles),
        in_specs=[
            pl.BlockSpec((lay.tm, SGU_WIDTH), lambda j, i: (i, 0)),
            pl.BlockSpec((lay.tm, RET_V), lambda j, i: (i, 0)),
            pl.BlockSpec((None, SGU_WIDTH, tn), lambda j, i: (layer, 0, j)),
            pl.BlockSpec((None, RET_V, tn), lambda j, i: (layer, 0, j)),
            pl.BlockSpec((lay.tm, tn), lambda j, i: (i, j)),
            pl.BlockSpec((lay.tm, tn), lambda j, i: (i, n // tn + j)),
        ],
        out_specs=pl.BlockSpec((lay.tm, tn), lambda j, i: (i, j)),
        out_shape=jax.ShapeDtypeStruct((lay.rows, n), BF16),
        compiler_params=_params(2),
        name="merge",
    )(sgu, ret, w_proj_a, w_proj_r, gates, gates)


def _proj_resid_kernel(a_ref, w_ref, resid_ref, gate_ref, o_ref):
    acc = jnp.dot(a_ref[...], w_ref[...], preferred_element_type=F32)
    o_ref[...] = resid_ref[...] + gate_ref[0] * acc


def _proj_resid(lay, a, w, w_index, resid, mod3, layer, k_gate):
    kdim = a.shape[1]
    n = w.shape[-1]
    tn = 512
    w_spec = pl.BlockSpec((None,) * len(w_index) + (kdim, tn), lambda j, i: (*w_index, 0, j))
    return pl.pallas_call(
        _proj_resid_kernel,
        grid=(n // tn, lay.n_tiles),
        in_specs=[
            pl.BlockSpec((lay.tm, kdim), lambda j, i: (i, 0)),
            w_spec,
            pl.BlockSpec((lay.tm, tn), lambda j, i: (i, j)),
            pl.BlockSpec((1, 1, tn), lambda j, i: (_mod_index(lay, layer, k_gate, i), 0, j)),
        ],
        out_specs=pl.BlockSpec((lay.tm, tn), lambda j, i: (i, j)),
        out_shape=jax.ShapeDtypeStruct((lay.rows, n), F32),
        compiler_params=_params(2),
        name="proj_resid",
    )(a, w, resid, mod3)


def _swiglu_up_kernel(h_ref, w1_ref, w3_ref, o_ref):
    h = h_ref[...]
    a = jnp.dot(h, w1_ref[...].astype(BF16), preferred_element_type=F32)
    b = jnp.dot(h, w3_ref[...].astype(BF16), preferred_element_type=F32)
    o_ref[...] = (_silu(a) * b).astype(o_ref.dtype)


def _swiglu_up(lay, h, w1, w3, w_index):
    d = h.shape[1]
    f = w1.shape[-1]
    tn = 512
    w_spec = pl.BlockSpec((None,) * len(w_index) + (d, tn), lambda j, i: (*w_index, 0, j))
    return pl.pallas_call(
        _swiglu_up_kernel,
        grid=(f // tn, lay.n_tiles),
        in_specs=[pl.BlockSpec((lay.tm, d), lambda j, i: (i, 0)), w_spec, w_spec],
        out_specs=pl.BlockSpec((lay.tm, tn), lambda j, i: (i, j)),
        out_shape=jax.ShapeDtypeStruct((lay.rows, f), BF16),
        compiler_params=_params(2),
        name="swiglu_up",
    )(h, w1, w3)


def _pack_bf16_pairs(y):
    half = y.shape[1] // 2
    lo = lax.bitcast_convert_type(y[:, :half].astype(BF16).astype(F32), jnp.uint32) >> 16
    hi = lax.bitcast_convert_type(y[:, half:].astype(BF16).astype(F32), jnp.uint32)
    return hi | lo


def _unpack_bf16_pairs(w):
    lo = lax.bitcast_convert_type(w << 16, F32).astype(BF16)
    hi = lax.bitcast_convert_type(w & jnp.uint32(0xFFFF0000), F32).astype(BF16)
    return jnp.concatenate([lo, hi], axis=1)


def _norm_router_kernel(x_ref, g_ref, shift_ref, scale_ref, rw_ref, rb_ref,
                        hp_ref, sel_ref, prob_ref, cnt_ref, run_ref):
    @pl.when(pl.program_id(0) == 0)
    def _():
        run_ref[...] = jnp.zeros_like(run_ref)

    x = x_ref[...]
    y = x * lax.rsqrt(jnp.mean(x * x, axis=-1, keepdims=True) + NORM_EPS) * g_ref[...]
    h = y * (1.0 + scale_ref[0]) + shift_ref[0]
    hp_ref[...] = _pack_bf16_pairs(h)

    logits = jnp.dot(h.astype(BF16), rw_ref[...], preferred_element_type=F32) + rb_ref[...]
    lane = lax.broadcasted_iota(jnp.int32, logits.shape, 1)
    neg = jnp.float32(-jnp.inf)
    lg = jnp.where(lane < N_EXPERTS, logits, neg)
    m1 = jnp.max(lg, axis=-1, keepdims=True)
    i1 = jnp.min(jnp.where(lg == m1, lane, LANES), axis=-1, keepdims=True)
    lg2 = jnp.where(lane == i1, neg, lg)
    m2 = jnp.max(lg2, axis=-1, keepdims=True)
    i2 = jnp.min(jnp.where(lg2 == m2, lane, LANES), axis=-1, keepdims=True)
    e2 = jnp.exp(m2 - m1)
    den = 1.0 + e2

    tm = x.shape[0]
    onehot = jnp.where(lane == i1, 1.0, 0.0) + jnp.where(lane == i2, 1.0, 0.0)
    earlier = jnp.where(lax.broadcasted_iota(jnp.int32, (tm, tm), 0) > lax.broadcasted_iota(jnp.int32, (tm, tm), 1),
                        1.0, 0.0).astype(BF16)
    before = jnp.dot(earlier, onehot.astype(BF16), preferred_element_type=F32) + run_ref[0:1, :]
    pos1 = jnp.sum(jnp.where(lane == i1, before, 0.0), axis=-1, keepdims=True).astype(jnp.int32)
    pos2 = jnp.sum(jnp.where(lane == i2, before, 0.0), axis=-1, keepdims=True).astype(jnp.int32)
    run_ref[...] = run_ref[...] + jnp.sum(onehot, axis=0, keepdims=True)
    cnt_ref[...] = run_ref[...]

    sel_ref[...] = jnp.where(lane == 0, i1, jnp.where(lane == 1, i2, jnp.where(lane == 2, pos1,
                                                                                jnp.where(lane == 3, pos2, 0))))
    prob_ref[...] = jnp.where(lane == 0, 1.0 / den, jnp.where(lane == 1, e2 / den, 0.0))


def _norm_router(lay, x_all, g, mod3, layer, k_shift, k_scale, router_w, router_b):
    d = x_all.shape[1]
    w = jnp.zeros((d, LANES), BF16).at[:, :N_EXPERTS].set(router_w.astype(BF16))
    b = jnp.zeros((1, LANES), F32).at[0, :N_EXPERTS].set(router_b.astype(F32))
    lane_block = pl.BlockSpec((lay.tm, LANES), lambda i: (i, 0))
    return pl.pallas_call(
        _norm_router_kernel,
        grid=(lay.n_tiles,),
        in_specs=[
            pl.BlockSpec((lay.tm, d), lambda i: (i, 0)),
            pl.BlockSpec((1, d), lambda i: (0, 0)),
            pl.BlockSpec((1, 1, d), lambda i: (_mod_index(lay, layer, k_shift, i), 0, 0)),
            pl.BlockSpec((1, 1, d), lambda i: (_mod_index(lay, layer, k_scale, i), 0, 0)),
            pl.BlockSpec((d, LANES), lambda i: (0, 0)),
            pl.BlockSpec((1, LANES), lambda i: (0, 0)),
        ],
        out_specs=[
            pl.BlockSpec((lay.tm, d // 2), lambda i: (i, 0)),
            lane_block,
            lane_block,
            pl.BlockSpec((8, LANES), lambda i: (0, 0)),
        ],
        out_shape=[
            jax.ShapeDtypeStruct((lay.rows, d // 2), jnp.uint32),
            jax.ShapeDtypeStruct((lay.rows, LANES), jnp.int32),
            jax.ShapeDtypeStruct((lay.rows, LANES), F32),
            jax.ShapeDtypeStruct((8, LANES), F32),
        ],
        scratch_shapes=[pltpu.VMEM((8, LANES), F32)],
        compiler_params=_params(1),
        name="norm_router",
    )(x_all, g.reshape(1, d), mod3, mod3, w, b)


def _routing_plan(sel, cnt, n_sorted_rows):
    counts = cnt[0, :N_EXPERTS].astype(jnp.int32)
    padded = (counts + MOE_TILE - 1) // MOE_TILE * MOE_TILE
    ends = jnp.cumsum(padded)
    starts = ends - padded
    dest1 = starts[sel[:, 0]] + sel[:, 2]
    dest2 = starts[sel[:, 1]] + sel[:, 3]
    tile_start = jnp.arange(n_sorted_rows // MOE_TILE, dtype=jnp.int32) * MOE_TILE
    tile_expert = jnp.minimum(jnp.sum(tile_start[:, None] >= ends[None, :], axis=1), N_EXPERTS - 1)
    n_used_tiles = (ends[-1:] // MOE_TILE).astype(jnp.int32)
    return dest1, dest2, tile_expert.astype(jnp.int32), n_used_tiles


def _dispatch_kernel(dest1_ref, dest2_ref, n_used_ref, h_ref, o_ref, src_ref, sem):
    i = pl.program_id(0)
    n_tok = h_ref.shape[0]
    tile = o_ref.shape[0]

    @pl.when(i == 0)
    def _():
        def clear(r, carry):
            src_ref[r] = 0
            return carry

        def invert(t, carry):
            src_ref[dest1_ref[t]] = t
            src_ref[dest2_ref[t]] = t
            return carry

        lax.fori_loop(0, src_ref.shape[0], clear, 0, unroll=8)
        lax.fori_loop(0, n_tok, invert, 0, unroll=4)

    def row_copy(r, t):
        return pltpu.make_async_copy(h_ref.at[pl.ds(t, 1)], o_ref.at[pl.ds(r, 1)], sem)

    @pl.when(i < n_used_ref[0])
    def _():
        base = i * tile

        def start(r, carry):
            row_copy(r, src_ref[base + r]).start()
            return carry

        def wait(r, carry):
            row_copy(0, 0).wait()
            return carry

        lax.fori_loop(0, tile, start, 0, unroll=4)
        lax.fori_loop(0, tile, wait, 0, unroll=4)

    @pl.when(i >= n_used_ref[0])
    def _():
        o_ref[...] = jnp.zeros_like(o_ref)


def _dispatch(hp, dest1, dest2, n_used, n_sorted_rows):
    width = hp.shape[1]
    return pl.pallas_call(
        _dispatch_kernel,
        grid_spec=pltpu.PrefetchScalarGridSpec(
            num_scalar_prefetch=3,
            grid=(n_sorted_rows // MOE_TILE,),
            in_specs=[pl.BlockSpec(memory_space=pl.ANY)],
            out_specs=pl.BlockSpec((MOE_TILE, width), lambda i, d1, d2, nu: (i, 0)),
            scratch_shapes=[pltpu.SMEM((n_sorted_rows,), jnp.int32), pltpu.SemaphoreType.DMA(())],
        ),
        out_shape=jax.ShapeDtypeStruct((n_sorted_rows, width), hp.dtype),
        compiler_params=_params(1),
        name="moe_dispatch",
    )(dest1, dest2, n_used, hp)


def _grouped_up_kernel(tile_expert_ref, n_used_ref, a_ref, w1_ref, w3_ref, o_ref):
    i = pl.program_id(1)

    @pl.when(i < n_used_ref[0])
    def _():
        h = _unpack_bf16_pairs(a_ref[...])
        a = jnp.dot(h, w1_ref[...].astype(BF16), preferred_element_type=F32)
        b = jnp.dot(h, w3_ref[...].astype(BF16), preferred_element_type=F32)
        o_ref[...] = (_silu(a) * b).astype(o_ref.dtype)

    @pl.when(i >= n_used_ref[0])
    def _():
        o_ref[...] = jnp.zeros_like(o_ref)


def _grouped_up(xs, w1, w3, moe_layer, tile_expert, n_used):
    n_rows, half = xs.shape
    d, f = w1.shape[-2:]
    tn = 512
    w_spec = pl.BlockSpec((None, None, d, tn), lambda j, i, te, nu: (moe_layer, te[i], 0, j))
    return pl.pallas_call(
        _grouped_up_kernel,
        grid_spec=pltpu.PrefetchScalarGridSpec(
            num_scalar_prefetch=2,
            grid=(f // tn, n_rows // MOE_TILE),
            in_specs=[pl.BlockSpec((MOE_TILE, half), lambda j, i, te, nu: (i, 0)), w_spec, w_spec],
            out_specs=pl.BlockSpec((MOE_TILE, tn), lambda j, i, te, nu: (i, j)),
        ),
        out_shape=jax.ShapeDtypeStruct((n_rows, f), BF16),
        compiler_params=_params(2),
        name="moe_up",
    )(tile_expert, n_used, xs, w1, w3)


def _grouped_down_kernel(tile_expert_ref, n_used_ref, a_ref, w_ref, o_ref):
    i = pl.program_id(1)

    @pl.when(i < n_used_ref[0])
    def _():
        o_ref[...] = jnp.dot(a_ref[...], w_ref[...].astype(BF16), preferred_element_type=F32)

    @pl.when(i >= n_used_ref[0])
    def _():
        o_ref[...] = jnp.zeros_like(o_ref)


def _grouped_down(act, w2, moe_layer, tile_expert, n_used):
    n_rows, f = act.shape
    n = w2.shape[-1]
    tn = 512
    return pl.pallas_call(
        _grouped_down_kernel,
        grid_spec=pltpu.PrefetchScalarGridSpec(
            num_scalar_prefetch=2,
            grid=(n // tn, n_rows // MOE_TILE),
            in_specs=[
                pl.BlockSpec((MOE_TILE, f), lambda j, i, te, nu: (i, 0)),
                pl.BlockSpec((None, None, f, tn), lambda j, i, te, nu: (moe_layer, te[i], 0, j)),
            ],
            out_specs=pl.BlockSpec((MOE_TILE, tn), lambda j, i, te, nu: (i, j)),
        ),
        out_shape=jax.ShapeDtypeStruct((n_rows, n), F32),
        compiler_params=_params(2),
        name="moe_down",
    )(tile_expert, n_used, act, w2)


def _combine_kernel(dest1_ref, dest2_ref, x_ref, prob_ref, gate_ref, ys_ref, o_ref, buf1, buf2, sem1, sem2):
    tm = x_ref.shape[0]
    base = pl.program_id(0) * tm

    def copy1(t, d):
        return pltpu.make_async_copy(ys_ref.at[pl.ds(d, 1)], buf1.at[pl.ds(t, 1)], sem1)

    def copy2(t, d):
        return pltpu.make_async_copy(ys_ref.at[pl.ds(d, 1)], buf2.at[pl.ds(t, 1)], sem2)

    def start(t, carry):
        copy1(t, dest1_ref[base + t]).start()
        copy2(t, dest2_ref[base + t]).start()
        return carry

    def wait(t, carry):
        copy1(0, 0).wait()
        copy2(0, 0).wait()
        return carry

    lax.fori_loop(0, tm, start, 0)
    lax.fori_loop(0, tm, wait, 0)
    p = prob_ref[...]
    f = p[:, 0:1] * buf1[...] + p[:, 1:2] * buf2[...]
    o_ref[...] = x_ref[...] + gate_ref[0] * f


def _combine(lay, x_all, prob, ys, dest1, dest2, mod3, layer, k_gate):
    d = x_all.shape[1]
    return pl.pallas_call(
        _combine_kernel,
        grid_spec=pltpu.PrefetchScalarGridSpec(
            num_scalar_prefetch=2,
            grid=(lay.n_tiles,),
            in_specs=[
                pl.BlockSpec((lay.tm, d), lambda i, d1, d2: (i, 0)),
                pl.BlockSpec((lay.tm, LANES), lambda i, d1, d2: (i, 0)),
                pl.BlockSpec((1, 1, d), lambda i, d1, d2: (_mod_index(lay, layer, k_gate, i), 0, 0)),
                pl.BlockSpec(memory_space=pl.ANY),
            ],
            out_specs=pl.BlockSpec((lay.tm, d), lambda i, d1, d2: (i, 0)),
            scratch_shapes=[
                pltpu.VMEM((lay.tm, d), F32),
                pltpu.VMEM((lay.tm, d), F32),
                pltpu.SemaphoreType.DMA(()),
                pltpu.SemaphoreType.DMA(()),
            ],
        ),
        out_shape=jax.ShapeDtypeStruct((lay.rows, d), F32),
        compiler_params=_params(1),
        name="moe_combine",
    )(dest1, dest2, x_all, prob, mod3, ys)


def _moe_ffn(lay, x_all, norm_g, mod3, layer, router_w, router_b, w1, w3, w2, moe_layer):
    hp, sel, prob, cnt = _norm_router(lay, x_all, norm_g, mod3, layer, 3, 4, router_w, router_b)
    n_sorted_rows = 2 * lay.rows + N_EXPERTS * MOE_TILE
    dest1, dest2, tile_expert, n_used = _routing_plan(sel, cnt, n_sorted_rows)
    xs = _dispatch(hp, dest1, dest2, n_used, n_sorted_rows)
    act = _grouped_up(xs, w1, w3, moe_layer, tile_expert, n_used)
    ys = _grouped_down(act, w2, moe_layer, tile_expert, n_used)
    return _combine(lay, x_all, prob, ys, dest1, dest2, mod3, layer, 5)


def _rope_tables(lay):
    t = jnp.arange(lay.seq)
    quarter = RET_QK_DIM // 4
    freqs = ROPE_BASE ** (-jnp.arange(quarter, dtype=F32) / quarter)

    def tables(pos):
        ang = pos.astype(F32)[:, None] * freqs[None, :]
        cos, sin = jnp.cos(ang), jnp.sin(ang)
        return jnp.concatenate([cos, cos], axis=-1), jnp.concatenate([-sin, sin], axis=-1)

    cos_r, sin_r = tables(t // GRID_W)
    cos_c, sin_c = tables(t % GRID_W)
    cos_l = jnp.concatenate([cos_r, cos_c], axis=-1)
    sin_l = jnp.concatenate([sin_r, sin_c], axis=-1)
    n_ctx = lay.batch * lay.ctx_len
    cos_t = jnp.concatenate([jnp.ones((n_ctx, RET_QK_DIM), F32)] + [cos_l] * lay.batch, axis=0)
    sin_t = jnp.concatenate([jnp.zeros((n_ctx, RET_QK_DIM), F32)] + [sin_l] * lay.batch, axis=0)
    return cos_t, sin_t


def kernel(x, c, ctx, c_ctx, w_mod, b_mod, norm1_g, norm2_g, w_in, sgu_ln_g, sgu_ln_b, sgu_w, sgu_b,
           ret_log_decay, w_proj_a, w_proj_r, w_out, ffn_w1, ffn_w3, ffn_w2, router_w, router_b,
           moe_w1, moe_w3, moe_w2, final_norm_g):
    batch, seq, d = x.shape
    ctx_len = ctx.shape[1]
    depth = w_mod.shape[0]
    lay = _Layout(batch, seq, ctx_len)

    cvec = jnp.zeros((MOD_ROWS, d), F32).at[:batch].set(c).at[batch].set(c_ctx)
    mod = _modulation(cvec, w_mod, b_mod)
    mod3 = mod.reshape(depth * MOD_ROWS * N_MOD, 1, d)

    cos_t, sin_t = _rope_tables(lay)
    x_all = jnp.concatenate([ctx.reshape(batch * ctx_len, d), x.reshape(batch * seq, d)], axis=0)

    w_in_b = w_in.astype(BF16)
    w_proj_a_b = w_proj_a.astype(BF16)
    w_proj_r_b = w_proj_r.astype(BF16)
    w_out_b = w_out.astype(BF16)
    sgu_w_b = sgu_w.astype(BF16)
    ffn_w2_b = ffn_w2.astype(BF16)
    log_decay = ret_log_decay.astype(F32)

    for layer in range(depth):
        h = _norm_modulate(lay, x_all, norm1_g[layer], mod3, layer, 0, 1, BF16)
        qk = _in_proj(lay, h, w_in_b, layer, Q0, V0 - Q0, "rope", (cos_t, sin_t))
        v = _in_proj(lay, h, w_in_b, layer, V0, G0 - V0, "plain")
        g = _in_proj(lay, h, w_in_b, layer, G0, UA0 - G0, "silu")
        uv = _in_proj(lay, h, w_in_b, layer, UA0, GA0 - UA0, "gelu")
        gates = _in_proj(lay, h, w_in_b, layer, GA0, IN_COLS - GA0, "sigmoid")
        o_bwd = _retention_pass(lay, qk, v, log_decay[layer, 1], reverse=True)
        ret = _retention_pass(lay, qk, v, log_decay[layer, 0], o_bwd, g, reverse=False)
        sgu = _sgu(lay, uv, sgu_ln_g[layer], sgu_ln_b[layer], sgu_w_b[layer], sgu_b[layer].T)
        y = _merge(lay, sgu, ret, gates, w_proj_a_b, w_proj_r_b, layer)
        x_all = _proj_resid(lay, y, w_out_b, (layer,), x_all, mod3, layer, 2)
        i = layer // 2
        if layer % 2 == 0:
            h2 = _norm_modulate(lay, x_all, norm2_g[layer], mod3, layer, 3, 4, BF16)
            act = _swiglu_up(lay, h2, ffn_w1, ffn_w3, (i,))
            x_all = _proj_resid(lay, act, ffn_w2_b, (i,), x_all, mod3, layer, 5)
        else:
            x_all = _moe_ffn(lay, x_all, norm2_g[layer], mod3, layer, router_w[i], router_b[i],
                             moe_w1, moe_w3, moe_w2, i)

    out = _final_norm(lay, x_all, final_norm_g)
    return out.reshape(batch, seq, d)
```

```python
import functools

import jax
import jax.numpy as jnp
from jax import lax
from jax.experimental import pallas as pl
from jax.experimental.pallas import tpu as pltpu

F32 = jnp.float32
BF16 = jnp.bfloat16

GRID_W = 64
CHUNK = 128
SGU_GROUPS = 8
SGU_WIDTH = 2048
RET_HEADS = 8
RET_QK_DIM = 256
RET_V_DIM = 512
RET_QK = RET_HEADS * RET_QK_DIM
RET_V = RET_HEADS * RET_V_DIM
ROPE_BASE = 10000.0
N_EXPERTS = 8
N_MOD = 6
NORM_EPS = 1e-6
MOD_ROWS = 8
LANES = 128
MOE_TILE = 512

VMEM_LIMIT_BYTES = 52 * 1024 * 1024

Q0 = 0
K0 = Q0 + RET_QK
V0 = K0 + RET_QK
G0 = V0 + RET_V
UA0 = G0 + RET_V
VA0 = UA0 + SGU_WIDTH
GA0 = VA0 + SGU_WIDTH
GR0 = GA0 + 2048
IN_COLS = GR0 + 2048


def _params(n_grid_dims):
    return pltpu.CompilerParams(
        dimension_semantics=("arbitrary",) * n_grid_dims,
        vmem_limit_bytes=VMEM_LIMIT_BYTES,
    )


def _gelu_tanh(x):
    return 0.5 * x * (1.0 + jnp.tanh(0.7978845608028654 * (x + 0.044715 * (x * x * x))))


def _silu(x):
    return x * jax.nn.sigmoid(x)


def _mod_kernel(c_ref, w_ref, b_ref, o_ref):
    sc = _silu(c_ref[...]).astype(BF16)
    o_ref[...] = jnp.dot(sc, w_ref[...].astype(BF16), preferred_element_type=F32) + b_ref[...]


def _modulation(cvec, w_mod, b_mod):
    depth, d, n = w_mod.shape
    tn = 1024
    return pl.pallas_call(
        _mod_kernel,
        grid=(depth, n // tn),
        in_specs=[
            pl.BlockSpec((MOD_ROWS, d), lambda l, j: (0, 0)),
            pl.BlockSpec((None, d, tn), lambda l, j: (l, 0, j)),
            pl.BlockSpec((None, 1, tn), lambda l, j: (l, 0, j)),
        ],
        out_specs=pl.BlockSpec((None, MOD_ROWS, tn), lambda l, j: (l, 0, j)),
        out_shape=jax.ShapeDtypeStruct((depth, MOD_ROWS, n), F32),
        compiler_params=_params(2),
        name="modulation",
    )(cvec, w_mod, b_mod.reshape(depth, 1, n))


def _norm_mod_kernel(x_ref, g_ref, shift_ref, scale_ref, o_ref):
    x = x_ref[...]
    y = x * lax.rsqrt(jnp.mean(x * x, axis=-1, keepdims=True) + NORM_EPS) * g_ref[...]
    o_ref[...] = (y * (1.0 + scale_ref[0]) + shift_ref[0]).astype(o_ref.dtype)


def _norm_kernel(x_ref, g_ref, o_ref):
    x = x_ref[...]
    y = x * lax.rsqrt(jnp.mean(x * x, axis=-1, keepdims=True) + NORM_EPS) * g_ref[...]
    o_ref[...] = y.astype(o_ref.dtype)


class _Layout:
    def __init__(self, batch, seq, ctx_len):
        self.batch, self.seq, self.ctx_len = batch, seq, ctx_len
        self.tm = batch * ctx_len
        assert seq % self.tm == 0 and ctx_len % CHUNK == 0 and seq % CHUNK == 0
        self.tiles_per_batch = seq // self.tm
        self.n_tiles = 1 + batch * self.tiles_per_batch
        self.rows = self.n_tiles * self.tm
        self.ctx_chunks = ctx_len // CHUNK
        self.lat_chunks = seq // CHUNK

    def mod_row(self, tile):
        return jnp.where(tile == 0, self.batch, (tile - 1) // self.tiles_per_batch)


def _mod_index(lay, layer, k, tile):
    return layer * MOD_ROWS * N_MOD + lay.mod_row(tile) * N_MOD + k


def _norm_modulate(lay, x_all, g, mod3, layer, k_shift, k_scale, out_dtype):
    d = x_all.shape[1]
    return pl.pallas_call(
        _norm_mod_kernel,
        grid=(lay.n_tiles,),
        in_specs=[
            pl.BlockSpec((lay.tm, d), lambda i: (i, 0)),
            pl.BlockSpec((1, d), lambda i: (0, 0)),
            pl.BlockSpec((1, 1, d), lambda i: (_mod_index(lay, layer, k_shift, i), 0, 0)),
            pl.BlockSpec((1, 1, d), lambda i: (_mod_index(lay, layer, k_scale, i), 0, 0)),
        ],
        out_specs=pl.BlockSpec((lay.tm, d), lambda i: (i, 0)),
        out_shape=jax.ShapeDtypeStruct((lay.rows, d), out_dtype),
        compiler_params=_params(1),
        name="norm_modulate",
    )(x_all, g.reshape(1, d), mod3, mod3)


def _final_norm(lay, x_all, g):
    d = x_all.shape[1]
    n_lat_tiles = lay.n_tiles - 1
    return pl.pallas_call(
        _norm_kernel,
        grid=(n_lat_tiles,),
        in_specs=[
            pl.BlockSpec((lay.tm, d), lambda i: (i + 1, 0)),
            pl.BlockSpec((1, d), lambda i: (0, 0)),
        ],
        out_specs=pl.BlockSpec((lay.tm, d), lambda i: (i, 0)),
        out_shape=jax.ShapeDtypeStruct((n_lat_tiles * lay.tm, d), F32),
        compiler_params=_params(1),
        name="final_norm",
    )(x_all, g.reshape(1, d))


IN_PROJ_TN = 2048
IN_PROJ_SUB = 512


def _in_proj_kernel(h_ref, w_ref, *rest, kind):
    o_ref = rest[-1]
    h = h_ref[...]
    if kind == "rope":
        cos_ref, sin_ref = rest[:2]
        scale = jnp.where(pl.program_id(0) == 0, 1.0, RET_QK_DIM ** -0.5).astype(F32)
    for c0 in range(0, IN_PROJ_TN, IN_PROJ_SUB):
        acc = jnp.dot(h, w_ref[:, c0:c0 + IN_PROJ_SUB].astype(BF16), preferred_element_type=F32)
        if kind == "rope":
            for s in range(0, IN_PROJ_SUB, LANES):
                t = (c0 + s) % RET_QK_DIM
                xs = acc[:, s:s + LANES]
                r = xs * cos_ref[:, t:t + LANES] + pltpu.roll(xs, LANES // 2, 1) * sin_ref[:, t:t + LANES]
                o_ref[:, c0 + s:c0 + s + LANES] = (r * scale).astype(o_ref.dtype)
        else:
            if kind == "silu":
                acc = _silu(acc)
            elif kind == "gelu":
                acc = _gelu_tanh(acc)
            elif kind == "sigmoid":
                acc = jax.nn.sigmoid(acc)
            else:
                assert kind == "plain"
            o_ref[:, c0:c0 + IN_PROJ_SUB] = acc.astype(o_ref.dtype)


def _in_proj(lay, h, w_in, layer, col0, n_cols, kind, tables=()):
    d = h.shape[1]
    tn = IN_PROJ_TN
    table_spec = pl.BlockSpec((lay.tm, RET_QK_DIM), lambda j, i: (i, 0))
    return pl.pallas_call(
        functools.partial(_in_proj_kernel, kind=kind),
        grid=(n_cols // tn, lay.n_tiles),
        in_specs=[
            pl.BlockSpec((lay.tm, d), lambda j, i: (i, 0)),
            pl.BlockSpec((None, d, tn), lambda j, i: (layer, 0, col0 // tn + j)),
        ] + [table_spec] * len(tables),
        out_specs=pl.BlockSpec((lay.tm, tn), lambda j, i: (i, j)),
        out_shape=jax.ShapeDtypeStruct((lay.rows, n_cols), BF16),
        compiler_params=_params(2),
        name="in_proj_" + kind,
    )(h, w_in, *tables)


def _retention_kernel(lg_ref, q_ref, k_ref, v_ref, *rest, reverse, final):
    if final:
        other_ref, gate_ref, o_ref, s_ref = rest
    else:
        o_ref, s_ref = rest

    @pl.when(pl.program_id(1) == 0)
    def _():
        s_ref[...] = jnp.zeros_like(s_ref)

    c = CHUNK
    qi = lax.broadcasted_iota(jnp.int32, (c, c), 0)
    kj = lax.broadcasted_iota(jnp.int32, (c, c), 1)
    diff = ((kj - qi) if reverse else (qi - kj)).astype(F32)
    pos = lax.broadcasted_iota(jnp.int32, (c, 1), 0).astype(F32)
    q_steps = (c - pos) if reverse else (pos + 1.0)
    k_steps = pos if reverse else (c - 1.0 - pos)

    for h in range(RET_HEADS):
        log_g = lg_ref[h]
        intra = jnp.where(diff >= 0, jnp.exp(jnp.maximum(diff, 0.0) * log_g), 0.0)
        q_decay = jnp.exp(q_steps * log_g)
        k_decay = jnp.exp(k_steps * log_g)
        chunk_decay = jnp.exp(jnp.full((1, 1), float(c), F32) * log_g)

        qh = q_ref[:, h * RET_QK_DIM:(h + 1) * RET_QK_DIM]
        kh = k_ref[:, h * RET_QK_DIM:(h + 1) * RET_QK_DIM]
        vh = v_ref[:, h * RET_V_DIM:(h + 1) * RET_V_DIM]
        s = s_ref[h]

        scores = lax.dot_general(qh, kh, (((1,), (1,)), ((), ())), preferred_element_type=F32) * intra
        o = (jnp.dot(scores.astype(BF16), vh, preferred_element_type=F32)
             + jnp.dot(qh, s.astype(BF16), preferred_element_type=F32) * q_decay)
        k_dec_t = (kh.astype(F32) * k_decay).T.astype(BF16)
        s_ref[h] = s * chunk_decay + jnp.dot(k_dec_t, vh, preferred_element_type=F32)

        sl = slice(h * RET_V_DIM, (h + 1) * RET_V_DIM)
        if final:
            o = o + other_ref[:, sl]
            mu = jnp.mean(o, axis=-1, keepdims=True)
            oc = o - mu
            var = jnp.mean(oc * oc, axis=-1, keepdims=True)
            o = oc * lax.rsqrt(var + NORM_EPS) * gate_ref[:, sl].astype(F32)
        o_ref[:, sl] = o.astype(o_ref.dtype)


def _retention_pass(lay, qk, v, log_g, other=None, gate=None, *, reverse):
    final = other is not None
    n_chunks = lay.ctx_chunks + lay.lat_chunks

    def row_block(b, n):
        ctx_blk = b * lay.ctx_chunks + ((lay.ctx_chunks - 1 - n) if reverse else n)
        m = n - lay.ctx_chunks
        lat_blk = lay.batch * lay.ctx_chunks + b * lay.lat_chunks + ((lay.lat_chunks - 1 - m) if reverse else m)
        return jnp.where(n < lay.ctx_chunks, ctx_blk, lat_blk)

    in_specs = [
        pl.BlockSpec((CHUNK, RET_QK), lambda b, n, lg: (row_block(b, n), 0)),
        pl.BlockSpec((CHUNK, RET_QK), lambda b, n, lg: (row_block(b, n), 1)),
        pl.BlockSpec((CHUNK, RET_V), lambda b, n, lg: (row_block(b, n), 0)),
    ]
    args = [qk, qk, v]
    if final:
        in_specs += [pl.BlockSpec((CHUNK, RET_V), lambda b, n, lg: (row_block(b, n), 0))] * 2
        args += [other, gate]
    return pl.pallas_call(
        functools.partial(_retention_kernel, reverse=reverse, final=final),
        grid_spec=pltpu.PrefetchScalarGridSpec(
            num_scalar_prefetch=1,
            grid=(lay.batch, n_chunks),
            in_specs=in_specs,
            out_specs=pl.BlockSpec((CHUNK, RET_V), lambda b, n, lg: (row_block(b, n), 0)),
            scratch_shapes=[pltpu.VMEM((RET_HEADS, RET_QK_DIM, RET_V_DIM), F32)],
        ),
        out_shape=jax.ShapeDtypeStruct((lay.rows, RET_V), BF16 if final else F32),
        compiler_params=_params(2),
        name="retention_fwd_merge" if final else "retention_bwd",
    )(log_g, *args)


def _sgu_kernel(u_ref, v_ref, lng_ref, lnb_ref, ws_ref, bst_ref, o_ref):
    v = v_ref[...].astype(F32)
    mu = jnp.mean(v, axis=-1, keepdims=True)
    vc = v - mu
    var = jnp.mean(vc * vc, axis=-1, keepdims=True)
    vn = (vc * lax.rsqrt(var + NORM_EPS) * lng_ref[...] + lnb_ref[...]).astype(BF16)
    gw = SGU_WIDTH // SGU_GROUPS
    for g in range(SGU_GROUPS):
        sl = slice(g * gw, (g + 1) * gw)
        s = jnp.dot(ws_ref[g].astype(BF16), vn[:, sl], preferred_element_type=F32) + bst_ref[:, g:g + 1]
        o_ref[:, sl] = (u_ref[:, sl].astype(F32) * s).astype(o_ref.dtype)


def _sgu(lay, uv, ln_g, ln_b, w_s, b_s_t):
    return pl.pallas_call(
        _sgu_kernel,
        grid=(lay.rows // CHUNK,),
        in_specs=[
            pl.BlockSpec((CHUNK, SGU_WIDTH), lambda i: (i, 0)),
            pl.BlockSpec((CHUNK, SGU_WIDTH), lambda i: (i, 1)),
            pl.BlockSpec((1, SGU_WIDTH), lambda i: (0, 0)),
            pl.BlockSpec((1, SGU_WIDTH), lambda i: (0, 0)),
            pl.BlockSpec((SGU_GROUPS, CHUNK, CHUNK), lambda i: (0, 0, 0)),
            pl.BlockSpec((CHUNK, SGU_GROUPS), lambda i: (0, 0)),
        ],
        out_specs=pl.BlockSpec((CHUNK, SGU_WIDTH), lambda i: (i, 0)),
        out_shape=jax.ShapeDtypeStruct((lay.rows, SGU_WIDTH), BF16),
        compiler_params=_params(1),
        name="sgu",
    )(uv, uv, ln_g.reshape(1, -1), ln_b.reshape(1, -1), w_s, b_s_t)


def _merge_kernel(sgu_ref, ret_ref, wa_ref, wr_ref, ga_ref, gr_ref, o_ref):
    a = jnp.dot(sgu_ref[...], wa_ref[...].astype(BF16), preferred_element_type=F32)
    r = jnp.dot(ret_ref[...], wr_ref[...].astype(BF16), preferred_element_type=F32)
    o_ref[...] = (ga_ref[...].astype(F32) * a + gr_ref[...].astype(F32) * r).astype(o_ref.dtype)


def _merge(lay, sgu, ret, gates, w_proj_a, w_proj_r, layer):
    n = w_proj_a.shape[2]
    tn = 512
    return pl.pallas_call(
        _merge_kernel,
        grid=(n // tn, lay.n_tiles),
        in_specs=[
            pl.BlockSpec((lay.tm, SGU_WIDTH), lambda j, i: (i, 0)),
            pl.BlockSpec((lay.tm, RET_V), lambda j, i: (i, 0)),
            pl.BlockSpec((None, SGU_WIDTH, tn), lambda j, i: (layer, 0, j)),
            pl.BlockSpec((None, RET_V, tn), lambda j, i: (layer, 0, j)),
            pl.BlockSpec((lay.tm, tn), lambda j, i: (i, j)),
            pl.BlockSpec((lay.tm, tn), lambda j, i: (i, n // tn + j)),
        ],
        out_specs=pl.BlockSpec((lay.tm, tn), lambda j, i: (i, j)),
        out_shape=jax.ShapeDtypeStruct((lay.rows, n), BF16),
        compiler_params=_params(2),
        name="merge",
    )(sgu, ret, w_proj_a, w_proj_r, gates, gates)


def _proj_resid_kernel(a_ref, w_ref, resid_ref, gate_ref, o_ref):
    acc = jnp.dot(a_ref[...], w_ref[...].astype(BF16), preferred_element_type=F32)
    o_ref[...] = resid_ref[...] + gate_ref[0] * acc


def _proj_resid(lay, a, w, w_index, resid, mod3, layer, k_gate):
    kdim = a.shape[1]
    n = w.shape[-1]
    tn = 512
    w_spec = pl.BlockSpec((None,) * len(w_index) + (kdim, tn), lambda j, i: (*w_index, 0, j))
    return pl.pallas_call(
        _proj_resid_kernel,
        grid=(n // tn, lay.n_tiles),
        in_specs=[
            pl.BlockSpec((lay.tm, kdim), lambda j, i: (i, 0)),
            w_spec,
            pl.BlockSpec((lay.tm, tn), lambda j, i: (i, j)),
            pl.BlockSpec((1, 1, tn), lambda j, i: (_mod_index(lay, layer, k_gate, i), 0, j)),
        ],
        out_specs=pl.BlockSpec((lay.tm, tn), lambda j, i: (i, j)),
        out_shape=jax.ShapeDtypeStruct((lay.rows, n), F32),
        compiler_params=_params(2),
        name="proj_resid",
    )(a, w, resid, mod3)


def _swiglu_up_kernel(h_ref, w1_ref, w3_ref, o_ref):
    h = h_ref[...]
    a = jnp.dot(h, w1_ref[...].astype(BF16), preferred_element_type=F32)
    b = jnp.dot(h, w3_ref[...].astype(BF16), preferred_element_type=F32)
    o_ref[...] = (_silu(a) * b).astype(o_ref.dtype)


def _swiglu_up(lay, h, w1, w3, w_index):
    d = h.shape[1]
    f = w1.shape[-1]
    tn = 512
    w_spec = pl.BlockSpec((None,) * len(w_index) + (d, tn), lambda j, i: (*w_index, 0, j))
    return pl.pallas_call(
        _swiglu_up_kernel,
        grid=(f // tn, lay.n_tiles),
        in_specs=[pl.BlockSpec((lay.tm, d), lambda j, i: (i, 0)), w_spec, w_spec],
        out_specs=pl.BlockSpec((lay.tm, tn), lambda j, i: (i, j)),
        out_shape=jax.ShapeDtypeStruct((lay.rows, f), BF16),
        compiler_params=_params(2),
        name="swiglu_up",
    )(h, w1, w3)


def _pack_bf16_pairs(y):
    half = y.shape[1] // 2
    lo = lax.bitcast_convert_type(y[:, :half].astype(BF16).astype(F32), jnp.uint32) >> 16
    hi = lax.bitcast_convert_type(y[:, half:].astype(BF16).astype(F32), jnp.uint32)
    return hi | lo


def _unpack_bf16_pairs(w):
    lo = lax.bitcast_convert_type(w << 16, F32).astype(BF16)
    hi = lax.bitcast_convert_type(w & jnp.uint32(0xFFFF0000), F32).astype(BF16)
    return jnp.concatenate([lo, hi], axis=1)


def _norm_router_kernel(x_ref, g_ref, shift_ref, scale_ref, rw_ref, rb_ref,
                        hp_ref, sel_ref, prob_ref, cnt_ref, run_ref):
    @pl.when(pl.program_id(0) == 0)
    def _():
        run_ref[...] = jnp.zeros_like(run_ref)

    x = x_ref[...]
    y = x * lax.rsqrt(jnp.mean(x * x, axis=-1, keepdims=True) + NORM_EPS) * g_ref[...]
    h = y * (1.0 + scale_ref[0]) + shift_ref[0]
    hp_ref[...] = _pack_bf16_pairs(h)

    logits = jnp.dot(h.astype(BF16), rw_ref[...], preferred_element_type=F32) + rb_ref[...]
    lane = lax.broadcasted_iota(jnp.int32, logits.shape, 1)
    neg = jnp.float32(-jnp.inf)
    lg = jnp.where(lane < N_EXPERTS, logits, neg)
    m1 = jnp.max(lg, axis=-1, keepdims=True)
    i1 = jnp.min(jnp.where(lg == m1, lane, LANES), axis=-1, keepdims=True)
    lg2 = jnp.where(lane == i1, neg, lg)
    m2 = jnp.max(lg2, axis=-1, keepdims=True)
    i2 = jnp.min(jnp.where(lg2 == m2, lane, LANES), axis=-1, keepdims=True)
    e2 = jnp.exp(m2 - m1)
    den = 1.0 + e2

    tm = x.shape[0]
    onehot = jnp.where(lane == i1, 1.0, 0.0) + jnp.where(lane == i2, 1.0, 0.0)
    earlier = jnp.where(lax.broadcasted_iota(jnp.int32, (tm, tm), 0) > lax.broadcasted_iota(jnp.int32, (tm, tm), 1),
                        1.0, 0.0).astype(BF16)
    before = jnp.dot(earlier, onehot.astype(BF16), preferred_element_type=F32) + run_ref[0:1, :]
    pos1 = jnp.sum(jnp.where(lane == i1, before, 0.0), axis=-1, keepdims=True).astype(jnp.int32)
    pos2 = jnp.sum(jnp.where(lane == i2, before, 0.0), axis=-1, keepdims=True).astype(jnp.int32)
    run_ref[...] = run_ref[...] + jnp.sum(onehot, axis=0, keepdims=True)
    cnt_ref[...] = run_ref[...]

    sel_ref[...] = jnp.where(lane == 0, i1, jnp.where(lane == 1, i2, jnp.where(lane == 2, pos1,
                                                                                jnp.where(lane == 3, pos2, 0))))
    prob_ref[...] = jnp.where(lane == 0, 1.0 / den, jnp.where(lane == 1, e2 / den, 0.0))


def _norm_router(lay, x_all, g, mod3, layer, k_shift, k_scale, router_w, router_b):
    d = x_all.shape[1]
    w = jnp.zeros((d, LANES), BF16).at[:, :N_EXPERTS].set(router_w.astype(BF16))
    b = jnp.zeros((1, LANES), F32).at[0, :N_EXPERTS].set(router_b.astype(F32))
    lane_block = pl.BlockSpec((lay.tm, LANES), lambda i: (i, 0))
    return pl.pallas_call(
        _norm_router_kernel,
        grid=(lay.n_tiles,),
        in_specs=[
            pl.BlockSpec((lay.tm, d), lambda i: (i, 0)),
            pl.BlockSpec((1, d), lambda i: (0, 0)),
            pl.BlockSpec((1, 1, d), lambda i: (_mod_index(lay, layer, k_shift, i), 0, 0)),
            pl.BlockSpec((1, 1, d), lambda i: (_mod_index(lay, layer, k_scale, i), 0, 0)),
            pl.BlockSpec((d, LANES), lambda i: (0, 0)),
            pl.BlockSpec((1, LANES), lambda i: (0, 0)),
        ],
        out_specs=[
            pl.BlockSpec((lay.tm, d // 2), lambda i: (i, 0)),
            lane_block,
            lane_block,
            pl.BlockSpec((8, LANES), lambda i: (0, 0)),
        ],
        out_shape=[
            jax.ShapeDtypeStruct((lay.rows, d // 2), jnp.uint32),
            jax.ShapeDtypeStruct((lay.rows, LANES), jnp.int32),
            jax.ShapeDtypeStruct((lay.rows, LANES), F32),
            jax.ShapeDtypeStruct((8, LANES), F32),
        ],
        scratch_shapes=[pltpu.VMEM((8, LANES), F32)],
        compiler_params=_params(1),
        name="norm_router",
    )(x_all, g.reshape(1, d), mod3, mod3, w, b)


def _routing_plan(sel, cnt, n_sorted_rows):
    counts = cnt[0, :N_EXPERTS].astype(jnp.int32)
    padded = (counts + MOE_TILE - 1) // MOE_TILE * MOE_TILE
    ends = jnp.cumsum(padded)
    starts = ends - padded
    dest1 = starts[sel[:, 0]] + sel[:, 2]
    dest2 = starts[sel[:, 1]] + sel[:, 3]
    tile_start = jnp.arange(n_sorted_rows // MOE_TILE, dtype=jnp.int32) * MOE_TILE
    tile_expert = jnp.minimum(jnp.sum(tile_start[:, None] >= ends[None, :], axis=1), N_EXPERTS - 1)
    n_used_tiles = (ends[-1:] // MOE_TILE).astype(jnp.int32)
    present = counts > 0
    run_expert = jnp.argsort(jnp.logical_not(present), stable=True).astype(jnp.int32)
    run_of_expert = jnp.cumsum(present.astype(jnp.int32)) - 1
    tile_run = jnp.maximum(run_of_expert[tile_expert], 0).astype(jnp.int32)
    run_first_tile = (starts[run_expert] // MOE_TILE).astype(jnp.int32)
    n_runs = jnp.sum(present.astype(jnp.int32)).reshape(1)
    return dest1, dest2, (tile_run, run_first_tile, run_expert, n_runs, n_used_tiles)


def _dispatch_kernel(dest1_ref, dest2_ref, n_used_ref, h_ref, o_ref, src_ref, sem):
    i = pl.program_id(0)
    n_tok = h_ref.shape[0]
    tile = o_ref.shape[0]

    @pl.when(i == 0)
    def _():
        def clear(r, carry):
            src_ref[r] = 0
            return carry

        def invert(t, carry):
            src_ref[dest1_ref[t]] = t
            src_ref[dest2_ref[t]] = t
            return carry

        lax.fori_loop(0, src_ref.shape[0], clear, 0, unroll=8)
        lax.fori_loop(0, n_tok, invert, 0, unroll=4)

    def row_copy(r, t):
        return pltpu.make_async_copy(h_ref.at[pl.ds(t, 1)], o_ref.at[pl.ds(r, 1)], sem)

    @pl.when(i < n_used_ref[0])
    def _():
        base = i * tile

        def start(r, carry):
            row_copy(r, src_ref[base + r]).start()
            return carry

        def wait(r, carry):
            row_copy(0, 0).wait()
            return carry

        lax.fori_loop(0, tile, start, 0, unroll=4)
        lax.fori_loop(0, tile, wait, 0, unroll=4)

    @pl.when(i >= n_used_ref[0])
    def _():
        o_ref[...] = jnp.zeros_like(o_ref)


def _dispatch(hp, dest1, dest2, n_used, n_sorted_rows):
    width = hp.shape[1]
    return pl.pallas_call(
        _dispatch_kernel,
        grid_spec=pltpu.PrefetchScalarGridSpec(
            num_scalar_prefetch=3,
            grid=(n_sorted_rows // MOE_TILE,),
            in_specs=[pl.BlockSpec(memory_space=pl.ANY)],
            out_specs=pl.BlockSpec((MOE_TILE, width), lambda i, d1, d2, nu: (i, 0)),
            scratch_shapes=[pltpu.SMEM((n_sorted_rows,), jnp.int32), pltpu.SemaphoreType.DMA(())],
        ),
        out_shape=jax.ShapeDtypeStruct((n_sorted_rows, width), hp.dtype),
        compiler_params=_params(1),
        name="moe_dispatch",
    )(dest1, dest2, n_used, hp)


def _grouped_kernel(tile_run_ref, run_first_ref, run_expert_ref, n_runs_ref, n_used_ref, a_ref, *rest,
                    moe_layer, tn, swiglu):
    n_mats = 2 if swiglu else 1
    w_hbm = rest[:n_mats]
    o_ref, wbuf, sems = rest[n_mats:]
    j, i = pl.program_id(0), pl.program_id(1)
    n_runs = n_runs_ref[0]
    run = tile_run_ref[i]
    seq = j * n_runs + run
    slot = lax.rem(seq, 2)

    def fetches(jj, rr, slot_):
        expert = run_expert_ref[rr]
        col = pl.multiple_of(jj * tn, tn)
        return [pltpu.make_async_copy(w.at[moe_layer, expert, :, pl.ds(col, tn)], wbuf.at[slot_, m], sems.at[slot_, m])
                for m, w in enumerate(w_hbm)]

    live = i < n_used_ref[0]

    @pl.when(live & (i == run_first_ref[run]))
    def _():
        @pl.when(seq == 0)
        def _():
            for c in fetches(j, run, slot):
                c.start()

        for c in fetches(j, run, slot):
            c.wait()
        wraps = run + 1 == n_runs
        next_j = jnp.where(wraps, j + 1, j)
        next_run = jnp.where(wraps, 0, run + 1)

        @pl.when(next_j < pl.num_programs(0))
        def _():
            for c in fetches(next_j, next_run, 1 - slot):
                c.start()

    @pl.when(live)
    def _():
        if swiglu:
            h = _unpack_bf16_pairs(a_ref[...])
            a = jnp.dot(h, wbuf[slot, 0].astype(BF16), preferred_element_type=F32)
            b = jnp.dot(h, wbuf[slot, 1].astype(BF16), preferred_element_type=F32)
            o_ref[...] = (_silu(a) * b).astype(o_ref.dtype)
        else:
            o_ref[...] = jnp.dot(a_ref[...], wbuf[slot, 0].astype(BF16), preferred_element_type=F32)

    @pl.when(jnp.logical_not(live))
    def _():
        o_ref[...] = jnp.zeros_like(o_ref)


def _grouped_matmul(a, weights, moe_layer, plan, out_dtype, *, swiglu, name):
    n_rows, a_width = a.shape
    kdim, n = weights[0].shape[-2:]
    tn = 512
    n_prefetch = 5
    idx = lambda j, i, *_: (i, 0)
    return pl.pallas_call(
        functools.partial(_grouped_kernel, moe_layer=moe_layer, tn=tn, swiglu=swiglu),
        grid_spec=pltpu.PrefetchScalarGridSpec(
            num_scalar_prefetch=n_prefetch,
            grid=(n // tn, n_rows // MOE_TILE),
            in_specs=[pl.BlockSpec((MOE_TILE, a_width), idx)] + [pl.BlockSpec(memory_space=pl.ANY)] * len(weights),
            out_specs=pl.BlockSpec((MOE_TILE, tn), lambda j, i, *_: (i, j)),
            scratch_shapes=[
                pltpu.VMEM((2, len(weights), kdim, tn), F32),
                pltpu.SemaphoreType.DMA((2, len(weights))),
            ],
        ),
        out_shape=jax.ShapeDtypeStruct((n_rows, n), out_dtype),
        compiler_params=_params(2),
        name=name,
    )(*plan, a, *weights)


def _combine_kernel(dest1_ref, dest2_ref, x_ref, prob_ref, gate_ref, ys_ref, o_ref, buf1, buf2, sem1, sem2):
    tm = x_ref.shape[0]
    base = pl.program_id(0) * tm

    def copy1(t, d):
        return pltpu.make_async_copy(ys_ref.at[pl.ds(d, 1)], buf1.at[pl.ds(t, 1)], sem1)

    def copy2(t, d):
        return pltpu.make_async_copy(ys_ref.at[pl.ds(d, 1)], buf2.at[pl.ds(t, 1)], sem2)

    def start(t, carry):
        copy1(t, dest1_ref[base + t]).start()
        copy2(t, dest2_ref[base + t]).start()
        return carry

    def wait(t, carry):
        copy1(0, 0).wait()
        copy2(0, 0).wait()
        return carry

    lax.fori_loop(0, tm, start, 0)
    lax.fori_loop(0, tm, wait, 0)
    p = prob_ref[...]
    f = p[:, 0:1] * buf1[...] + p[:, 1:2] * buf2[...]
    o_ref[...] = x_ref[...] + gate_ref[0] * f


def _combine(lay, x_all, prob, ys, dest1, dest2, mod3, layer, k_gate):
    d = x_all.shape[1]
    return pl.pallas_call(
        _combine_kernel,
        grid_spec=pltpu.PrefetchScalarGridSpec(
            num_scalar_prefetch=2,
            grid=(lay.n_tiles,),
            in_specs=[
                pl.BlockSpec((lay.tm, d), lambda i, d1, d2: (i, 0)),
                pl.BlockSpec((lay.tm, LANES), lambda i, d1, d2: (i, 0)),
                pl.BlockSpec((1, 1, d), lambda i, d1, d2: (_mod_index(lay, layer, k_gate, i), 0, 0)),
                pl.BlockSpec(memory_space=pl.ANY),
            ],
            out_specs=pl.BlockSpec((lay.tm, d), lambda i, d1, d2: (i, 0)),
            scratch_shapes=[
                pltpu.VMEM((lay.tm, d), F32),
                pltpu.VMEM((lay.tm, d), F32),
                pltpu.SemaphoreType.DMA(()),
                pltpu.SemaphoreType.DMA(()),
            ],
        ),
        out_shape=jax.ShapeDtypeStruct((lay.rows, d), F32),
        compiler_params=_params(1),
        name="moe_combine",
    )(dest1, dest2, x_all, prob, mod3, ys)


def _moe_ffn(lay, x_all, norm_g, mod3, layer, router_w, router_b, w1, w3, w2, moe_layer):
    hp, sel, prob, cnt = _norm_router(lay, x_all, norm_g, mod3, layer, 3, 4, router_w, router_b)
    n_sorted_rows = 2 * lay.rows + N_EXPERTS * MOE_TILE
    dest1, dest2, plan = _routing_plan(sel, cnt, n_sorted_rows)
    xs = _dispatch(hp, dest1, dest2, plan[-1], n_sorted_rows)
    act = _grouped_matmul(xs, (w1, w3), moe_layer, plan, BF16, swiglu=True, name="moe_up")
    ys = _grouped_matmul(act, (w2,), moe_layer, plan, F32, swiglu=False, name="moe_down")
    return _combine(lay, x_all, prob, ys, dest1, dest2, mod3, layer, 5)


def _rope_tables(lay):
    t = jnp.arange(lay.seq)
    quarter = RET_QK_DIM // 4
    freqs = ROPE_BASE ** (-jnp.arange(quarter, dtype=F32) / quarter)

    def tables(pos):
        ang = pos.astype(F32)[:, None] * freqs[None, :]
        cos, sin = jnp.cos(ang), jnp.sin(ang)
        return jnp.concatenate([cos, cos], axis=-1), jnp.concatenate([-sin, sin], axis=-1)

    cos_r, sin_r = tables(t // GRID_W)
    cos_c, sin_c = tables(t % GRID_W)
    cos_l = jnp.concatenate([cos_r, cos_c], axis=-1)
    sin_l = jnp.concatenate([sin_r, sin_c], axis=-1)
    n_ctx = lay.batch * lay.ctx_len
    cos_t = jnp.concatenate([jnp.ones((n_ctx, RET_QK_DIM), F32)] + [cos_l] * lay.batch, axis=0)
    sin_t = jnp.concatenate([jnp.zeros((n_ctx, RET_QK_DIM), F32)] + [sin_l] * lay.batch, axis=0)
    return cos_t, sin_t


def kernel(x, c, ctx, c_ctx, w_mod, b_mod, norm1_g, norm2_g, w_in, sgu_ln_g, sgu_ln_b, sgu_w, sgu_b,
           ret_log_decay, w_proj_a, w_proj_r, w_out, ffn_w1, ffn_w3, ffn_w2, router_w, router_b,
           moe_w1, moe_w3, moe_w2, final_norm_g):
    batch, seq, d = x.shape
    ctx_len = ctx.shape[1]
    depth = w_mod.shape[0]
    lay = _Layout(batch, seq, ctx_len)

    cvec = jnp.zeros((MOD_ROWS, d), F32).at[:batch].set(c).at[batch].set(c_ctx)
    mod = _modulation(cvec, w_mod, b_mod)
    mod3 = mod.reshape(depth * MOD_ROWS * N_MOD, 1, d)

    cos_t, sin_t = _rope_tables(lay)
    x_all = jnp.concatenate([ctx.reshape(batch * ctx_len, d), x.reshape(batch * seq, d)], axis=0)

    log_decay = ret_log_decay.astype(F32)

    for layer in range(depth):
        h = _norm_modulate(lay, x_all, norm1_g[layer], mod3, layer, 0, 1, BF16)
        qk = _in_proj(lay, h, w_in, layer, Q0, V0 - Q0, "rope", (cos_t, sin_t))
        v = _in_proj(lay, h, w_in, layer, V0, G0 - V0, "plain")
        g = _in_proj(lay, h, w_in, layer, G0, UA0 - G0, "silu")
        uv = _in_proj(lay, h, w_in, layer, UA0, GA0 - UA0, "gelu")
        gates = _in_proj(lay, h, w_in, layer, GA0, IN_COLS - GA0, "sigmoid")
        o_bwd = _retention_pass(lay, qk, v, log_decay[layer, 1], reverse=True)
        ret = _retention_pass(lay, qk, v, log_decay[layer, 0], o_bwd, g, reverse=False)
        sgu = _sgu(lay, uv, sgu_ln_g[layer], sgu_ln_b[layer], sgu_w[layer], sgu_b[layer].T)
        y = _merge(lay, sgu, ret, gates, w_proj_a, w_proj_r, layer)
        x_all = _proj_resid(lay, y, w_out, (layer,), x_all, mod3, layer, 2)
        i = layer // 2
        if layer % 2 == 0:
            h2 = _norm_modulate(lay, x_all, norm2_g[layer], mod3, layer, 3, 4, BF16)
            act = _swiglu_up(lay, h2, ffn_w1, ffn_w3, (i,))
            x_all = _proj_resid(lay, act, ffn_w2, (i,), x_all, mod3, layer, 5)
        else:
            x_all = _moe_ffn(lay, x_all, norm2_g[layer], mod3, layer, router_w[i], router_b[i],
                             moe_w1, moe_w3, moe_w2, i)

    out = _final_norm(lay, x_all, final_norm_g)
    return out.reshape(batch, seq, d)
```

```python
import functools

import jax
import jax.numpy as jnp
from jax import lax
from jax.experimental import pallas as pl
from jax.experimental.pallas import tpu as pltpu

F32 = jnp.float32
BF16 = jnp.bfloat16

GRID_W = 64
CHUNK = 128
SGU_GROUPS = 8
SGU_WIDTH = 2048
RET_HEADS = 8
RET_QK_DIM = 256
RET_V_DIM = 512
RET_QK = RET_HEADS * RET_QK_DIM
RET_V = RET_HEADS * RET_V_DIM
ROPE_BASE = 10000.0
N_EXPERTS = 8
N_MOD = 6
NORM_EPS = 1e-6
MOD_ROWS = 8
LANES = 128
MOE_TILE = 512
SWIGLU_SUB = 256

VMEM_LIMIT_BYTES = 52 * 1024 * 1024

Q0 = 0
K0 = Q0 + RET_QK
V0 = K0 + RET_QK
G0 = V0 + RET_V
UA0 = G0 + RET_V
VA0 = UA0 + SGU_WIDTH
GA0 = VA0 + SGU_WIDTH
GR0 = GA0 + 2048
IN_COLS = GR0 + 2048


def _params(n_grid_dims):
    return pltpu.CompilerParams(
        dimension_semantics=("arbitrary",) * n_grid_dims,
        vmem_limit_bytes=VMEM_LIMIT_BYTES,
    )


def _gelu_tanh(x):
    c = 0.7978845608028654
    half_x = 0.5 * x
    return half_x * jnp.tanh(x * (c + (c * 0.044715) * (x * x))) + half_x


def _sigmoid(x):
    return 0.5 * jnp.tanh(0.5 * x) + 0.5


def _silu(x):
    return x * _sigmoid(x)


def _mod_kernel(c_ref, w_ref, b_ref, o_ref):
    sc = _silu(c_ref[...]).astype(BF16)
    o_ref[...] = jnp.dot(sc, w_ref[...].astype(BF16), preferred_element_type=F32) + b_ref[...]


def _modulation(cvec, w_mod, b_mod):
    depth, d, n = w_mod.shape
    tn = 1024
    return pl.pallas_call(
        _mod_kernel,
        grid=(depth, n // tn),
        in_specs=[
            pl.BlockSpec((MOD_ROWS, d), lambda l, j: (0, 0)),
            pl.BlockSpec((None, d, tn), lambda l, j: (l, 0, j)),
            pl.BlockSpec((None, 1, tn), lambda l, j: (l, 0, j)),
        ],
        out_specs=pl.BlockSpec((None, MOD_ROWS, tn), lambda l, j: (l, 0, j)),
        out_shape=jax.ShapeDtypeStruct((depth, MOD_ROWS, n), F32),
        compiler_params=_params(2),
        name="modulation",
    )(cvec, w_mod, b_mod.reshape(depth, 1, n))


def _norm_mod_kernel(x_ref, g_ref, shift_ref, scale_ref, o_ref):
    x = x_ref[...]
    y = x * lax.rsqrt(jnp.mean(x * x, axis=-1, keepdims=True) + NORM_EPS) * g_ref[...]
    o_ref[...] = (y * (1.0 + scale_ref[0]) + shift_ref[0]).astype(o_ref.dtype)


def _norm_kernel(x_ref, g_ref, o_ref):
    x = x_ref[...]
    y = x * lax.rsqrt(jnp.mean(x * x, axis=-1, keepdims=True) + NORM_EPS) * g_ref[...]
    o_ref[...] = y.astype(o_ref.dtype)


class _Layout:
    def __init__(self, batch, seq, ctx_len):
        self.batch, self.seq, self.ctx_len = batch, seq, ctx_len
        self.tm = batch * ctx_len
        assert seq % self.tm == 0 and ctx_len % CHUNK == 0 and seq % CHUNK == 0
        self.tiles_per_batch = seq // self.tm
        self.n_tiles = 1 + batch * self.tiles_per_batch
        self.rows = self.n_tiles * self.tm
        self.ctx_chunks = ctx_len // CHUNK
        self.lat_chunks = seq // CHUNK

    def mod_row(self, tile):
        return jnp.where(tile == 0, self.batch, (tile - 1) // self.tiles_per_batch)


def _mod_index(lay, layer, k, tile):
    return layer * MOD_ROWS * N_MOD + lay.mod_row(tile) * N_MOD + k


def _norm_modulate(lay, x_all, g, mod3, layer, k_shift, k_scale, out_dtype):
    d = x_all.shape[1]
    return pl.pallas_call(
        _norm_mod_kernel,
        grid=(lay.n_tiles,),
        in_specs=[
            pl.BlockSpec((lay.tm, d), lambda i: (i, 0)),
            pl.BlockSpec((1, d), lambda i: (0, 0)),
            pl.BlockSpec((1, 1, d), lambda i: (_mod_index(lay, layer, k_shift, i), 0, 0)),
            pl.BlockSpec((1, 1, d), lambda i: (_mod_index(lay, layer, k_scale, i), 0, 0)),
        ],
        out_specs=pl.BlockSpec((lay.tm, d), lambda i: (i, 0)),
        out_shape=jax.ShapeDtypeStruct((lay.rows, d), out_dtype),
        compiler_params=_params(1),
        name="norm_modulate",
    )(x_all, g.reshape(1, d), mod3, mod3)


def _final_norm(lay, x_all, g):
    d = x_all.shape[1]
    n_lat_tiles = lay.n_tiles - 1
    return pl.pallas_call(
        _norm_kernel,
        grid=(n_lat_tiles,),
        in_specs=[
            pl.BlockSpec((lay.tm, d), lambda i: (i + 1, 0)),
            pl.BlockSpec((1, d), lambda i: (0, 0)),
        ],
        out_specs=pl.BlockSpec((lay.tm, d), lambda i: (i, 0)),
        out_shape=jax.ShapeDtypeStruct((n_lat_tiles * lay.tm, d), F32),
        compiler_params=_params(1),
        name="final_norm",
    )(x_all, g.reshape(1, d))


IN_PROJ_TN = 2048
IN_PROJ_SUB = 512


def _in_proj_kernel(h_ref, w_ref, *rest, kind):
    o_ref = rest[-1]
    h = h_ref[...]
    if kind == "rope":
        cos_ref, sin_ref = rest[:2]
        scale = jnp.where(pl.program_id(0) == 0, 1.0, RET_QK_DIM ** -0.5).astype(F32)
    for c0 in range(0, IN_PROJ_TN, IN_PROJ_SUB):
        acc = jnp.dot(h, w_ref[:, c0:c0 + IN_PROJ_SUB].astype(BF16), preferred_element_type=F32)
        if kind == "rope":
            for s in range(0, IN_PROJ_SUB, LANES):
                t = (c0 + s) % RET_QK_DIM
                xs = acc[:, s:s + LANES]
                r = xs * cos_ref[:, t:t + LANES] + pltpu.roll(xs, LANES // 2, 1) * sin_ref[:, t:t + LANES]
                o_ref[:, c0 + s:c0 + s + LANES] = (r * scale).astype(o_ref.dtype)
        else:
            if kind == "silu":
                acc = _silu(acc)
            elif kind == "gelu":
                acc = _gelu_tanh(acc)
            elif kind == "sigmoid":
                acc = _sigmoid(acc)
            else:
                assert kind == "plain"
            o_ref[:, c0:c0 + IN_PROJ_SUB] = acc.astype(o_ref.dtype)


def _in_proj(lay, h, w_in, layer, col0, n_cols, kind, tables=()):
    d = h.shape[1]
    tn = IN_PROJ_TN
    table_spec = pl.BlockSpec((lay.tm, RET_QK_DIM), lambda j, i: (i, 0))
    return pl.pallas_call(
        functools.partial(_in_proj_kernel, kind=kind),
        grid=(n_cols // tn, lay.n_tiles),
        in_specs=[
            pl.BlockSpec((lay.tm, d), lambda j, i: (i, 0)),
            pl.BlockSpec((None, d, tn), lambda j, i: (layer, 0, col0 // tn + j)),
        ] + [table_spec] * len(tables),
        out_specs=pl.BlockSpec((lay.tm, tn), lambda j, i: (i, j)),
        out_shape=jax.ShapeDtypeStruct((lay.rows, n_cols), BF16),
        compiler_params=_params(2),
        name="in_proj_" + kind,
    )(h, w_in, *tables)


RET_STEP_CHUNKS = 2


def _retention_kernel(lg_ref, q_ref, k_ref, v_ref, *rest, reverse, final):
    if final:
        other_ref, gate_ref, o_ref, s_ref = rest
    else:
        o_ref, s_ref = rest
    c = CHUNK

    @pl.when(pl.program_id(1) == 0)
    def _():
        s_ref[...] = jnp.zeros_like(s_ref)

    qi = lax.broadcasted_iota(jnp.int32, (c, c), 0)
    kj = lax.broadcasted_iota(jnp.int32, (c, c), 1)
    diff = ((kj - qi) if reverse else (qi - kj)).astype(F32)
    pos = lax.broadcasted_iota(jnp.int32, (c, 1), 0).astype(F32)
    q_steps = (c - pos) if reverse else (pos + 1.0)
    k_steps = pos if reverse else (c - 1.0 - pos)

    subs = range(RET_STEP_CHUNKS)
    for h in range(RET_HEADS):
        log_g = lg_ref[h]
        intra = jnp.where(diff >= 0, jnp.exp(jnp.maximum(diff, 0.0) * log_g), 0.0)
        q_decay = jnp.exp(q_steps * log_g)
        k_decay = jnp.exp(k_steps * log_g)
        chunk_decay = jnp.exp(jnp.full((1, 1), float(c), F32) * log_g)
        for sub in (reversed(subs) if reverse else subs):
            rows = slice(sub * c, (sub + 1) * c)
            qh = q_ref[rows, h * RET_QK_DIM:(h + 1) * RET_QK_DIM]
            kh = k_ref[rows, h * RET_QK_DIM:(h + 1) * RET_QK_DIM]
            vh = v_ref[rows, h * RET_V_DIM:(h + 1) * RET_V_DIM]
            s = s_ref[h]

            scores = lax.dot_general(qh, kh, (((1,), (1,)), ((), ())), preferred_element_type=F32) * intra
            o = (jnp.dot(scores.astype(BF16), vh, preferred_element_type=F32)
                 + jnp.dot(qh, s.astype(BF16), preferred_element_type=F32) * q_decay)
            k_dec_t = (kh.astype(F32) * k_decay).T.astype(BF16)
            s_ref[h] = s * chunk_decay + jnp.dot(k_dec_t, vh, preferred_element_type=F32)

            sl = slice(h * RET_V_DIM, (h + 1) * RET_V_DIM)
            if final:
                o = o + other_ref[rows, sl]
                mu = jnp.mean(o, axis=-1, keepdims=True)
                oc = o - mu
                var = jnp.mean(oc * oc, axis=-1, keepdims=True)
                o = oc * lax.rsqrt(var + NORM_EPS) * gate_ref[rows, sl].astype(F32)
            o_ref[rows, sl] = o.astype(o_ref.dtype)


def _retention_pass(lay, qk, v, log_g, other=None, gate=None, *, reverse):
    final = other is not None
    step_rows = RET_STEP_CHUNKS * CHUNK
    assert lay.ctx_len % step_rows == 0 and lay.seq % step_rows == 0
    ctx_steps, lat_steps = lay.ctx_len // step_rows, lay.seq // step_rows

    def row_block(b, n):
        ctx_blk = b * ctx_steps + ((ctx_steps - 1 - n) if reverse else n)
        m = n - ctx_steps
        lat_blk = lay.batch * ctx_steps + b * lat_steps + ((lat_steps - 1 - m) if reverse else m)
        return jnp.where(n < ctx_steps, ctx_blk, lat_blk)

    in_specs = [
        pl.BlockSpec((step_rows, RET_QK), lambda b, n, lg: (row_block(b, n), 0)),
        pl.BlockSpec((step_rows, RET_QK), lambda b, n, lg: (row_block(b, n), 1)),
        pl.BlockSpec((step_rows, RET_V), lambda b, n, lg: (row_block(b, n), 0)),
    ]
    args = [qk, qk, v]
    if final:
        in_specs += [pl.BlockSpec((step_rows, RET_V), lambda b, n, lg: (row_block(b, n), 0))] * 2
        args += [other, gate]
    return pl.pallas_call(
        functools.partial(_retention_kernel, reverse=reverse, final=final),
        grid_spec=pltpu.PrefetchScalarGridSpec(
            num_scalar_prefetch=1,
            grid=(lay.batch, ctx_steps + lat_steps),
            in_specs=in_specs,
            out_specs=pl.BlockSpec((step_rows, RET_V), lambda b, n, lg: (row_block(b, n), 0)),
            scratch_shapes=[pltpu.VMEM((RET_HEADS, RET_QK_DIM, RET_V_DIM), F32)],
        ),
        out_shape=jax.ShapeDtypeStruct((lay.rows, RET_V), BF16 if final else F32),
        compiler_params=_params(2),
        name="retention_fwd_merge" if final else "retention_bwd",
    )(log_g, *args)


def _sgu_kernel(u_ref, v_ref, lng_ref, lnb_ref, ws_ref, bst_ref, o_ref):
    gw = SGU_WIDTH // SGU_GROUPS
    for c0 in range(0, o_ref.shape[0], CHUNK):
        rows = slice(c0, c0 + CHUNK)
        v = v_ref[rows, :].astype(F32)
        mu = jnp.mean(v, axis=-1, keepdims=True)
        vc = v - mu
        var = jnp.mean(vc * vc, axis=-1, keepdims=True)
        vn = (vc * lax.rsqrt(var + NORM_EPS) * lng_ref[...] + lnb_ref[...]).astype(BF16)
        for g in range(SGU_GROUPS):
            sl = slice(g * gw, (g + 1) * gw)
            s = jnp.dot(ws_ref[g].astype(BF16), vn[:, sl], preferred_element_type=F32) + bst_ref[:, g:g + 1]
            o_ref[rows, sl] = (u_ref[rows, sl].astype(F32) * s).astype(o_ref.dtype)


def _sgu(lay, uv, ln_g, ln_b, w_s, b_s_t):
    return pl.pallas_call(
        _sgu_kernel,
        grid=(lay.n_tiles,),
        in_specs=[
            pl.BlockSpec((lay.tm, SGU_WIDTH), lambda i: (i, 0)),
            pl.BlockSpec((lay.tm, SGU_WIDTH), lambda i: (i, 1)),
            pl.BlockSpec((1, SGU_WIDTH), lambda i: (0, 0)),
            pl.BlockSpec((1, SGU_WIDTH), lambda i: (0, 0)),
            pl.BlockSpec((SGU_GROUPS, CHUNK, CHUNK), lambda i: (0, 0, 0)),
            pl.BlockSpec((CHUNK, SGU_GROUPS), lambda i: (0, 0)),
        ],
        out_specs=pl.BlockSpec((lay.tm, SGU_WIDTH), lambda i: (i, 0)),
        out_shape=jax.ShapeDtypeStruct((lay.rows, SGU_WIDTH), BF16),
        compiler_params=_params(1),
        name="sgu",
    )(uv, uv, ln_g.reshape(1, -1), ln_b.reshape(1, -1), w_s, b_s_t)


def _merge_kernel(sgu_ref, ret_ref, wa_ref, wr_ref, ga_ref, gr_ref, o_ref):
    a = jnp.dot(sgu_ref[...], wa_ref[...].astype(BF16), preferred_element_type=F32)
    r = jnp.dot(ret_ref[...], wr_ref[...].astype(BF16), preferred_element_type=F32)
    o_ref[...] = (ga_ref[...].astype(F32) * a + gr_ref[...].astype(F32) * r).astype(o_ref.dtype)


def _merge(lay, sgu, ret, gates, w_proj_a, w_proj_r, layer):
    n = w_proj_a.shape[2]
    tn = 512
    return pl.pallas_call(
        _merge_kernel,
        grid=(n // tn, lay.n_tiles),
        in_specs=[
            pl.BlockSpec((lay.tm, SGU_WIDTH), lambda j, i: (i, 0)),
            pl.BlockSpec((lay.tm, RET_V), lambda j, i: (i, 0)),
            pl.BlockSpec((None, SGU_WIDTH, tn), lambda j, i: (layer, 0, j)),
            pl.BlockSpec((None, RET_V, tn), lambda j, i: (layer, 0, j)),
            pl.BlockSpec((lay.tm, tn), lambda j, i: (i, j)),
            pl.BlockSpec((lay.tm, tn), lambda j, i: (i, n // tn + j)),
        ],
        out_specs=pl.BlockSpec((lay.tm, tn), lambda j, i: (i, j)),
        out_shape=jax.ShapeDtypeStruct((lay.rows, n), BF16),
        compiler_params=_params(2),
        name="merge",
    )(sgu, ret, w_proj_a, w_proj_r, gates, gates)


def _proj_resid_kernel(a_ref, w_ref, resid_ref, gate_ref, o_ref):
    acc = jnp.dot(a_ref[...], w_ref[...].astype(BF16), preferred_element_type=F32)
    o_ref[...] = resid_ref[...] + gate_ref[0] * acc


def _proj_resid(lay, a, w, w_index, resid, mod3, layer, k_gate):
    kdim = a.shape[1]
    n = w.shape[-1]
    tn = 512
    w_spec = pl.BlockSpec((None,) * len(w_index) + (kdim, tn), lambda j, i: (*w_index, 0, j))
    return pl.pallas_call(
        _proj_resid_kernel,
        grid=(n // tn, lay.n_tiles),
        in_specs=[
            pl.BlockSpec((lay.tm, kdim), lambda j, i: (i, 0)),
            w_spec,
            pl.BlockSpec((lay.tm, tn), lambda j, i: (i, j)),
            pl.BlockSpec((1, 1, tn), lambda j, i: (_mod_index(lay, layer, k_gate, i), 0, j)),
        ],
        out_specs=pl.BlockSpec((lay.tm, tn), lambda j, i: (i, j)),
        out_shape=jax.ShapeDtypeStruct((lay.rows, n), F32),
        compiler_params=_params(2),
        name="proj_resid",
    )(a, w, resid, mod3)


def _swiglu_up_kernel(h_ref, w1_ref, w3_ref, o_ref):
    h = h_ref[...]
    for c0 in range(0, o_ref.shape[1], SWIGLU_SUB):
        cols = slice(c0, c0 + SWIGLU_SUB)
        a = jnp.dot(h, w1_ref[:, cols].astype(BF16), preferred_element_type=F32)
        b = jnp.dot(h, w3_ref[:, cols].astype(BF16), preferred_element_type=F32)
        o_ref[:, cols] = (_silu(a) * b).astype(o_ref.dtype)


def _swiglu_up(lay, h, w1, w3, w_index):
    d = h.shape[1]
    f = w1.shape[-1]
    tn = 512
    w_spec = pl.BlockSpec((None,) * len(w_index) + (d, tn), lambda j, i: (*w_index, 0, j))
    return pl.pallas_call(
        _swiglu_up_kernel,
        grid=(f // tn, lay.n_tiles),
        in_specs=[pl.BlockSpec((lay.tm, d), lambda j, i: (i, 0)), w_spec, w_spec],
        out_specs=pl.BlockSpec((lay.tm, tn), lambda j, i: (i, j)),
        out_shape=jax.ShapeDtypeStruct((lay.rows, f), BF16),
        compiler_params=_params(2),
        name="swiglu_up",
    )(h, w1, w3)


def _pack_bf16_pairs(y):
    half = y.shape[1] // 2
    lo = lax.bitcast_convert_type(y[:, :half].astype(BF16).astype(F32), jnp.uint32) >> 16
    hi = lax.bitcast_convert_type(y[:, half:].astype(BF16).astype(F32), jnp.uint32)
    return hi | lo


def _unpack_bf16_pairs(w):
    lo = lax.bitcast_convert_type(w << 16, F32).astype(BF16)
    hi = lax.bitcast_convert_type(w & jnp.uint32(0xFFFF0000), F32).astype(BF16)
    return jnp.concatenate([lo, hi], axis=1)


def _norm_router_kernel(x_ref, g_ref, shift_ref, scale_ref, rw_ref, rb_ref,
                        hp_ref, sel_ref, prob_ref, cnt_ref, run_ref):
    @pl.when(pl.program_id(0) == 0)
    def _():
        run_ref[...] = jnp.zeros_like(run_ref)

    x = x_ref[...]
    y = x * lax.rsqrt(jnp.mean(x * x, axis=-1, keepdims=True) + NORM_EPS) * g_ref[...]
    h = y * (1.0 + scale_ref[0]) + shift_ref[0]
    hp_ref[...] = _pack_bf16_pairs(h)

    logits = jnp.dot(h.astype(BF16), rw_ref[...], preferred_element_type=F32) + rb_ref[...]
    lane = lax.broadcasted_iota(jnp.int32, logits.shape, 1)
    neg = jnp.float32(-jnp.inf)
    lg = jnp.where(lane < N_EXPERTS, logits, neg)
    m1 = jnp.max(lg, axis=-1, keepdims=True)
    i1 = jnp.min(jnp.where(lg == m1, lane, LANES), axis=-1, keepdims=True)
    lg2 = jnp.where(lane == i1, neg, lg)
    m2 = jnp.max(lg2, axis=-1, keepdims=True)
    i2 = jnp.min(jnp.where(lg2 == m2, lane, LANES), axis=-1, keepdims=True)
    e2 = jnp.exp(m2 - m1)
    den = 1.0 + e2

    tm = x.shape[0]
    onehot = jnp.where(lane == i1, 1.0, 0.0) + jnp.where(lane == i2, 1.0, 0.0)
    earlier = jnp.where(lax.broadcasted_iota(jnp.int32, (tm, tm), 0) > lax.broadcasted_iota(jnp.int32, (tm, tm), 1),
                        1.0, 0.0).astype(BF16)
    before = jnp.dot(earlier, onehot.astype(BF16), preferred_element_type=F32) + run_ref[0:1, :]
    pos1 = jnp.sum(jnp.where(lane == i1, before, 0.0), axis=-1, keepdims=True).astype(jnp.int32)
    pos2 = jnp.sum(jnp.where(lane == i2, before, 0.0), axis=-1, keepdims=True).astype(jnp.int32)
    run_ref[...] = run_ref[...] + jnp.sum(onehot, axis=0, keepdims=True)
    cnt_ref[...] = run_ref[...]

    sel_ref[...] = jnp.where(lane == 0, i1, jnp.where(lane == 1, i2, jnp.where(lane == 2, pos1,
                                                                                jnp.where(lane == 3, pos2, 0))))
    prob_ref[...] = jnp.where(lane == 0, 1.0 / den, jnp.where(lane == 1, e2 / den, 0.0))


def _norm_router(lay, x_all, g, mod3, layer, k_shift, k_scale, router_w, router_b):
    d = x_all.shape[1]
    w = jnp.zeros((d, LANES), BF16).at[:, :N_EXPERTS].set(router_w.astype(BF16))
    b = jnp.zeros((1, LANES), F32).at[0, :N_EXPERTS].set(router_b.astype(F32))
    lane_block = pl.BlockSpec((lay.tm, LANES), lambda i: (i, 0))
    return pl.pallas_call(
        _norm_router_kernel,
        grid=(lay.n_tiles,),
        in_specs=[
            pl.BlockSpec((lay.tm, d), lambda i: (i, 0)),
            pl.BlockSpec((1, d), lambda i: (0, 0)),
            pl.BlockSpec((1, 1, d), lambda i: (_mod_index(lay, layer, k_shift, i), 0, 0)),
            pl.BlockSpec((1, 1, d), lambda i: (_mod_index(lay, layer, k_scale, i), 0, 0)),
            pl.BlockSpec((d, LANES), lambda i: (0, 0)),
            pl.BlockSpec((1, LANES), lambda i: (0, 0)),
        ],
        out_specs=[
            pl.BlockSpec((lay.tm, d // 2), lambda i: (i, 0)),
            lane_block,
            lane_block,
            pl.BlockSpec((8, LANES), lambda i: (0, 0)),
        ],
        out_shape=[
            jax.ShapeDtypeStruct((lay.rows, d // 2), jnp.uint32),
            jax.ShapeDtypeStruct((lay.rows, LANES), jnp.int32),
            jax.ShapeDtypeStruct((lay.rows, LANES), F32),
            jax.ShapeDtypeStruct((8, LANES), F32),
        ],
        scratch_shapes=[pltpu.VMEM((8, LANES), F32)],
        compiler_params=_params(1),
        name="norm_router",
    )(x_all, g.reshape(1, d), mod3, mod3, w, b)


def _routing_plan(sel, cnt, n_sorted_rows):
    counts = cnt[0, :N_EXPERTS].astype(jnp.int32)
    padded = (counts + MOE_TILE - 1) // MOE_TILE * MOE_TILE
    ends = jnp.cumsum(padded)
    starts = ends - padded
    dest1 = starts[sel[:, 0]] + sel[:, 2]
    dest2 = starts[sel[:, 1]] + sel[:, 3]
    tile_start = jnp.arange(n_sorted_rows // MOE_TILE, dtype=jnp.int32) * MOE_TILE
    tile_expert = jnp.minimum(jnp.sum(tile_start[:, None] >= ends[None, :], axis=1), N_EXPERTS - 1)
    n_used_tiles = (ends[-1:] // MOE_TILE).astype(jnp.int32)
    present = counts > 0
    run_expert = jnp.argsort(jnp.logical_not(present), stable=True).astype(jnp.int32)
    run_of_expert = jnp.cumsum(present.astype(jnp.int32)) - 1
    tile_run = jnp.maximum(run_of_expert[tile_expert], 0).astype(jnp.int32)
    run_first_tile = (starts[run_expert] // MOE_TILE).astype(jnp.int32)
    n_runs = jnp.sum(present.astype(jnp.int32)).reshape(1)
    return dest1, dest2, (tile_run, run_first_tile, run_expert, n_runs, n_used_tiles)


def _dispatch_kernel(dest1_ref, dest2_ref, n_used_ref, h_ref, o_ref, src_ref, sem):
    i = pl.program_id(0)
    n_tok = h_ref.shape[0]
    tile = o_ref.shape[0]

    @pl.when(i == 0)
    def _():
        def clear(r, carry):
            src_ref[r] = 0
            return carry

        def invert(t, carry):
            src_ref[dest1_ref[t]] = t
            src_ref[dest2_ref[t]] = t
            return carry

        lax.fori_loop(0, src_ref.shape[0], clear, 0, unroll=8)
        lax.fori_loop(0, n_tok, invert, 0, unroll=4)

    def row_copy(r, t):
        return pltpu.make_async_copy(h_ref.at[pl.ds(t, 1)], o_ref.at[pl.ds(r, 1)], sem)

    @pl.when(i < n_used_ref[0])
    def _():
        base = i * tile

        def start(r, carry):
            row_copy(r, src_ref[base + r]).start()
            return carry

        def wait(r, carry):
            row_copy(0, 0).wait()
            return carry

        lax.fori_loop(0, tile, start, 0, unroll=4)
        lax.fori_loop(0, tile, wait, 0, unroll=4)

    @pl.when(i >= n_used_ref[0])
    def _():
        o_ref[...] = jnp.zeros_like(o_ref)


def _dispatch(hp, dest1, dest2, n_used, n_sorted_rows):
    width = hp.shape[1]
    return pl.pallas_call(
        _dispatch_kernel,
        grid_spec=pltpu.PrefetchScalarGridSpec(
            num_scalar_prefetch=3,
            grid=(n_sorted_rows // MOE_TILE,),
            in_specs=[pl.BlockSpec(memory_space=pl.ANY)],
            out_specs=pl.BlockSpec((MOE_TILE, width), lambda i, d1, d2, nu: (i, 0)),
            scratch_shapes=[pltpu.SMEM((n_sorted_rows,), jnp.int32), pltpu.SemaphoreType.DMA(())],
        ),
        out_shape=jax.ShapeDtypeStruct((n_sorted_rows, width), hp.dtype),
        compiler_params=_params(1),
        name="moe_dispatch",
    )(dest1, dest2, n_used, hp)


def _grouped_kernel(tile_run_ref, run_first_ref, run_expert_ref, n_runs_ref, n_used_ref, a_ref, *rest,
                    moe_layer, tn, swiglu):
    n_mats = 2 if swiglu else 1
    w_hbm = rest[:n_mats]
    o_ref, wbuf, sems = rest[n_mats:]
    j, i = pl.program_id(0), pl.program_id(1)
    n_runs = n_runs_ref[0]
    run = tile_run_ref[i]
    seq = j * n_runs + run
    slot = lax.rem(seq, 2)

    def fetches(jj, rr, slot_):
        expert = run_expert_ref[rr]
        col = pl.multiple_of(jj * tn, tn)
        return [pltpu.make_async_copy(w.at[moe_layer, expert, :, pl.ds(col, tn)], wbuf.at[slot_, m], sems.at[slot_, m])
                for m, w in enumerate(w_hbm)]

    live = i < n_used_ref[0]

    @pl.when(live & (i == run_first_ref[run]))
    def _():
        @pl.when(seq == 0)
        def _():
            for c in fetches(j, run, slot):
                c.start()

        for c in fetches(j, run, slot):
            c.wait()
        wraps = run + 1 == n_runs
        next_j = jnp.where(wraps, j + 1, j)
        next_run = jnp.where(wraps, 0, run + 1)

        @pl.when(next_j < pl.num_programs(0))
        def _():
            for c in fetches(next_j, next_run, 1 - slot):
                c.start()

    @pl.when(live)
    def _():
        if swiglu:
            h = _unpack_bf16_pairs(a_ref[...])
            for c0 in range(0, tn, SWIGLU_SUB):
                cols = slice(c0, c0 + SWIGLU_SUB)
                a = jnp.dot(h, wbuf[slot, 0, :, cols].astype(BF16), preferred_element_type=F32)
                b = jnp.dot(h, wbuf[slot, 1, :, cols].astype(BF16), preferred_element_type=F32)
                o_ref[:, cols] = (_silu(a) * b).astype(o_ref.dtype)
        else:
            o_ref[...] = jnp.dot(a_ref[...], wbuf[slot, 0].astype(BF16), preferred_element_type=F32)

    @pl.when(jnp.logical_not(live))
    def _():
        o_ref[...] = jnp.zeros_like(o_ref)


def _grouped_matmul(a, weights, moe_layer, plan, out_dtype, *, swiglu, name):
    n_rows, a_width = a.shape
    kdim, n = weights[0].shape[-2:]
    tn = 512
    n_prefetch = 5
    idx = lambda j, i, *_: (i, 0)
    return pl.pallas_call(
        functools.partial(_grouped_kernel, moe_layer=moe_layer, tn=tn, swiglu=swiglu),
        grid_spec=pltpu.PrefetchScalarGridSpec(
            num_scalar_prefetch=n_prefetch,
            grid=(n // tn, n_rows // MOE_TILE),
            in_specs=[pl.BlockSpec((MOE_TILE, a_width), idx)] + [pl.BlockSpec(memory_space=pl.ANY)] * len(weights),
            out_specs=pl.BlockSpec((MOE_TILE, tn), lambda j, i, *_: (i, j)),
            scratch_shapes=[
                pltpu.VMEM((2, len(weights), kdim, tn), F32),
                pltpu.SemaphoreType.DMA((2, len(weights))),
            ],
        ),
        out_shape=jax.ShapeDtypeStruct((n_rows, n), out_dtype),
        compiler_params=_params(2),
        name=name,
    )(*plan, a, *weights)


def _combine_kernel(dest1_ref, dest2_ref, x_ref, prob_ref, gate_ref, ys_ref, *rest, mode, first_tile):
    if mode == "final":
        g_ref, o_ref, buf1, buf2, sem1, sem2 = rest
    else:
        g_ref, shift_ref, scale_ref, o_ref, h_ref, buf1, buf2, sem1, sem2 = rest
    tm = x_ref.shape[0]
    base = (pl.program_id(0) + first_tile) * tm

    def copy1(t, d):
        return pltpu.make_async_copy(ys_ref.at[pl.ds(d, 1)], buf1.at[pl.ds(t, 1)], sem1)

    def copy2(t, d):
        return pltpu.make_async_copy(ys_ref.at[pl.ds(d, 1)], buf2.at[pl.ds(t, 1)], sem2)

    def start(t, carry):
        copy1(t, dest1_ref[base + t]).start()
        copy2(t, dest2_ref[base + t]).start()
        return carry

    def wait(t, carry):
        copy1(0, 0).wait()
        copy2(0, 0).wait()
        return carry

    lax.fori_loop(0, tm, start, 0)
    lax.fori_loop(0, tm, wait, 0)
    p = prob_ref[...]
    f = p[:, 0:1] * buf1[...] + p[:, 1:2] * buf2[...]
    x = x_ref[...] + gate_ref[0] * f
    y = x * lax.rsqrt(jnp.mean(x * x, axis=-1, keepdims=True) + NORM_EPS) * g_ref[...]
    if mode == "final":
        o_ref[...] = y
    else:
        o_ref[...] = x
        h_ref[...] = (y * (1.0 + scale_ref[0]) + shift_ref[0]).astype(h_ref.dtype)


def _combine(lay, x_all, prob, ys, dest1, dest2, mod3, layer, k_gate, norm_g, *, final):
    d = x_all.shape[1]
    first_tile = 1 if final else 0
    tile = lambda i, d1, d2: (i + first_tile, 0)
    mod = lambda lyr, k: (lambda i, d1, d2: (_mod_index(lay, lyr, k, i + first_tile), 0, 0))
    row_spec = pl.BlockSpec((lay.tm, d), lambda i, d1, d2: (i, 0))
    in_specs = [
        pl.BlockSpec((lay.tm, d), tile),
        pl.BlockSpec((lay.tm, LANES), tile),
        pl.BlockSpec((1, 1, d), mod(layer, k_gate)),
        pl.BlockSpec(memory_space=pl.ANY),
        pl.BlockSpec((1, d), lambda i, d1, d2: (0, 0)),
    ]
    args = [x_all, prob, mod3, ys, norm_g.reshape(1, d)]
    n_tiles = lay.n_tiles - first_tile
    if final:
        out_specs = row_spec
        out_shape = jax.ShapeDtypeStruct((n_tiles * lay.tm, d), F32)
    else:
        in_specs += [pl.BlockSpec((1, 1, d), mod(layer + 1, 0)), pl.BlockSpec((1, 1, d), mod(layer + 1, 1))]
        args += [mod3, mod3]
        out_specs = [row_spec, row_spec]
        out_shape = [jax.ShapeDtypeStruct((lay.rows, d), F32), jax.ShapeDtypeStruct((lay.rows, d), BF16)]
    return pl.pallas_call(
        functools.partial(_combine_kernel, mode="final" if final else "next", first_tile=first_tile),
        grid_spec=pltpu.PrefetchScalarGridSpec(
            num_scalar_prefetch=2,
            grid=(n_tiles,),
            in_specs=in_specs,
            out_specs=out_specs,
            scratch_shapes=[
                pltpu.VMEM((lay.tm, d), F32),
                pltpu.VMEM((lay.tm, d), F32),
                pltpu.SemaphoreType.DMA(()),
                pltpu.SemaphoreType.DMA(()),
            ],
        ),
        out_shape=out_shape,
        compiler_params=_params(1),
        name="moe_combine",
    )(dest1, dest2, *args)


def _moe_ffn(lay, x_all, norm_g, mod3, layer, router_w, router_b, w1, w3, w2, moe_layer, next_norm_g, *, final):
    hp, sel, prob, cnt = _norm_router(lay, x_all, norm_g, mod3, layer, 3, 4, router_w, router_b)
    n_sorted_rows = 2 * lay.rows + N_EXPERTS * MOE_TILE
    dest1, dest2, plan = _routing_plan(sel, cnt, n_sorted_rows)
    xs = _dispatch(hp, dest1, dest2, plan[-1], n_sorted_rows)
    act = _grouped_matmul(xs, (w1, w3), moe_layer, plan, BF16, swiglu=True, name="moe_up")
    ys = _grouped_matmul(act, (w2,), moe_layer, plan, F32, swiglu=False, name="moe_down")
    return _combine(lay, x_all, prob, ys, dest1, dest2, mod3, layer, 5, next_norm_g, final=final)


def _rope_tables(lay):
    t = jnp.arange(lay.seq)
    quarter = RET_QK_DIM // 4
    freqs = ROPE_BASE ** (-jnp.arange(quarter, dtype=F32) / quarter)

    def tables(pos):
        ang = pos.astype(F32)[:, None] * freqs[None, :]
        cos, sin = jnp.cos(ang), jnp.sin(ang)
        return jnp.concatenate([cos, cos], axis=-1), jnp.concatenate([-sin, sin], axis=-1)

    cos_r, sin_r = tables(t // GRID_W)
    cos_c, sin_c = tables(t % GRID_W)
    cos_l = jnp.concatenate([cos_r, cos_c], axis=-1)
    sin_l = jnp.concatenate([sin_r, sin_c], axis=-1)
    n_ctx = lay.batch * lay.ctx_len
    cos_t = jnp.concatenate([jnp.ones((n_ctx, RET_QK_DIM), F32)] + [cos_l] * lay.batch, axis=0)
    sin_t = jnp.concatenate([jnp.zeros((n_ctx, RET_QK_DIM), F32)] + [sin_l] * lay.batch, axis=0)
    return cos_t, sin_t


def kernel(x, c, ctx, c_ctx, w_mod, b_mod, norm1_g, norm2_g, w_in, sgu_ln_g, sgu_ln_b, sgu_w, sgu_b,
           ret_log_decay, w_proj_a, w_proj_r, w_out, ffn_w1, ffn_w3, ffn_w2, router_w, router_b,
           moe_w1, moe_w3, moe_w2, final_norm_g):
    batch, seq, d = x.shape
    ctx_len = ctx.shape[1]
    depth = w_mod.shape[0]
    lay = _Layout(batch, seq, ctx_len)

    cvec = jnp.zeros((MOD_ROWS, d), F32).at[:batch].set(c).at[batch].set(c_ctx)
    mod = _modulation(cvec, w_mod, b_mod)
    mod3 = mod.reshape(depth * MOD_ROWS * N_MOD, 1, d)

    cos_t, sin_t = _rope_tables(lay)
    x_all = jnp.concatenate([ctx.reshape(batch * ctx_len, d), x.reshape(batch * seq, d)], axis=0)

    log_decay = ret_log_decay.astype(F32)

    h = None
    out = None
    for layer in range(depth):
        if h is None:
            h = _norm_modulate(lay, x_all, norm1_g[layer], mod3, layer, 0, 1, BF16)
        qk = _in_proj(lay, h, w_in, layer, Q0, V0 - Q0, "rope", (cos_t, sin_t))
        v = _in_proj(lay, h, w_in, layer, V0, G0 - V0, "plain")
        g = _in_proj(lay, h, w_in, layer, G0, UA0 - G0, "silu")
        uv = _in_proj(lay, h, w_in, layer, UA0, GA0 - UA0, "gelu")
        gates = _in_proj(lay, h, w_in, layer, GA0, IN_COLS - GA0, "sigmoid")
        o_bwd = _retention_pass(lay, qk, v, log_decay[layer, 1], reverse=True)
        ret = _retention_pass(lay, qk, v, log_decay[layer, 0], o_bwd, g, reverse=False)
        sgu = _sgu(lay, uv, sgu_ln_g[layer], sgu_ln_b[layer], sgu_w[layer], sgu_b[layer].T)
        y = _merge(lay, sgu, ret, gates, w_proj_a, w_proj_r, layer)
        x_all = _proj_resid(lay, y, w_out, (layer,), x_all, mod3, layer, 2)
        i = layer // 2
        last = layer == depth - 1
        h = None
        if layer % 2 == 0:
            h2 = _norm_modulate(lay, x_all, norm2_g[layer], mod3, layer, 3, 4, BF16)
            act = _swiglu_up(lay, h2, ffn_w1, ffn_w3, (i,))
            x_all = _proj_resid(lay, act, ffn_w2, (i,), x_all, mod3, layer, 5)
        else:
            res = _moe_ffn(lay, x_all, norm2_g[layer], mod3, layer, router_w[i], router_b[i], moe_w1, moe_w3, moe_w2,
                           i, final_norm_g if last else norm1_g[layer + 1], final=last)
            if last:
                out = res
            else:
                x_all, h = res

    if out is None:
        out = _final_norm(lay, x_all, final_norm_g)
    return out.reshape(batch, seq, d)
```

```python
import functools

import jax
import jax.numpy as jnp
import numpy as np
from jax import lax
from jax.experimental import pallas as pl
from jax.experimental.pallas import tpu as pltpu

F32 = jnp.float32
BF16 = jnp.bfloat16

GRID_W = 64
CHUNK = 128
SGU_GROUPS = 8
SGU_WIDTH = 2048
RET_HEADS = 8
RET_QK_DIM = 256
RET_V_DIM = 512
RET_QK = RET_HEADS * RET_QK_DIM
RET_V = RET_HEADS * RET_V_DIM
ROPE_BASE = 10000.0
N_EXPERTS = 8
N_MOD = 6
NORM_EPS = 1e-6
MOD_ROWS = 8
LANES = 128
MOE_TILE = 512
SWIGLU_SUB = 256

VMEM_LIMIT_BYTES = 52 * 1024 * 1024

Q0 = 0
K0 = Q0 + RET_QK
V0 = K0 + RET_QK
G0 = V0 + RET_V
UA0 = G0 + RET_V
VA0 = UA0 + SGU_WIDTH
GA0 = VA0 + SGU_WIDTH
GR0 = GA0 + 2048
IN_COLS = GR0 + 2048


def _params(n_grid_dims):
    return pltpu.CompilerParams(
        dimension_semantics=("arbitrary",) * n_grid_dims,
        vmem_limit_bytes=VMEM_LIMIT_BYTES,
    )


def _gelu_tanh(x):
    c = 0.7978845608028654
    half_x = 0.5 * x
    return half_x * jnp.tanh(x * (c + (c * 0.044715) * (x * x))) + half_x


def _sigmoid(x):
    return 0.5 * jnp.tanh(0.5 * x) + 0.5


def _silu(x):
    return x * _sigmoid(x)


def _mod_kernel(c_ref, w_ref, b_ref, o_ref):
    sc = _silu(c_ref[...]).astype(BF16)
    o_ref[...] = jnp.dot(sc, w_ref[...].astype(BF16), preferred_element_type=F32) + b_ref[...]


def _modulation(cvec, w_mod, b_mod):
    depth, d, n = w_mod.shape
    tn = 1024
    return pl.pallas_call(
        _mod_kernel,
        grid=(depth, n // tn),
        in_specs=[
            pl.BlockSpec((MOD_ROWS, d), lambda l, j: (0, 0)),
            pl.BlockSpec((None, d, tn), lambda l, j: (l, 0, j)),
            pl.BlockSpec((None, 1, tn), lambda l, j: (l, 0, j)),
        ],
        out_specs=pl.BlockSpec((None, MOD_ROWS, tn), lambda l, j: (l, 0, j)),
        out_shape=jax.ShapeDtypeStruct((depth, MOD_ROWS, n), F32),
        compiler_params=_params(2),
        name="modulation",
    )(cvec, w_mod, b_mod.reshape(depth, 1, n))


def _norm_mod_kernel(x_ref, g_ref, shift_ref, scale_ref, o_ref):
    x = x_ref[...]
    y = x * lax.rsqrt(jnp.mean(x * x, axis=-1, keepdims=True) + NORM_EPS) * g_ref[...]
    o_ref[...] = (y * (1.0 + scale_ref[0]) + shift_ref[0]).astype(o_ref.dtype)


def _norm_kernel(x_ref, g_ref, o_ref):
    x = x_ref[...]
    y = x * lax.rsqrt(jnp.mean(x * x, axis=-1, keepdims=True) + NORM_EPS) * g_ref[...]
    o_ref[...] = y.astype(o_ref.dtype)


class _Layout:
    def __init__(self, batch, seq, ctx_len):
        self.batch, self.seq, self.ctx_len = batch, seq, ctx_len
        self.tm = batch * ctx_len
        assert seq % self.tm == 0 and ctx_len % CHUNK == 0 and seq % CHUNK == 0
        self.tiles_per_batch = seq // self.tm
        self.n_tiles = 1 + batch * self.tiles_per_batch
        self.rows = self.n_tiles * self.tm
        self.ctx_chunks = ctx_len // CHUNK
        self.lat_chunks = seq // CHUNK

    def mod_row(self, tile):
        return jnp.where(tile == 0, self.batch, (tile - 1) // self.tiles_per_batch)


def _mod_index(lay, layer, k, tile):
    return layer * MOD_ROWS * N_MOD + lay.mod_row(tile) * N_MOD + k


def _norm_modulate(lay, x_all, g, mod3, layer, k_shift, k_scale, out_dtype):
    d = x_all.shape[1]
    return pl.pallas_call(
        _norm_mod_kernel,
        grid=(lay.n_tiles,),
        in_specs=[
            pl.BlockSpec((lay.tm, d), lambda i: (i, 0)),
            pl.BlockSpec((1, d), lambda i: (0, 0)),
            pl.BlockSpec((1, 1, d), lambda i: (_mod_index(lay, layer, k_shift, i), 0, 0)),
            pl.BlockSpec((1, 1, d), lambda i: (_mod_index(lay, layer, k_scale, i), 0, 0)),
        ],
        out_specs=pl.BlockSpec((lay.tm, d), lambda i: (i, 0)),
        out_shape=jax.ShapeDtypeStruct((lay.rows, d), out_dtype),
        compiler_params=_params(1),
        name="norm_modulate",
    )(x_all, g.reshape(1, d), mod3, mod3)


def _final_norm(lay, x_all, g):
    d = x_all.shape[1]
    n_lat_tiles = lay.n_tiles - 1
    return pl.pallas_call(
        _norm_kernel,
        grid=(n_lat_tiles,),
        in_specs=[
            pl.BlockSpec((lay.tm, d), lambda i: (i + 1, 0)),
            pl.BlockSpec((1, d), lambda i: (0, 0)),
        ],
        out_specs=pl.BlockSpec((lay.tm, d), lambda i: (i, 0)),
        out_shape=jax.ShapeDtypeStruct((n_lat_tiles * lay.tm, d), F32),
        compiler_params=_params(1),
        name="final_norm",
    )(x_all, g.reshape(1, d))


IN_PROJ_TN = 2048
IN_PROJ_SUB = 512


def _in_proj_kernel(h_ref, w_ref, *rest, kind):
    o_ref = rest[-1]
    h = h_ref[...]
    if kind == "rope":
        cos_ref, sin_ref = rest[:2]
        scale = jnp.where(pl.program_id(0) == 0, 1.0, RET_QK_DIM ** -0.5).astype(F32)
    for c0 in range(0, IN_PROJ_TN, IN_PROJ_SUB):
        acc = jnp.dot(h, w_ref[:, c0:c0 + IN_PROJ_SUB].astype(BF16), preferred_element_type=F32)
        if kind == "rope":
            for s in range(0, IN_PROJ_SUB, LANES):
                t = (c0 + s) % RET_QK_DIM
                xs = acc[:, s:s + LANES]
                r = xs * cos_ref[:, t:t + LANES] + pltpu.roll(xs, LANES // 2, 1) * sin_ref[:, t:t + LANES]
                o_ref[:, c0 + s:c0 + s + LANES] = (r * scale).astype(o_ref.dtype)
        else:
            if kind == "silu":
                acc = _silu(acc)
            elif kind == "gelu":
                acc = _gelu_tanh(acc)
            elif kind == "sigmoid":
                acc = _sigmoid(acc)
            else:
                assert kind == "plain"
            o_ref[:, c0:c0 + IN_PROJ_SUB] = acc.astype(o_ref.dtype)


def _in_proj(lay, h, w_in, layer, col0, n_cols, kind, tables=()):
    d = h.shape[1]
    tn = IN_PROJ_TN
    table_spec = pl.BlockSpec((lay.tm, RET_QK_DIM),
                              lambda j, i: (jnp.where(i == 0, 0, 1 + lax.rem(i - 1, lay.tiles_per_batch)), 0))
    return pl.pallas_call(
        functools.partial(_in_proj_kernel, kind=kind),
        grid=(n_cols // tn, lay.n_tiles),
        in_specs=[
            pl.BlockSpec((lay.tm, d), lambda j, i: (i, 0)),
            pl.BlockSpec((None, d, tn), lambda j, i: (layer, 0, col0 // tn + j)),
        ] + [table_spec] * len(tables),
        out_specs=pl.BlockSpec((lay.tm, tn), lambda j, i: (i, j)),
        out_shape=jax.ShapeDtypeStruct((lay.rows, n_cols), BF16),
        compiler_params=_params(2),
        name="in_proj_" + kind,
    )(h, w_in, *tables)


RET_STEP_CHUNKS = 2


def _retention_kernel(lg_ref, q_ref, k_ref, v_ref, *rest, reverse, final):
    if final:
        other_ref, gate_ref, o_ref, s_ref = rest
    else:
        o_ref, s_ref = rest
    c = CHUNK

    @pl.when(pl.program_id(1) == 0)
    def _():
        s_ref[...] = jnp.zeros_like(s_ref)

    qi = lax.broadcasted_iota(jnp.int32, (c, c), 0)
    kj = lax.broadcasted_iota(jnp.int32, (c, c), 1)
    diff = ((kj - qi) if reverse else (qi - kj)).astype(F32)
    pos = lax.broadcasted_iota(jnp.int32, (c, 1), 0).astype(F32)
    q_steps = (c - pos) if reverse else (pos + 1.0)
    k_steps = pos if reverse else (c - 1.0 - pos)

    subs = range(RET_STEP_CHUNKS)
    for h in range(RET_HEADS):
        log_g = lg_ref[h]
        intra = jnp.where(diff >= 0, jnp.exp(jnp.maximum(diff, 0.0) * log_g), 0.0)
        q_decay = jnp.exp(q_steps * log_g)
        k_decay = jnp.exp(k_steps * log_g)
        chunk_decay = jnp.exp(jnp.full((1, 1), float(c), F32) * log_g)
        for sub in (reversed(subs) if reverse else subs):
            rows = slice(sub * c, (sub + 1) * c)
            qh = q_ref[rows, h * RET_QK_DIM:(h + 1) * RET_QK_DIM]
            kh = k_ref[rows, h * RET_QK_DIM:(h + 1) * RET_QK_DIM]
            vh = v_ref[rows, h * RET_V_DIM:(h + 1) * RET_V_DIM]
            s = s_ref[h]

            scores = lax.dot_general(qh, kh, (((1,), (1,)), ((), ())), preferred_element_type=F32) * intra
            o = (jnp.dot(scores.astype(BF16), vh, preferred_element_type=F32)
                 + jnp.dot(qh, s.astype(BF16), preferred_element_type=F32) * q_decay)
            k_dec_t = (kh.astype(F32) * k_decay).T.astype(BF16)
            s_ref[h] = s * chunk_decay + jnp.dot(k_dec_t, vh, preferred_element_type=F32)

            sl = slice(h * RET_V_DIM, (h + 1) * RET_V_DIM)
            if final:
                o = o + other_ref[rows, sl]
                mu = jnp.mean(o, axis=-1, keepdims=True)
                oc = o - mu
                var = jnp.mean(oc * oc, axis=-1, keepdims=True)
                o = oc * lax.rsqrt(var + NORM_EPS) * gate_ref[rows, sl].astype(F32)
            o_ref[rows, sl] = o.astype(o_ref.dtype)


def _retention_pass(lay, qk, v, log_g, other=None, gate=None, *, reverse):
    final = other is not None
    step_rows = RET_STEP_CHUNKS * CHUNK
    assert lay.ctx_len % step_rows == 0 and lay.seq % step_rows == 0
    ctx_steps, lat_steps = lay.ctx_len // step_rows, lay.seq // step_rows

    def row_block(b, n):
        ctx_blk = b * ctx_steps + ((ctx_steps - 1 - n) if reverse else n)
        m = n - ctx_steps
        lat_blk = lay.batch * ctx_steps + b * lat_steps + ((lat_steps - 1 - m) if reverse else m)
        return jnp.where(n < ctx_steps, ctx_blk, lat_blk)

    in_specs = [
        pl.BlockSpec((step_rows, RET_QK), lambda b, n, lg: (row_block(b, n), 0)),
        pl.BlockSpec((step_rows, RET_QK), lambda b, n, lg: (row_block(b, n), 1)),
        pl.BlockSpec((step_rows, RET_V), lambda b, n, lg: (row_block(b, n), 0)),
    ]
    args = [qk, qk, v]
    if final:
        in_specs += [pl.BlockSpec((step_rows, RET_V), lambda b, n, lg: (row_block(b, n), 0))] * 2
        args += [other, gate]
    return pl.pallas_call(
        functools.partial(_retention_kernel, reverse=reverse, final=final),
        grid_spec=pltpu.PrefetchScalarGridSpec(
            num_scalar_prefetch=1,
            grid=(lay.batch, ctx_steps + lat_steps),
            in_specs=in_specs,
            out_specs=pl.BlockSpec((step_rows, RET_V), lambda b, n, lg: (row_block(b, n), 0)),
            scratch_shapes=[pltpu.VMEM((RET_HEADS, RET_QK_DIM, RET_V_DIM), F32)],
        ),
        out_shape=jax.ShapeDtypeStruct((lay.rows, RET_V), BF16 if final else F32),
        compiler_params=_params(2),
        name="retention_fwd_merge" if final else "retention_bwd",
    )(log_g, *args)


def _sgu_kernel(u_ref, v_ref, lng_ref, lnb_ref, ws_ref, bst_ref, o_ref):
    gw = SGU_WIDTH // SGU_GROUPS
    for c0 in range(0, o_ref.shape[0], CHUNK):
        rows = slice(c0, c0 + CHUNK)
        v = v_ref[rows, :].astype(F32)
        mu = jnp.mean(v, axis=-1, keepdims=True)
        vc = v - mu
        var = jnp.mean(vc * vc, axis=-1, keepdims=True)
        vn = (vc * lax.rsqrt(var + NORM_EPS) * lng_ref[...] + lnb_ref[...]).astype(BF16)
        for g in range(SGU_GROUPS):
            sl = slice(g * gw, (g + 1) * gw)
            s = jnp.dot(ws_ref[g].astype(BF16), vn[:, sl], preferred_element_type=F32) + bst_ref[:, g:g + 1]
            o_ref[rows, sl] = (u_ref[rows, sl].astype(F32) * s).astype(o_ref.dtype)


def _sgu(lay, uv, ln_g, ln_b, w_s, b_s_t):
    return pl.pallas_call(
        _sgu_kernel,
        grid=(lay.n_tiles,),
        in_specs=[
            pl.BlockSpec((lay.tm, SGU_WIDTH), lambda i: (i, 0)),
            pl.BlockSpec((lay.tm, SGU_WIDTH), lambda i: (i, 1)),
            pl.BlockSpec((1, SGU_WIDTH), lambda i: (0, 0)),
            pl.BlockSpec((1, SGU_WIDTH), lambda i: (0, 0)),
            pl.BlockSpec((SGU_GROUPS, CHUNK, CHUNK), lambda i: (0, 0, 0)),
            pl.BlockSpec((CHUNK, SGU_GROUPS), lambda i: (0, 0)),
        ],
        out_specs=pl.BlockSpec((lay.tm, SGU_WIDTH), lambda i: (i, 0)),
        out_shape=jax.ShapeDtypeStruct((lay.rows, SGU_WIDTH), BF16),
        compiler_params=_params(1),
        name="sgu",
    )(uv, uv, ln_g.reshape(1, -1), ln_b.reshape(1, -1), w_s, b_s_t)


def _merge_kernel(sgu_ref, ret_ref, wa_ref, wr_ref, ga_ref, gr_ref, o_ref):
    a = jnp.dot(sgu_ref[...], wa_ref[...].astype(BF16), preferred_element_type=F32)
    r = jnp.dot(ret_ref[...], wr_ref[...].astype(BF16), preferred_element_type=F32)
    o_ref[...] = (ga_ref[...].astype(F32) * a + gr_ref[...].astype(F32) * r).astype(o_ref.dtype)


def _merge(lay, sgu, ret, gates, w_proj_a, w_proj_r, layer):
    n = w_proj_a.shape[2]
    tn = 512
    return pl.pallas_call(
        _merge_kernel,
        grid=(n // tn, lay.n_tiles),
        in_specs=[
            pl.BlockSpec((lay.tm, SGU_WIDTH), lambda j, i: (i, 0)),
            pl.BlockSpec((lay.tm, RET_V), lambda j, i: (i, 0)),
            pl.BlockSpec((None, SGU_WIDTH, tn), lambda j, i: (layer, 0, j)),
            pl.BlockSpec((None, RET_V, tn), lambda j, i: (layer, 0, j)),
            pl.BlockSpec((lay.tm, tn), lambda j, i: (i, j)),
            pl.BlockSpec((lay.tm, tn), lambda j, i: (i, n // tn + j)),
        ],
        out_specs=pl.BlockSpec((lay.tm, tn), lambda j, i: (i, j)),
        out_shape=jax.ShapeDtypeStruct((lay.rows, n), BF16),
        compiler_params=_params(2),
        name="merge",
    )(sgu, ret, w_proj_a, w_proj_r, gates, gates)


def _proj_resid_kernel(a_ref, w_ref, resid_ref, gate_ref, o_ref):
    acc = jnp.dot(a_ref[...], w_ref[...].astype(BF16), preferred_element_type=F32)
    o_ref[...] = resid_ref[...] + gate_ref[0] * acc


def _proj_resid(lay, a, w, w_index, resid, mod3, layer, k_gate):
    kdim = a.shape[1]
    n = w.shape[-1]
    tn = 512
    w_spec = pl.BlockSpec((None,) * len(w_index) + (kdim, tn), lambda j, i: (*w_index, 0, j))
    return pl.pallas_call(
        _proj_resid_kernel,
        grid=(n // tn, lay.n_tiles),
        in_specs=[
            pl.BlockSpec((lay.tm, kdim), lambda j, i: (i, 0)),
            w_spec,
            pl.BlockSpec((lay.tm, tn), lambda j, i: (i, j)),
            pl.BlockSpec((1, 1, tn), lambda j, i: (_mod_index(lay, layer, k_gate, i), 0, j)),
        ],
        out_specs=pl.BlockSpec((lay.tm, tn), lambda j, i: (i, j)),
        out_shape=jax.ShapeDtypeStruct((lay.rows, n), F32),
        compiler_params=_params(2),
        name="proj_resid",
    )(a, w, resid, mod3)


OUT_PROJ_SUB = 512


def _out_proj_kernel(a_ref, w_ref, resid_ref, gate_ref, *rest, with_norm):
    if with_norm:
        g_ref, shift_ref, scale_ref, o_ref, h_ref = rest
    else:
        (o_ref,) = rest
    a = a_ref[...]
    for c0 in range(0, o_ref.shape[1], OUT_PROJ_SUB):
        cols = slice(c0, c0 + OUT_PROJ_SUB)
        acc = jnp.dot(a, w_ref[:, cols].astype(BF16), preferred_element_type=F32)
        o_ref[:, cols] = resid_ref[:, cols] + gate_ref[0][:, cols] * acc
    if with_norm:
        x = o_ref[...]
        y = x * lax.rsqrt(jnp.mean(x * x, axis=-1, keepdims=True) + NORM_EPS) * g_ref[...]
        h_ref[...] = (y * (1.0 + scale_ref[0]) + shift_ref[0]).astype(h_ref.dtype)


def _out_proj(lay, a, w, layer, resid, mod3, k_gate, norm_g=None, k_shift=None, k_scale=None):
    kdim, n = w.shape[-2:]
    with_norm = norm_g is not None
    row = lambda width: pl.BlockSpec((lay.tm, width), lambda i: (i, 0))
    mod = lambda k: pl.BlockSpec((1, 1, n), lambda i: (_mod_index(lay, layer, k, i), 0, 0))
    in_specs = [
        row(kdim),
        pl.BlockSpec((None, kdim, n), lambda i: (layer, 0, 0), pipeline_mode=pl.Buffered(1)),
        row(n),
        mod(k_gate),
    ]
    args = [a, w, resid, mod3]
    x_shape = jax.ShapeDtypeStruct((lay.rows, n), F32)
    if with_norm:
        in_specs += [pl.BlockSpec((1, n), lambda i: (0, 0)), mod(k_shift), mod(k_scale)]
        args += [norm_g.reshape(1, n), mod3, mod3]
        out_specs, out_shape = [row(n), row(n)], [x_shape, jax.ShapeDtypeStruct((lay.rows, n), BF16)]
    else:
        out_specs, out_shape = row(n), x_shape
    return pl.pallas_call(
        functools.partial(_out_proj_kernel, with_norm=with_norm),
        grid=(lay.n_tiles,),
        in_specs=in_specs,
        out_specs=out_specs,
        out_shape=out_shape,
        compiler_params=_params(1),
        name="out_proj",
    )(*args)


def _swiglu_up_kernel(h_ref, w1_ref, w3_ref, o_ref):
    h = h_ref[...]
    for c0 in range(0, o_ref.shape[1], SWIGLU_SUB):
        cols = slice(c0, c0 + SWIGLU_SUB)
        a = jnp.dot(h, w1_ref[:, cols].astype(BF16), preferred_element_type=F32)
        b = jnp.dot(h, w3_ref[:, cols].astype(BF16), preferred_element_type=F32)
        o_ref[:, cols] = (_silu(a) * b).astype(o_ref.dtype)


def _swiglu_up(lay, h, w1, w3, w_index):
    d = h.shape[1]
    f = w1.shape[-1]
    tn = 512
    w_spec = pl.BlockSpec((None,) * len(w_index) + (d, tn), lambda j, i: (*w_index, 0, j))
    return pl.pallas_call(
        _swiglu_up_kernel,
        grid=(f // tn, lay.n_tiles),
        in_specs=[pl.BlockSpec((lay.tm, d), lambda j, i: (i, 0)), w_spec, w_spec],
        out_specs=pl.BlockSpec((lay.tm, tn), lambda j, i: (i, j)),
        out_shape=jax.ShapeDtypeStruct((lay.rows, f), BF16),
        compiler_params=_params(2),
        name="swiglu_up",
    )(h, w1, w3)


def _pack_bf16_pairs(y):
    half = y.shape[1] // 2
    lo = lax.bitcast_convert_type(y[:, :half].astype(BF16).astype(F32), jnp.uint32) >> 16
    hi = lax.bitcast_convert_type(y[:, half:].astype(BF16).astype(F32), jnp.uint32)
    return hi | lo


def _unpack_bf16_pairs(w):
    lo = lax.bitcast_convert_type(w << 16, F32).astype(BF16)
    hi = lax.bitcast_convert_type(w & jnp.uint32(0xFFFF0000), F32).astype(BF16)
    return jnp.concatenate([lo, hi], axis=1)


def _norm_router_kernel(x_ref, g_ref, shift_ref, scale_ref, rw_ref, rb_ref,
                        hp_ref, sel_ref, prob_ref, cnt_ref, run_ref):
    @pl.when(pl.program_id(0) == 0)
    def _():
        run_ref[...] = jnp.zeros_like(run_ref)

    x = x_ref[...]
    y = x * lax.rsqrt(jnp.mean(x * x, axis=-1, keepdims=True) + NORM_EPS) * g_ref[...]
    h = y * (1.0 + scale_ref[0]) + shift_ref[0]
    hp_ref[...] = _pack_bf16_pairs(h)

    logits = jnp.dot(h.astype(BF16), rw_ref[...], preferred_element_type=F32) + rb_ref[...]
    lane = lax.broadcasted_iota(jnp.int32, logits.shape, 1)
    neg = jnp.float32(-jnp.inf)
    lg = jnp.where(lane < N_EXPERTS, logits, neg)
    m1 = jnp.max(lg, axis=-1, keepdims=True)
    i1 = jnp.min(jnp.where(lg == m1, lane, LANES), axis=-1, keepdims=True)
    lg2 = jnp.where(lane == i1, neg, lg)
    m2 = jnp.max(lg2, axis=-1, keepdims=True)
    i2 = jnp.min(jnp.where(lg2 == m2, lane, LANES), axis=-1, keepdims=True)
    e2 = jnp.exp(m2 - m1)
    den = 1.0 + e2

    tm = x.shape[0]
    onehot = jnp.where(lane == i1, 1.0, 0.0) + jnp.where(lane == i2, 1.0, 0.0)
    earlier = jnp.where(lax.broadcasted_iota(jnp.int32, (tm, tm), 0) > lax.broadcasted_iota(jnp.int32, (tm, tm), 1),
                        1.0, 0.0).astype(BF16)
    before = jnp.dot(earlier, onehot.astype(BF16), preferred_element_type=F32) + run_ref[0:1, :]
    pos1 = jnp.sum(jnp.where(lane == i1, before, 0.0), axis=-1, keepdims=True).astype(jnp.int32)
    pos2 = jnp.sum(jnp.where(lane == i2, before, 0.0), axis=-1, keepdims=True).astype(jnp.int32)
    run_ref[...] = run_ref[...] + jnp.sum(onehot, axis=0, keepdims=True)
    cnt_ref[...] = run_ref[...]

    sel_ref[...] = jnp.where(lane == 0, i1, jnp.where(lane == 1, i2, jnp.where(lane == 2, pos1,
                                                                                jnp.where(lane == 3, pos2, 0))))
    prob_ref[...] = jnp.where(lane == 0, 1.0 / den, jnp.where(lane == 1, e2 / den, 0.0))


def _norm_router(lay, x_all, g, mod3, layer, k_shift, k_scale, router_w, router_b):
    d = x_all.shape[1]
    w = jnp.zeros((d, LANES), BF16).at[:, :N_EXPERTS].set(router_w.astype(BF16))
    b = jnp.zeros((1, LANES), F32).at[0, :N_EXPERTS].set(router_b.astype(F32))
    lane_block = pl.BlockSpec((lay.tm, LANES), lambda i: (i, 0))
    return pl.pallas_call(
        _norm_router_kernel,
        grid=(lay.n_tiles,),
        in_specs=[
            pl.BlockSpec((lay.tm, d), lambda i: (i, 0)),
            pl.BlockSpec((1, d), lambda i: (0, 0)),
            pl.BlockSpec((1, 1, d), lambda i: (_mod_index(lay, layer, k_shift, i), 0, 0)),
            pl.BlockSpec((1, 1, d), lambda i: (_mod_index(lay, layer, k_scale, i), 0, 0)),
            pl.BlockSpec((d, LANES), lambda i: (0, 0)),
            pl.BlockSpec((1, LANES), lambda i: (0, 0)),
        ],
        out_specs=[
            pl.BlockSpec((lay.tm, d // 2), lambda i: (i, 0)),
            lane_block,
            lane_block,
            pl.BlockSpec((8, LANES), lambda i: (0, 0)),
        ],
        out_shape=[
            jax.ShapeDtypeStruct((lay.rows, d // 2), jnp.uint32),
            jax.ShapeDtypeStruct((lay.rows, LANES), jnp.int32),
            jax.ShapeDtypeStruct((lay.rows, LANES), F32),
            jax.ShapeDtypeStruct((8, LANES), F32),
        ],
        scratch_shapes=[pltpu.VMEM((8, LANES), F32)],
        compiler_params=_params(1),
        name="norm_router",
    )(x_all, g.reshape(1, d), mod3, mod3, w, b)


def _routing_plan(sel, cnt, n_sorted_rows):
    counts = cnt[0, :N_EXPERTS].astype(jnp.int32)
    padded = (counts + MOE_TILE - 1) // MOE_TILE * MOE_TILE
    ends = jnp.cumsum(padded)
    starts = ends - padded
    dest1 = starts[sel[:, 0]] + sel[:, 2]
    dest2 = starts[sel[:, 1]] + sel[:, 3]
    tile_start = jnp.arange(n_sorted_rows // MOE_TILE, dtype=jnp.int32) * MOE_TILE
    tile_expert = jnp.minimum(jnp.sum(tile_start[:, None] >= ends[None, :], axis=1), N_EXPERTS - 1)
    n_used_tiles = (ends[-1:] // MOE_TILE).astype(jnp.int32)
    present = counts > 0
    run_expert = jnp.argsort(jnp.logical_not(present), stable=True).astype(jnp.int32)
    run_of_expert = jnp.cumsum(present.astype(jnp.int32)) - 1
    tile_run = jnp.maximum(run_of_expert[tile_expert], 0).astype(jnp.int32)
    run_first_tile = (starts[run_expert] // MOE_TILE).astype(jnp.int32)
    n_runs = jnp.sum(present.astype(jnp.int32)).reshape(1)
    return dest1, dest2, (tile_run, run_first_tile, run_expert, n_runs, n_used_tiles)


def _dispatch_kernel(dest1_ref, dest2_ref, n_used_ref, h_ref, o_ref, src_ref, buf, sems):
    i = pl.program_id(0)
    n_tok = h_ref.shape[0]
    tile = o_ref.shape[0]
    n_used = n_used_ref[0]

    def row_copy(slot, r, t):
        return pltpu.make_async_copy(h_ref.at[pl.ds(t, 1)], buf.at[slot, pl.ds(r, 1)], sems.at[slot])

    def start_tile(tile_index):
        slot = lax.rem(tile_index, 2)
        base = tile_index * tile

        def start(r, carry):
            row_copy(slot, r, src_ref[base + r]).start()
            return carry

        lax.fori_loop(0, tile, start, 0, unroll=4)

    @pl.when(i == 0)
    def _():
        def clear(r, carry):
            src_ref[r] = 0
            return carry

        def invert(t, carry):
            src_ref[dest1_ref[t]] = t
            src_ref[dest2_ref[t]] = t
            return carry

        lax.fori_loop(0, src_ref.shape[0], clear, 0, unroll=8)
        lax.fori_loop(0, n_tok, invert, 0, unroll=4)
        start_tile(0)

    @pl.when(i + 1 < n_used)
    def _():
        start_tile(i + 1)

    @pl.when(i < n_used)
    def _():
        slot = lax.rem(i, 2)

        def wait(r, carry):
            row_copy(slot, 0, 0).wait()
            return carry

        lax.fori_loop(0, tile, wait, 0, unroll=4)
        o_ref[...] = buf[slot]

    @pl.when(i >= n_used)
    def _():
        o_ref[...] = jnp.zeros_like(o_ref)


def _dispatch(hp, dest1, dest2, n_used, n_sorted_rows):
    width = hp.shape[1]
    return pl.pallas_call(
        _dispatch_kernel,
        grid_spec=pltpu.PrefetchScalarGridSpec(
            num_scalar_prefetch=3,
            grid=(n_sorted_rows // MOE_TILE,),
            in_specs=[pl.BlockSpec(memory_space=pl.ANY)],
            out_specs=pl.BlockSpec((MOE_TILE, width), lambda i, d1, d2, nu: (i, 0)),
            scratch_shapes=[
                pltpu.SMEM((n_sorted_rows,), jnp.int32),
                pltpu.VMEM((2, MOE_TILE, width), hp.dtype),
                pltpu.SemaphoreType.DMA((2,)),
            ],
        ),
        out_shape=jax.ShapeDtypeStruct((n_sorted_rows, width), hp.dtype),
        compiler_params=_params(1),
        name="moe_dispatch",
    )(dest1, dest2, n_used, hp)


def _grouped_kernel(tile_run_ref, run_first_ref, run_expert_ref, n_runs_ref, n_used_ref, a_ref, *rest,
                    moe_layer, tn, swiglu):
    n_mats = 2 if swiglu else 1
    w_hbm = rest[:n_mats]
    o_ref, wbuf, sems = rest[n_mats:]
    j, i = pl.program_id(0), pl.program_id(1)
    n_runs = n_runs_ref[0]
    run = tile_run_ref[i]
    seq = j * n_runs + run
    slot = lax.rem(seq, 2)

    def fetches(jj, rr, slot_):
        expert = run_expert_ref[rr]
        col = pl.multiple_of(jj * tn, tn)
        return [pltpu.make_async_copy(w.at[moe_layer, expert, :, pl.ds(col, tn)], wbuf.at[slot_, m], sems.at[slot_, m])
                for m, w in enumerate(w_hbm)]

    live = i < n_used_ref[0]

    @pl.when(live & (i == run_first_ref[run]))
    def _():
        @pl.when(seq == 0)
        def _():
            for c in fetches(j, run, slot):
                c.start()

        for c in fetches(j, run, slot):
            c.wait()
        wraps = run + 1 == n_runs
        next_j = jnp.where(wraps, j + 1, j)
        next_run = jnp.where(wraps, 0, run + 1)

        @pl.when(next_j < pl.num_programs(0))
        def _():
            for c in fetches(next_j, next_run, 1 - slot):
                c.start()

    @pl.when(live)
    def _():
        if swiglu:
            h = _unpack_bf16_pairs(a_ref[...])
            for c0 in range(0, tn, SWIGLU_SUB):
                cols = slice(c0, c0 + SWIGLU_SUB)
                a = jnp.dot(h, wbuf[slot, 0, :, cols].astype(BF16), preferred_element_type=F32)
                b = jnp.dot(h, wbuf[slot, 1, :, cols].astype(BF16), preferred_element_type=F32)
                o_ref[:, cols] = (_silu(a) * b).astype(o_ref.dtype)
        else:
            o_ref[...] = jnp.dot(a_ref[...], wbuf[slot, 0].astype(BF16), preferred_element_type=F32)

    @pl.when(jnp.logical_not(live))
    def _():
        o_ref[...] = jnp.zeros_like(o_ref)


def _grouped_matmul(a, weights, moe_layer, plan, out_dtype, *, swiglu, name):
    n_rows, a_width = a.shape
    kdim, n = weights[0].shape[-2:]
    tn = 512
    n_prefetch = 5
    idx = lambda j, i, *_: (i, 0)
    return pl.pallas_call(
        functools.partial(_grouped_kernel, moe_layer=moe_layer, tn=tn, swiglu=swiglu),
        grid_spec=pltpu.PrefetchScalarGridSpec(
            num_scalar_prefetch=n_prefetch,
            grid=(n // tn, n_rows // MOE_TILE),
            in_specs=[pl.BlockSpec((MOE_TILE, a_width), idx)] + [pl.BlockSpec(memory_space=pl.ANY)] * len(weights),
            out_specs=pl.BlockSpec((MOE_TILE, tn), lambda j, i, *_: (i, j)),
            scratch_shapes=[
                pltpu.VMEM((2, len(weights), kdim, tn), F32),
                pltpu.SemaphoreType.DMA((2, len(weights))),
            ],
        ),
        out_shape=jax.ShapeDtypeStruct((n_rows, n), out_dtype),
        compiler_params=_params(2),
        name=name,
    )(*plan, a, *weights)


def _combine_kernel(dest1_ref, dest2_ref, x_ref, prob_ref, gate_ref, ys_ref, *rest, mode, first_tile):
    if mode == "final":
        g_ref, o_ref, buf, sems = rest
    else:
        g_ref, shift_ref, scale_ref, o_ref, h_ref, buf, sems = rest
    tm = x_ref.shape[0]
    i = pl.program_id(0)

    def row_copy(slot, k, t, d):
        return pltpu.make_async_copy(ys_ref.at[pl.ds(d, 1)], buf.at[slot, k, pl.ds(t, 1)], sems.at[slot, k])

    def start_tile(step):
        slot = lax.rem(step, 2)
        base = (step + first_tile) * tm

        def start(t, carry):
            row_copy(slot, 0, t, dest1_ref[base + t]).start()
            row_copy(slot, 1, t, dest2_ref[base + t]).start()
            return carry

        lax.fori_loop(0, tm, start, 0, unroll=2)

    @pl.when(i == 0)
    def _():
        start_tile(0)

    @pl.when(i + 1 < pl.num_programs(0))
    def _():
        start_tile(i + 1)

    slot = lax.rem(i, 2)

    def wait(t, carry):
        row_copy(slot, 0, 0, 0).wait()
        row_copy(slot, 1, 0, 0).wait()
        return carry

    lax.fori_loop(0, tm, wait, 0, unroll=2)
    p = prob_ref[...]
    f = p[:, 0:1] * buf[slot, 0] + p[:, 1:2] * buf[slot, 1]
    x = x_ref[...] + gate_ref[0] * f
    y = x * lax.rsqrt(jnp.mean(x * x, axis=-1, keepdims=True) + NORM_EPS) * g_ref[...]
    if mode == "final":
        o_ref[...] = y
    else:
        o_ref[...] = x
        h_ref[...] = (y * (1.0 + scale_ref[0]) + shift_ref[0]).astype(h_ref.dtype)


def _combine(lay, x_all, prob, ys, dest1, dest2, mod3, layer, k_gate, norm_g, *, final):
    d = x_all.shape[1]
    first_tile = 1 if final else 0
    tile = lambda i, d1, d2: (i + first_tile, 0)
    mod = lambda lyr, k: (lambda i, d1, d2: (_mod_index(lay, lyr, k, i + first_tile), 0, 0))
    row_spec = pl.BlockSpec((lay.tm, d), lambda i, d1, d2: (i, 0))
    in_specs = [
        pl.BlockSpec((lay.tm, d), tile),
        pl.BlockSpec((lay.tm, LANES), tile),
        pl.BlockSpec((1, 1, d), mod(layer, k_gate)),
        pl.BlockSpec(memory_space=pl.ANY),
        pl.BlockSpec((1, d), lambda i, d1, d2: (0, 0)),
    ]
    args = [x_all, prob, mod3, ys, norm_g.reshape(1, d)]
    n_tiles = lay.n_tiles - first_tile
    if final:
        out_specs = row_spec
        out_shape = jax.ShapeDtypeStruct((n_tiles * lay.tm, d), F32)
    else:
        in_specs += [pl.BlockSpec((1, 1, d), mod(layer + 1, 0)), pl.BlockSpec((1, 1, d), mod(layer + 1, 1))]
        args += [mod3, mod3]
        out_specs = [row_spec, row_spec]
        out_shape = [jax.ShapeDtypeStruct((lay.rows, d), F32), jax.ShapeDtypeStruct((lay.rows, d), BF16)]
    return pl.pallas_call(
        functools.partial(_combine_kernel, mode="final" if final else "next", first_tile=first_tile),
        grid_spec=pltpu.PrefetchScalarGridSpec(
            num_scalar_prefetch=2,
            grid=(n_tiles,),
            in_specs=in_specs,
            out_specs=out_specs,
            scratch_shapes=[pltpu.VMEM((2, 2, lay.tm, d), F32), pltpu.SemaphoreType.DMA((2, 2))],
        ),
        out_shape=out_shape,
        compiler_params=_params(1),
        name="moe_combine",
    )(dest1, dest2, *args)


def _moe_ffn(lay, x_all, norm_g, mod3, layer, router_w, router_b, w1, w3, w2, moe_layer, next_norm_g, *, final):
    hp, sel, prob, cnt = _norm_router(lay, x_all, norm_g, mod3, layer, 3, 4, router_w, router_b)
    n_sorted_rows = 2 * lay.rows + N_EXPERTS * MOE_TILE
    dest1, dest2, plan = _routing_plan(sel, cnt, n_sorted_rows)
    xs = _dispatch(hp, dest1, dest2, plan[-1], n_sorted_rows)
    act = _grouped_matmul(xs, (w1, w3), moe_layer, plan, BF16, swiglu=True, name="moe_up")
    ys = _grouped_matmul(act, (w2,), moe_layer, plan, F32, swiglu=False, name="moe_down")
    return _combine(lay, x_all, prob, ys, dest1, dest2, mod3, layer, 5, next_norm_g, final=final)


def _rope_tables(lay):
    t = np.arange(lay.seq)
    quarter = RET_QK_DIM // 4
    freqs = np.float32(ROPE_BASE) ** (-np.arange(quarter, dtype=np.float32) / np.float32(quarter))

    def tables(pos):
        ang = pos.astype(np.float32)[:, None] * freqs[None, :]
        cos, sin = np.cos(ang), np.sin(ang)
        return np.concatenate([cos, cos], axis=-1), np.concatenate([-sin, sin], axis=-1)

    cos_r, sin_r = tables(t // GRID_W)
    cos_c, sin_c = tables(t % GRID_W)
    cos_t = np.concatenate([np.ones((lay.tm, RET_QK_DIM), np.float32), np.concatenate([cos_r, cos_c], -1)], axis=0)
    sin_t = np.concatenate([np.zeros((lay.tm, RET_QK_DIM), np.float32), np.concatenate([sin_r, sin_c], -1)], axis=0)
    return jnp.asarray(cos_t, F32), jnp.asarray(sin_t, F32)


def kernel(x, c, ctx, c_ctx, w_mod, b_mod, norm1_g, norm2_g, w_in, sgu_ln_g, sgu_ln_b, sgu_w, sgu_b,
           ret_log_decay, w_proj_a, w_proj_r, w_out, ffn_w1, ffn_w3, ffn_w2, router_w, router_b,
           moe_w1, moe_w3, moe_w2, final_norm_g):
    batch, seq, d = x.shape
    ctx_len = ctx.shape[1]
    depth = w_mod.shape[0]
    lay = _Layout(batch, seq, ctx_len)

    cvec = jnp.zeros((MOD_ROWS, d), F32).at[:batch].set(c).at[batch].set(c_ctx)
    mod = _modulation(cvec, w_mod, b_mod)
    mod3 = mod.reshape(depth * MOD_ROWS * N_MOD, 1, d)

    cos_t, sin_t = _rope_tables(lay)
    x_all = jnp.concatenate([ctx.reshape(batch * ctx_len, d), x.reshape(batch * seq, d)], axis=0)

    log_decay = ret_log_decay.astype(F32)

    h = None
    out = None
    for layer in range(depth):
        if h is None:
            h = _norm_modulate(lay, x_all, norm1_g[layer], mod3, layer, 0, 1, BF16)
        qk = _in_proj(lay, h, w_in, layer, Q0, V0 - Q0, "rope", (cos_t, sin_t))
        v = _in_proj(lay, h, w_in, layer, V0, G0 - V0, "plain")
        g = _in_proj(lay, h, w_in, layer, G0, UA0 - G0, "silu")
        uv = _in_proj(lay, h, w_in, layer, UA0, GA0 - UA0, "gelu")
        gates = _in_proj(lay, h, w_in, layer, GA0, IN_COLS - GA0, "sigmoid")
        o_bwd = _retention_pass(lay, qk, v, log_decay[layer, 1], reverse=True)
        ret = _retention_pass(lay, qk, v, log_decay[layer, 0], o_bwd, g, reverse=False)
        sgu = _sgu(lay, uv, sgu_ln_g[layer], sgu_ln_b[layer], sgu_w[layer], sgu_b[layer].T)
        y = _merge(lay, sgu, ret, gates, w_proj_a, w_proj_r, layer)
        i = layer // 2
        last = layer == depth - 1
        h = None
        if layer % 2 == 0:
            x_all, h2 = _out_proj(lay, y, w_out, layer, x_all, mod3, 2, norm2_g[layer], 3, 4)
            act = _swiglu_up(lay, h2, ffn_w1, ffn_w3, (i,))
            x_all = _proj_resid(lay, act, ffn_w2, (i,), x_all, mod3, layer, 5)
        else:
            x_all = _out_proj(lay, y, w_out, layer, x_all, mod3, 2)
            res = _moe_ffn(lay, x_all, norm2_g[layer], mod3, layer, router_w[i], router_b[i], moe_w1, moe_w3, moe_w2,
                           i, final_norm_g if last else norm1_g[layer + 1], final=last)
            if last:
                out = res
            else:
                x_all, h = res

    if out is None:
        out = _final_norm(lay, x_all, final_norm_g)
    return out.reshape(batch, seq, d)
```

```python
import functools

import jax
import jax.numpy as jnp
import numpy as np
from jax import lax
from jax.experimental import pallas as pl
from jax.experimental.pallas import tpu as pltpu

F32 = jnp.float32
BF16 = jnp.bfloat16

GRID_W = 64
CHUNK = 128
SGU_GROUPS = 8
SGU_WIDTH = 2048
RET_HEADS = 8
RET_QK_DIM = 256
RET_V_DIM = 512
RET_QK = RET_HEADS * RET_QK_DIM
RET_V = RET_HEADS * RET_V_DIM
ROPE_BASE = 10000.0
N_EXPERTS = 8
N_MOD = 6
NORM_EPS = 1e-6
MOD_ROWS = 8
LANES = 128
MOE_TILE = 512
ROW_DMA_UNROLL = 4
SWIGLU_SUB = 256

VMEM_LIMIT_BYTES = 52 * 1024 * 1024

Q0 = 0
K0 = Q0 + RET_QK
V0 = K0 + RET_QK
G0 = V0 + RET_V
UA0 = G0 + RET_V
VA0 = UA0 + SGU_WIDTH
GA0 = VA0 + SGU_WIDTH
GR0 = GA0 + 2048
IN_COLS = GR0 + 2048


def _params(n_grid_dims):
    return pltpu.CompilerParams(
        dimension_semantics=("arbitrary",) * n_grid_dims,
        vmem_limit_bytes=VMEM_LIMIT_BYTES,
    )


def _gelu_tanh(x):
    c = 0.7978845608028654
    half_x = 0.5 * x
    return half_x * jnp.tanh(x * (c + (c * 0.044715) * (x * x))) + half_x


def _sigmoid(x):
    return 0.5 * jnp.tanh(0.5 * x) + 0.5


def _silu(x):
    return x * _sigmoid(x)


def _mod_kernel(c_ref, w_ref, b_ref, o_ref):
    sc = _silu(c_ref[...]).astype(BF16)
    o_ref[...] = jnp.dot(sc, w_ref[...].astype(BF16), preferred_element_type=F32) + b_ref[...]


def _modulation(cvec, w_mod, b_mod):
    depth, d, n = w_mod.shape
    tn = 1024
    return pl.pallas_call(
        _mod_kernel,
        grid=(depth, n // tn),
        in_specs=[
            pl.BlockSpec((MOD_ROWS, d), lambda l, j: (0, 0)),
            pl.BlockSpec((None, d, tn), lambda l, j: (l, 0, j)),
            pl.BlockSpec((None, 1, tn), lambda l, j: (l, 0, j)),
        ],
        out_specs=pl.BlockSpec((None, MOD_ROWS, tn), lambda l, j: (l, 0, j)),
        out_shape=jax.ShapeDtypeStruct((depth, MOD_ROWS, n), F32),
        compiler_params=_params(2),
        name="modulation",
    )(cvec, w_mod, b_mod.reshape(depth, 1, n))


def _norm_mod_kernel(x_ref, g_ref, shift_ref, scale_ref, o_ref):
    x = x_ref[...]
    y = x * lax.rsqrt(jnp.mean(x * x, axis=-1, keepdims=True) + NORM_EPS) * g_ref[...]
    o_ref[...] = (y * (1.0 + scale_ref[0]) + shift_ref[0]).astype(o_ref.dtype)


def _norm_kernel(x_ref, g_ref, o_ref):
    x = x_ref[...]
    y = x * lax.rsqrt(jnp.mean(x * x, axis=-1, keepdims=True) + NORM_EPS) * g_ref[...]
    o_ref[...] = y.astype(o_ref.dtype)


class _Layout:
    def __init__(self, batch, seq, ctx_len):
        self.batch, self.seq, self.ctx_len = batch, seq, ctx_len
        self.tm = batch * ctx_len
        assert seq % self.tm == 0 and ctx_len % CHUNK == 0 and seq % CHUNK == 0
        self.tiles_per_batch = seq // self.tm
        self.n_tiles = 1 + batch * self.tiles_per_batch
        self.rows = self.n_tiles * self.tm
        self.ctx_chunks = ctx_len // CHUNK
        self.lat_chunks = seq // CHUNK

    def mod_row(self, tile):
        return jnp.where(tile == 0, self.batch, (tile - 1) // self.tiles_per_batch)


def _mod_index(lay, layer, k, tile):
    return layer * MOD_ROWS * N_MOD + lay.mod_row(tile) * N_MOD + k


def _norm_modulate(lay, x_all, g, mod3, layer, k_shift, k_scale, out_dtype):
    d = x_all.shape[1]
    return pl.pallas_call(
        _norm_mod_kernel,
        grid=(lay.n_tiles,),
        in_specs=[
            pl.BlockSpec((lay.tm, d), lambda i: (i, 0)),
            pl.BlockSpec((1, d), lambda i: (0, 0)),
            pl.BlockSpec((1, 1, d), lambda i: (_mod_index(lay, layer, k_shift, i), 0, 0)),
            pl.BlockSpec((1, 1, d), lambda i: (_mod_index(lay, layer, k_scale, i), 0, 0)),
        ],
        out_specs=pl.BlockSpec((lay.tm, d), lambda i: (i, 0)),
        out_shape=jax.ShapeDtypeStruct((lay.rows, d), out_dtype),
        compiler_params=_params(1),
        name="norm_modulate",
    )(x_all, g.reshape(1, d), mod3, mod3)


def _final_norm(lay, x_all, g):
    d = x_all.shape[1]
    n_lat_tiles = lay.n_tiles - 1
    return pl.pallas_call(
        _norm_kernel,
        grid=(n_lat_tiles,),
        in_specs=[
            pl.BlockSpec((lay.tm, d), lambda i: (i + 1, 0)),
            pl.BlockSpec((1, d), lambda i: (0, 0)),
        ],
        out_specs=pl.BlockSpec((lay.tm, d), lambda i: (i, 0)),
        out_shape=jax.ShapeDtypeStruct((n_lat_tiles * lay.tm, d), F32),
        compiler_params=_params(1),
        name="final_norm",
    )(x_all, g.reshape(1, d))


IN_PROJ_TN = 2048
IN_PROJ_SUB = 512


def _in_proj_kernel(h_ref, w_ref, *rest, kind):
    o_ref = rest[-1]
    h = h_ref[...]
    if kind == "rope":
        cos_ref, sin_ref = rest[:2]
        scale = jnp.where(pl.program_id(0) == 0, 1.0, RET_QK_DIM ** -0.5).astype(F32)
    for c0 in range(0, IN_PROJ_TN, IN_PROJ_SUB):
        acc = jnp.dot(h, w_ref[:, c0:c0 + IN_PROJ_SUB].astype(BF16), preferred_element_type=F32)
        if kind == "rope":
            for s in range(0, IN_PROJ_SUB, LANES):
                t = (c0 + s) % RET_QK_DIM
                xs = acc[:, s:s + LANES]
                r = xs * cos_ref[:, t:t + LANES] + pltpu.roll(xs, LANES // 2, 1) * sin_ref[:, t:t + LANES]
                o_ref[:, c0 + s:c0 + s + LANES] = (r * scale).astype(o_ref.dtype)
        else:
            if kind == "silu":
                acc = _silu(acc)
            elif kind == "gelu":
                acc = _gelu_tanh(acc)
            elif kind == "sigmoid":
                acc = _sigmoid(acc)
            else:
                assert kind == "plain"
            o_ref[:, c0:c0 + IN_PROJ_SUB] = acc.astype(o_ref.dtype)


def _in_proj(lay, h, w_in, layer, col0, n_cols, kind, tables=()):
    d = h.shape[1]
    tn = IN_PROJ_TN
    table_spec = pl.BlockSpec((lay.tm, RET_QK_DIM),
                              lambda j, i: (jnp.where(i == 0, 0, 1 + lax.rem(i - 1, lay.tiles_per_batch)), 0))
    return pl.pallas_call(
        functools.partial(_in_proj_kernel, kind=kind),
        grid=(n_cols // tn, lay.n_tiles),
        in_specs=[
            pl.BlockSpec((lay.tm, d), lambda j, i: (i, 0)),
            pl.BlockSpec((None, d, tn), lambda j, i: (layer, 0, col0 // tn + j)),
        ] + [table_spec] * len(tables),
        out_specs=pl.BlockSpec((lay.tm, tn), lambda j, i: (i, j)),
        out_shape=jax.ShapeDtypeStruct((lay.rows, n_cols), BF16),
        compiler_params=_params(2),
        name="in_proj_" + kind,
    )(h, w_in, *tables)


RET_STEP_CHUNKS = 1
RET_CHUNK = 256


def _retention_kernel(lg_ref, q_ref, k_ref, v_ref, *rest, reverse, final):
    if final:
        other_ref, gate_ref, o_ref, s_ref = rest
    else:
        o_ref, s_ref = rest
    c = RET_CHUNK

    @pl.when(pl.program_id(1) == 0)
    def _():
        s_ref[...] = jnp.zeros_like(s_ref)

    qi = lax.broadcasted_iota(jnp.int32, (c, c), 0)
    kj = lax.broadcasted_iota(jnp.int32, (c, c), 1)
    diff = ((kj - qi) if reverse else (qi - kj)).astype(F32)
    pos = lax.broadcasted_iota(jnp.int32, (c, 1), 0).astype(F32)
    q_steps = (c - pos) if reverse else (pos + 1.0)
    k_steps = pos if reverse else (c - 1.0 - pos)

    subs = range(RET_STEP_CHUNKS)
    for h in range(RET_HEADS):
        log_g = lg_ref[h]
        intra = jnp.where(diff >= 0, jnp.exp(jnp.maximum(diff, 0.0) * log_g), 0.0)
        q_decay = jnp.exp(q_steps * log_g)
        k_decay = jnp.exp(k_steps * log_g)
        chunk_decay = jnp.exp(jnp.full((1, 1), float(c), F32) * log_g)
        for sub in (reversed(subs) if reverse else subs):
            rows = slice(sub * c, (sub + 1) * c)
            qh = q_ref[rows, h * RET_QK_DIM:(h + 1) * RET_QK_DIM]
            kh = k_ref[rows, h * RET_QK_DIM:(h + 1) * RET_QK_DIM]
            vh = v_ref[rows, h * RET_V_DIM:(h + 1) * RET_V_DIM]
            s = s_ref[h]

            scores = lax.dot_general(qh, kh, (((1,), (1,)), ((), ())), preferred_element_type=F32) * intra
            o = (jnp.dot(scores.astype(BF16), vh, preferred_element_type=F32)
                 + jnp.dot(qh, s.astype(BF16), preferred_element_type=F32) * q_decay)
            k_dec_t = (kh.astype(F32) * k_decay).T.astype(BF16)
            s_ref[h] = s * chunk_decay + jnp.dot(k_dec_t, vh, preferred_element_type=F32)

            sl = slice(h * RET_V_DIM, (h + 1) * RET_V_DIM)
            if final:
                o = o + other_ref[rows, sl].astype(F32)
                mu = jnp.mean(o, axis=-1, keepdims=True)
                oc = o - mu
                var = jnp.mean(oc * oc, axis=-1, keepdims=True)
                o = oc * lax.rsqrt(var + NORM_EPS) * gate_ref[rows, sl].astype(F32)
            o_ref[rows, sl] = o.astype(o_ref.dtype)


def _retention_pass(lay, qk, v, log_g, other=None, gate=None, *, reverse):
    final = other is not None
    step_rows = RET_STEP_CHUNKS * RET_CHUNK
    assert lay.ctx_len % step_rows == 0 and lay.seq % step_rows == 0
    ctx_steps, lat_steps = lay.ctx_len // step_rows, lay.seq // step_rows

    def row_block(b, n):
        ctx_blk = b * ctx_steps + ((ctx_steps - 1 - n) if reverse else n)
        m = n - ctx_steps
        lat_blk = lay.batch * ctx_steps + b * lat_steps + ((lat_steps - 1 - m) if reverse else m)
        return jnp.where(n < ctx_steps, ctx_blk, lat_blk)

    in_specs = [
        pl.BlockSpec((step_rows, RET_QK), lambda b, n, lg: (row_block(b, n), 0)),
        pl.BlockSpec((step_rows, RET_QK), lambda b, n, lg: (row_block(b, n), 1)),
        pl.BlockSpec((step_rows, RET_V), lambda b, n, lg: (row_block(b, n), 0)),
    ]
    args = [qk, qk, v]
    if final:
        in_specs += [pl.BlockSpec((step_rows, RET_V), lambda b, n, lg: (row_block(b, n), 0))] * 2
        args += [other, gate]
    return pl.pallas_call(
        functools.partial(_retention_kernel, reverse=reverse, final=final),
        grid_spec=pltpu.PrefetchScalarGridSpec(
            num_scalar_prefetch=1,
            grid=(lay.batch, ctx_steps + lat_steps),
            in_specs=in_specs,
            out_specs=pl.BlockSpec((step_rows, RET_V), lambda b, n, lg: (row_block(b, n), 0)),
            scratch_shapes=[pltpu.VMEM((RET_HEADS, RET_QK_DIM, RET_V_DIM), F32)],
        ),
        out_shape=jax.ShapeDtypeStruct((lay.rows, RET_V), BF16),
        compiler_params=_params(2),
        name="retention_fwd_merge" if final else "retention_bwd",
    )(log_g, *args)


def _sgu_kernel(u_ref, v_ref, lng_ref, lnb_ref, ws_ref, bst_ref, o_ref):
    gw = SGU_WIDTH // SGU_GROUPS
    for c0 in range(0, o_ref.shape[0], CHUNK):
        rows = slice(c0, c0 + CHUNK)
        v = v_ref[rows, :].astype(F32)
        mu = jnp.mean(v, axis=-1, keepdims=True)
        vc = v - mu
        var = jnp.mean(vc * vc, axis=-1, keepdims=True)
        vn = (vc * lax.rsqrt(var + NORM_EPS) * lng_ref[...] + lnb_ref[...]).astype(BF16)
        for g in range(SGU_GROUPS):
            sl = slice(g * gw, (g + 1) * gw)
            s = jnp.dot(ws_ref[g].astype(BF16), vn[:, sl], preferred_element_type=F32) + bst_ref[:, g:g + 1]
            o_ref[rows, sl] = (u_ref[rows, sl].astype(F32) * s).astype(o_ref.dtype)


def _sgu(lay, uv, ln_g, ln_b, w_s, b_s_t):
    return pl.pallas_call(
        _sgu_kernel,
        grid=(lay.n_tiles,),
        in_specs=[
            pl.BlockSpec((lay.tm, SGU_WIDTH), lambda i: (i, 0)),
            pl.BlockSpec((lay.tm, SGU_WIDTH), lambda i: (i, 1)),
            pl.BlockSpec((1, SGU_WIDTH), lambda i: (0, 0)),
            pl.BlockSpec((1, SGU_WIDTH), lambda i: (0, 0)),
            pl.BlockSpec((SGU_GROUPS, CHUNK, CHUNK), lambda i: (0, 0, 0)),
            pl.BlockSpec((CHUNK, SGU_GROUPS), lambda i: (0, 0)),
        ],
        out_specs=pl.BlockSpec((lay.tm, SGU_WIDTH), lambda i: (i, 0)),
        out_shape=jax.ShapeDtypeStruct((lay.rows, SGU_WIDTH), BF16),
        compiler_params=_params(1),
        name="sgu",
    )(uv, uv, ln_g.reshape(1, -1), ln_b.reshape(1, -1), w_s, b_s_t)


def _merge_kernel(sgu_ref, ret_ref, wa_ref, wr_ref, ga_ref, gr_ref, o_ref):
    a = jnp.dot(sgu_ref[...], wa_ref[...].astype(BF16), preferred_element_type=F32)
    r = jnp.dot(ret_ref[...], wr_ref[...].astype(BF16), preferred_element_type=F32)
    o_ref[...] = (ga_ref[...].astype(F32) * a + gr_ref[...].astype(F32) * r).astype(o_ref.dtype)


def _merge(lay, sgu, ret, gates, w_proj_a, w_proj_r, layer):
    n = w_proj_a.shape[2]
    tn = 512
    return pl.pallas_call(
        _merge_kernel,
        grid=(n // tn, lay.n_tiles),
        in_specs=[
            pl.BlockSpec((lay.tm, SGU_WIDTH), lambda j, i: (i, 0)),
            pl.BlockSpec((lay.tm, RET_V), lambda j, i: (i, 0)),
            pl.BlockSpec((None, SGU_WIDTH, tn), lambda j, i: (layer, 0, j)),
            pl.BlockSpec((None, RET_V, tn), lambda j, i: (layer, 0, j)),
            pl.BlockSpec((lay.tm, tn), lambda j, i: (i, j)),
            pl.BlockSpec((lay.tm, tn), lambda j, i: (i, n // tn + j)),
        ],
        out_specs=pl.BlockSpec((lay.tm, tn), lambda j, i: (i, j)),
        out_shape=jax.ShapeDtypeStruct((lay.rows, n), BF16),
        compiler_params=_params(2),
        name="merge",
    )(sgu, ret, w_proj_a, w_proj_r, gates, gates)


def _proj_resid_kernel(a_ref, w_ref, resid_ref, gate_ref, o_ref):
    acc = jnp.dot(a_ref[...], w_ref[...].astype(BF16), preferred_element_type=F32)
    o_ref[...] = resid_ref[...] + gate_ref[0] * acc


def _proj_resid(lay, a, w, w_index, resid, mod3, layer, k_gate):
    kdim = a.shape[1]
    n = w.shape[-1]
    tn = 512
    w_spec = pl.BlockSpec((None,) * len(w_index) + (kdim, tn), lambda j, i: (*w_index, 0, j))
    return pl.pallas_call(
        _proj_resid_kernel,
        grid=(n // tn, lay.n_tiles),
        in_specs=[
            pl.BlockSpec((lay.tm, kdim), lambda j, i: (i, 0)),
            w_spec,
            pl.BlockSpec((lay.tm, tn), lambda j, i: (i, j)),
            pl.BlockSpec((1, 1, tn), lambda j, i: (_mod_index(lay, layer, k_gate, i), 0, j)),
        ],
        out_specs=pl.BlockSpec((lay.tm, tn), lambda j, i: (i, j)),
        out_shape=jax.ShapeDtypeStruct((lay.rows, n), F32),
        compiler_params=_params(2),
        name="proj_resid",
    )(a, w, resid, mod3)


OUT_PROJ_SUB = 512


def _out_proj_kernel(a_ref, w_ref, resid_ref, gate_ref, *rest, with_norm):
    if with_norm:
        g_ref, shift_ref, scale_ref, o_ref, h_ref = rest
    else:
        (o_ref,) = rest
    a = a_ref[...]
    for c0 in range(0, o_ref.shape[1], OUT_PROJ_SUB):
        cols = slice(c0, c0 + OUT_PROJ_SUB)
        acc = jnp.dot(a, w_ref[:, cols].astype(BF16), preferred_element_type=F32)
        o_ref[:, cols] = resid_ref[:, cols] + gate_ref[0][:, cols] * acc
    if with_norm:
        x = o_ref[...]
        y = x * lax.rsqrt(jnp.mean(x * x, axis=-1, keepdims=True) + NORM_EPS) * g_ref[...]
        h_ref[...] = (y * (1.0 + scale_ref[0]) + shift_ref[0]).astype(h_ref.dtype)


def _out_proj(lay, a, w, layer, resid, mod3, k_gate, norm_g=None, k_shift=None, k_scale=None):
    kdim, n = w.shape[-2:]
    with_norm = norm_g is not None
    row = lambda width: pl.BlockSpec((lay.tm, width), lambda i: (i, 0))
    mod = lambda k: pl.BlockSpec((1, 1, n), lambda i: (_mod_index(lay, layer, k, i), 0, 0))
    in_specs = [
        row(kdim),
        pl.BlockSpec((None, kdim, n), lambda i: (layer, 0, 0), pipeline_mode=pl.Buffered(1)),
        row(n),
        mod(k_gate),
    ]
    args = [a, w, resid, mod3]
    x_shape = jax.ShapeDtypeStruct((lay.rows, n), F32)
    if with_norm:
        in_specs += [pl.BlockSpec((1, n), lambda i: (0, 0)), mod(k_shift), mod(k_scale)]
        args += [norm_g.reshape(1, n), mod3, mod3]
        out_specs, out_shape = [row(n), row(n)], [x_shape, jax.ShapeDtypeStruct((lay.rows, n), BF16)]
    else:
        out_specs, out_shape = row(n), x_shape
    return pl.pallas_call(
        functools.partial(_out_proj_kernel, with_norm=with_norm),
        grid=(lay.n_tiles,),
        in_specs=in_specs,
        out_specs=out_specs,
        out_shape=out_shape,
        compiler_params=_params(1),
        name="out_proj",
    )(*args)


def _swiglu_up_kernel(h_ref, w1_ref, w3_ref, o_ref):
    h = h_ref[...]
    for c0 in range(0, o_ref.shape[1], SWIGLU_SUB):
        cols = slice(c0, c0 + SWIGLU_SUB)
        a = jnp.dot(h, w1_ref[:, cols].astype(BF16), preferred_element_type=F32)
        b = jnp.dot(h, w3_ref[:, cols].astype(BF16), preferred_element_type=F32)
        o_ref[:, cols] = (_silu(a) * b).astype(o_ref.dtype)


def _swiglu_up(lay, h, w1, w3, w_index):
    d = h.shape[1]
    f = w1.shape[-1]
    tn = 512
    w_spec = pl.BlockSpec((None,) * len(w_index) + (d, tn), lambda j, i: (*w_index, 0, j))
    return pl.pallas_call(
        _swiglu_up_kernel,
        grid=(f // tn, lay.n_tiles),
        in_specs=[pl.BlockSpec((lay.tm, d), lambda j, i: (i, 0)), w_spec, w_spec],
        out_specs=pl.BlockSpec((lay.tm, tn), lambda j, i: (i, j)),
        out_shape=jax.ShapeDtypeStruct((lay.rows, f), BF16),
        compiler_params=_params(2),
        name="swiglu_up",
    )(h, w1, w3)


def _pack_bf16_pairs(y):
    half = y.shape[1] // 2
    lo = lax.bitcast_convert_type(y[:, :half].astype(BF16).astype(F32), jnp.uint32) >> 16
    hi = lax.bitcast_convert_type(y[:, half:].astype(BF16).astype(F32), jnp.uint32)
    return hi | lo


def _unpack_bf16_pairs(w):
    lo = lax.bitcast_convert_type(w << 16, F32).astype(BF16)
    hi = lax.bitcast_convert_type(w & jnp.uint32(0xFFFF0000), F32).astype(BF16)
    return jnp.concatenate([lo, hi], axis=1)


def _norm_router_kernel(x_ref, g_ref, shift_ref, scale_ref, rw_ref, rb_ref,
                        hp_ref, sel_ref, prob_ref, cnt_ref, run_ref):
    @pl.when(pl.program_id(0) == 0)
    def _():
        run_ref[...] = jnp.zeros_like(run_ref)

    x = x_ref[...]
    y = x * lax.rsqrt(jnp.mean(x * x, axis=-1, keepdims=True) + NORM_EPS) * g_ref[...]
    h = y * (1.0 + scale_ref[0]) + shift_ref[0]
    hp_ref[...] = _pack_bf16_pairs(h)

    logits = jnp.dot(h.astype(BF16), rw_ref[...], preferred_element_type=F32) + rb_ref[...]
    lane = lax.broadcasted_iota(jnp.int32, logits.shape, 1)
    neg = jnp.float32(-jnp.inf)
    lg = jnp.where(lane < N_EXPERTS, logits, neg)
    m1 = jnp.max(lg, axis=-1, keepdims=True)
    i1 = jnp.min(jnp.where(lg == m1, lane, LANES), axis=-1, keepdims=True)
    lg2 = jnp.where(lane == i1, neg, lg)
    m2 = jnp.max(lg2, axis=-1, keepdims=True)
    i2 = jnp.min(jnp.where(lg2 == m2, lane, LANES), axis=-1, keepdims=True)
    e2 = jnp.exp(m2 - m1)
    den = 1.0 + e2

    tm = x.shape[0]
    onehot = jnp.where(lane == i1, 1.0, 0.0) + jnp.where(lane == i2, 1.0, 0.0)
    earlier = jnp.where(lax.broadcasted_iota(jnp.int32, (tm, tm), 0) > lax.broadcasted_iota(jnp.int32, (tm, tm), 1),
                        1.0, 0.0).astype(BF16)
    before = jnp.dot(earlier, onehot.astype(BF16), preferred_element_type=F32) + run_ref[0:1, :]
    pos1 = jnp.sum(jnp.where(lane == i1, before, 0.0), axis=-1, keepdims=True).astype(jnp.int32)
    pos2 = jnp.sum(jnp.where(lane == i2, before, 0.0), axis=-1, keepdims=True).astype(jnp.int32)
    run_ref[...] = run_ref[...] + jnp.sum(onehot, axis=0, keepdims=True)
    cnt_ref[...] = run_ref[...]

    sel_ref[...] = jnp.where(lane == 0, i1, jnp.where(lane == 1, i2, jnp.where(lane == 2, pos1,
                                                                                jnp.where(lane == 3, pos2, 0))))
    prob_ref[...] = jnp.where(lane == 0, 1.0 / den, jnp.where(lane == 1, e2 / den, 0.0))


def _norm_router(lay, x_all, g, mod3, layer, k_shift, k_scale, router_w, router_b):
    d = x_all.shape[1]
    w = jnp.zeros((d, LANES), BF16).at[:, :N_EXPERTS].set(router_w.astype(BF16))
    b = jnp.zeros((1, LANES), F32).at[0, :N_EXPERTS].set(router_b.astype(F32))
    lane_block = pl.BlockSpec((lay.tm, LANES), lambda i: (i, 0))
    return pl.pallas_call(
        _norm_router_kernel,
        grid=(lay.n_tiles,),
        in_specs=[
            pl.BlockSpec((lay.tm, d), lambda i: (i, 0)),
            pl.BlockSpec((1, d), lambda i: (0, 0)),
            pl.BlockSpec((1, 1, d), lambda i: (_mod_index(lay, layer, k_shift, i), 0, 0)),
            pl.BlockSpec((1, 1, d), lambda i: (_mod_index(lay, layer, k_scale, i), 0, 0)),
            pl.BlockSpec((d, LANES), lambda i: (0, 0)),
            pl.BlockSpec((1, LANES), lambda i: (0, 0)),
        ],
        out_specs=[
            pl.BlockSpec((lay.tm, d // 2), lambda i: (i, 0)),
            lane_block,
            lane_block,
            pl.BlockSpec((8, LANES), lambda i: (0, 0)),
        ],
        out_shape=[
            jax.ShapeDtypeStruct((lay.rows, d // 2), jnp.uint32),
            jax.ShapeDtypeStruct((lay.rows, LANES), jnp.int32),
            jax.ShapeDtypeStruct((lay.rows, LANES), F32),
            jax.ShapeDtypeStruct((8, LANES), F32),
        ],
        scratch_shapes=[pltpu.VMEM((8, LANES), F32)],
        compiler_params=_params(1),
        name="norm_router",
    )(x_all, g.reshape(1, d), mod3, mod3, w, b)


def _routing_plan(sel, cnt, n_sorted_rows):
    counts = cnt[0, :N_EXPERTS].astype(jnp.int32)
    padded = (counts + MOE_TILE - 1) // MOE_TILE * MOE_TILE
    ends = jnp.cumsum(padded)
    starts = ends - padded
    dest1 = starts[sel[:, 0]] + sel[:, 2]
    dest2 = starts[sel[:, 1]] + sel[:, 3]
    tile_start = jnp.arange(n_sorted_rows // MOE_TILE, dtype=jnp.int32) * MOE_TILE
    tile_expert = jnp.minimum(jnp.sum(tile_start[:, None] >= ends[None, :], axis=1), N_EXPERTS - 1)
    n_used_tiles = (ends[-1:] // MOE_TILE).astype(jnp.int32)
    present = counts > 0
    run_expert = jnp.argsort(jnp.logical_not(present), stable=True).astype(jnp.int32)
    run_of_expert = jnp.cumsum(present.astype(jnp.int32)) - 1
    tile_run = jnp.maximum(run_of_expert[tile_expert], 0).astype(jnp.int32)
    run_first_tile = (starts[run_expert] // MOE_TILE).astype(jnp.int32)
    n_runs = jnp.sum(present.astype(jnp.int32)).reshape(1)
    return dest1, dest2, (tile_run, run_first_tile, run_expert, n_runs, n_used_tiles)


def _dispatch_kernel(dest1_ref, dest2_ref, n_used_ref, h_ref, o_ref, src_ref, buf, sems):
    i = pl.program_id(0)
    n_tok = h_ref.shape[0]
    tile = o_ref.shape[0]
    n_used = n_used_ref[0]

    def row_copy(slot, r, t):
        return pltpu.make_async_copy(h_ref.at[pl.ds(t, 1)], buf.at[slot, pl.ds(r, 1)], sems.at[slot])

    def start_tile(tile_index):
        slot = lax.rem(tile_index, 2)
        base = tile_index * tile

        def start(q, carry):
            for u in range(ROW_DMA_UNROLL):
                r = q * ROW_DMA_UNROLL + u
                row_copy(slot, r, src_ref[base + r]).start(priority=u % 2)
            return carry

        lax.fori_loop(0, tile // ROW_DMA_UNROLL, start, 0)

    @pl.when(i == 0)
    def _():
        def clear(r, carry):
            src_ref[r] = 0
            return carry

        def invert(t, carry):
            src_ref[dest1_ref[t]] = t
            src_ref[dest2_ref[t]] = t
            return carry

        lax.fori_loop(0, src_ref.shape[0], clear, 0, unroll=8)
        lax.fori_loop(0, n_tok, invert, 0, unroll=4)
        start_tile(0)

    @pl.when(i + 1 < n_used)
    def _():
        start_tile(i + 1)

    @pl.when(i < n_used)
    def _():
        slot = lax.rem(i, 2)

        def wait(r, carry):
            row_copy(slot, 0, 0).wait()
            return carry

        lax.fori_loop(0, tile, wait, 0, unroll=4)
        o_ref[...] = buf[slot]

    @pl.when(i >= n_used)
    def _():
        o_ref[...] = jnp.zeros_like(o_ref)


def _dispatch(hp, dest1, dest2, n_used, n_sorted_rows):
    width = hp.shape[1]
    return pl.pallas_call(
        _dispatch_kernel,
        grid_spec=pltpu.PrefetchScalarGridSpec(
            num_scalar_prefetch=3,
            grid=(n_sorted_rows // MOE_TILE,),
            in_specs=[pl.BlockSpec(memory_space=pl.ANY)],
            out_specs=pl.BlockSpec((MOE_TILE, width), lambda i, d1, d2, nu: (i, 0)),
            scratch_shapes=[
                pltpu.SMEM((n_sorted_rows,), jnp.int32),
                pltpu.VMEM((2, MOE_TILE, width), hp.dtype),
                pltpu.SemaphoreType.DMA((2,)),
            ],
        ),
        out_shape=jax.ShapeDtypeStruct((n_sorted_rows, width), hp.dtype),
        compiler_params=_params(1),
        name="moe_dispatch",
    )(dest1, dest2, n_used, hp)


def _grouped_kernel(tile_run_ref, run_first_ref, run_expert_ref, n_runs_ref, n_used_ref, a_ref, *rest,
                    moe_layer, tn, swiglu):
    n_mats = 2 if swiglu else 1
    w_hbm = rest[:n_mats]
    o_ref, wbuf, sems = rest[n_mats:]
    j, i = pl.program_id(0), pl.program_id(1)
    n_runs = n_runs_ref[0]
    run = tile_run_ref[i]
    seq = j * n_runs + run
    slot = lax.rem(seq, 2)

    def fetches(jj, rr, slot_):
        expert = run_expert_ref[rr]
        col = pl.multiple_of(jj * tn, tn)
        return [pltpu.make_async_copy(w.at[moe_layer, expert, :, pl.ds(col, tn)], wbuf.at[slot_, m], sems.at[slot_, m])
                for m, w in enumerate(w_hbm)]

    live = i < n_used_ref[0]

    @pl.when(live & (i == run_first_ref[run]))
    def _():
        @pl.when(seq == 0)
        def _():
            for c in fetches(j, run, slot):
                c.start()

        for c in fetches(j, run, slot):
            c.wait()
        wraps = run + 1 == n_runs
        next_j = jnp.where(wraps, j + 1, j)
        next_run = jnp.where(wraps, 0, run + 1)

        @pl.when(next_j < pl.num_programs(0))
        def _():
            for c in fetches(next_j, next_run, 1 - slot):
                c.start()

    @pl.when(live)
    def _():
        if swiglu:
            h = _unpack_bf16_pairs(a_ref[...])
            for c0 in range(0, tn, SWIGLU_SUB):
                cols = slice(c0, c0 + SWIGLU_SUB)
                a = jnp.dot(h, wbuf[slot, 0, :, cols].astype(BF16), preferred_element_type=F32)
                b = jnp.dot(h, wbuf[slot, 1, :, cols].astype(BF16), preferred_element_type=F32)
                o_ref[:, cols] = (_silu(a) * b).astype(o_ref.dtype)
        else:
            o_ref[...] = jnp.dot(a_ref[...], wbuf[slot, 0].astype(BF16), preferred_element_type=F32)

    @pl.when(jnp.logical_not(live))
    def _():
        o_ref[...] = jnp.zeros_like(o_ref)


def _grouped_matmul(a, weights, moe_layer, plan, out_dtype, *, swiglu, name):
    n_rows, a_width = a.shape
    kdim, n = weights[0].shape[-2:]
    tn = 512
    n_prefetch = 5
    idx = lambda j, i, *_: (i, 0)
    return pl.pallas_call(
        functools.partial(_grouped_kernel, moe_layer=moe_layer, tn=tn, swiglu=swiglu),
        grid_spec=pltpu.PrefetchScalarGridSpec(
            num_scalar_prefetch=n_prefetch,
            grid=(n // tn, n_rows // MOE_TILE),
            in_specs=[pl.BlockSpec((MOE_TILE, a_width), idx)] + [pl.BlockSpec(memory_space=pl.ANY)] * len(weights),
            out_specs=pl.BlockSpec((MOE_TILE, tn), lambda j, i, *_: (i, j)),
            scratch_shapes=[
                pltpu.VMEM((2, len(weights), kdim, tn), F32),
                pltpu.SemaphoreType.DMA((2, len(weights))),
            ],
        ),
        out_shape=jax.ShapeDtypeStruct((n_rows, n), out_dtype),
        compiler_params=_params(2),
        name=name,
    )(*plan, a, *weights)


def _combine_kernel(dest1_ref, dest2_ref, x_ref, prob_ref, gate_ref, ys_ref, *rest, mode, first_tile):
    if mode == "final":
        g_ref, o_ref, buf, sems = rest
    else:
        g_ref, shift_ref, scale_ref, o_ref, h_ref, buf, sems = rest
    tm = x_ref.shape[0]
    i = pl.program_id(0)

    def row_copy(slot, k, t, d):
        return pltpu.make_async_copy(ys_ref.at[pl.ds(d, 1)], buf.at[slot, k, pl.ds(t, 1)], sems.at[slot, k])

    def start_tile(step):
        slot = lax.rem(step, 2)
        base = (step + first_tile) * tm

        def start(q, carry):
            for u in range(ROW_DMA_UNROLL):
                t = q * ROW_DMA_UNROLL + u
                row_copy(slot, 0, t, dest1_ref[base + t]).start(priority=0)
                row_copy(slot, 1, t, dest2_ref[base + t]).start(priority=1)
            return carry

        lax.fori_loop(0, tm // ROW_DMA_UNROLL, start, 0)

    @pl.when(i == 0)
    def _():
        start_tile(0)

    @pl.when(i + 1 < pl.num_programs(0))
    def _():
        start_tile(i + 1)

    slot = lax.rem(i, 2)

    def wait(t, carry):
        row_copy(slot, 0, 0, 0).wait()
        row_copy(slot, 1, 0, 0).wait()
        return carry

    lax.fori_loop(0, tm, wait, 0, unroll=2)
    p = prob_ref[...]
    f = p[:, 0:1] * buf[slot, 0] + p[:, 1:2] * buf[slot, 1]
    x = x_ref[...] + gate_ref[0] * f
    y = x * lax.rsqrt(jnp.mean(x * x, axis=-1, keepdims=True) + NORM_EPS) * g_ref[...]
    if mode == "final":
        o_ref[...] = y
    else:
        o_ref[...] = x
        h_ref[...] = (y * (1.0 + scale_ref[0]) + shift_ref[0]).astype(h_ref.dtype)


def _combine(lay, x_all, prob, ys, dest1, dest2, mod3, layer, k_gate, norm_g, *, final):
    d = x_all.shape[1]
    first_tile = 1 if final else 0
    tile = lambda i, d1, d2: (i + first_tile, 0)
    mod = lambda lyr, k: (lambda i, d1, d2: (_mod_index(lay, lyr, k, i + first_tile), 0, 0))
    row_spec = pl.BlockSpec((lay.tm, d), lambda i, d1, d2: (i, 0))
    in_specs = [
        pl.BlockSpec((lay.tm, d), tile),
        pl.BlockSpec((lay.tm, LANES), tile),
        pl.BlockSpec((1, 1, d), mod(layer, k_gate)),
        pl.BlockSpec(memory_space=pl.ANY),
        pl.BlockSpec((1, d), lambda i, d1, d2: (0, 0)),
    ]
    args = [x_all, prob, mod3, ys, norm_g.reshape(1, d)]
    n_tiles = lay.n_tiles - first_tile
    if final:
        out_specs = row_spec
        out_shape = jax.ShapeDtypeStruct((n_tiles * lay.tm, d), F32)
    else:
        in_specs += [pl.BlockSpec((1, 1, d), mod(layer + 1, 0)), pl.BlockSpec((1, 1, d), mod(layer + 1, 1))]
        args += [mod3, mod3]
        out_specs = [row_spec, row_spec]
        out_shape = [jax.ShapeDtypeStruct((lay.rows, d), F32), jax.ShapeDtypeStruct((lay.rows, d), BF16)]
    return pl.pallas_call(
        functools.partial(_combine_kernel, mode="final" if final else "next", first_tile=first_tile),
        grid_spec=pltpu.PrefetchScalarGridSpec(
            num_scalar_prefetch=2,
            grid=(n_tiles,),
            in_specs=in_specs,
            out_specs=out_specs,
            scratch_shapes=[pltpu.VMEM((2, 2, lay.tm, d), F32), pltpu.SemaphoreType.DMA((2, 2))],
        ),
        out_shape=out_shape,
        compiler_params=_params(1),
        name="moe_combine",
    )(dest1, dest2, *args)


def _moe_ffn(lay, x_all, norm_g, mod3, layer, router_w, router_b, w1, w3, w2, moe_layer, next_norm_g, *, final):
    hp, sel, prob, cnt = _norm_router(lay, x_all, norm_g, mod3, layer, 3, 4, router_w, router_b)
    n_sorted_rows = 2 * lay.rows + N_EXPERTS * MOE_TILE
    dest1, dest2, plan = _routing_plan(sel, cnt, n_sorted_rows)
    xs = _dispatch(hp, dest1, dest2, plan[-1], n_sorted_rows)
    act = _grouped_matmul(xs, (w1, w3), moe_layer, plan, BF16, swiglu=True, name="moe_up")
    ys = _grouped_matmul(act, (w2,), moe_layer, plan, F32, swiglu=False, name="moe_down")
    return _combine(lay, x_all, prob, ys, dest1, dest2, mod3, layer, 5, next_norm_g, final=final)


def _rope_tables(lay):
    t = np.arange(lay.seq)
    quarter = RET_QK_DIM // 4
    freqs = np.float32(ROPE_BASE) ** (-np.arange(quarter, dtype=np.float32) / np.float32(quarter))

    def tables(pos):
        ang = pos.astype(np.float32)[:, None] * freqs[None, :]
        cos, sin = np.cos(ang), np.sin(ang)
        return np.concatenate([cos, cos], axis=-1), np.concatenate([-sin, sin], axis=-1)

    cos_r, sin_r = tables(t // GRID_W)
    cos_c, sin_c = tables(t % GRID_W)
    cos_t = np.concatenate([np.ones((lay.tm, RET_QK_DIM), np.float32), np.concatenate([cos_r, cos_c], -1)], axis=0)
    sin_t = np.concatenate([np.zeros((lay.tm, RET_QK_DIM), np.float32), np.concatenate([sin_r, sin_c], -1)], axis=0)
    return jnp.asarray(cos_t, F32), jnp.asarray(sin_t, F32)


def kernel(x, c, ctx, c_ctx, w_mod, b_mod, norm1_g, norm2_g, w_in, sgu_ln_g, sgu_ln_b, sgu_w, sgu_b,
           ret_log_decay, w_proj_a, w_proj_r, w_out, ffn_w1, ffn_w3, ffn_w2, router_w, router_b,
           moe_w1, moe_w3, moe_w2, final_norm_g):
    batch, seq, d = x.shape
    ctx_len = ctx.shape[1]
    depth = w_mod.shape[0]
    lay = _Layout(batch, seq, ctx_len)

    cvec = jnp.zeros((MOD_ROWS, d), F32).at[:batch].set(c).at[batch].set(c_ctx)
    mod = _modulation(cvec, w_mod, b_mod)
    mod3 = mod.reshape(depth * MOD_ROWS * N_MOD, 1, d)

    cos_t, sin_t = _rope_tables(lay)
    x_all = jnp.concatenate([ctx.reshape(batch * ctx_len, d), x.reshape(batch * seq, d)], axis=0)

    log_decay = ret_log_decay.astype(F32)

    h = None
    out = None
    for layer in range(depth):
        if h is None:
            h = _norm_modulate(lay, x_all, norm1_g[layer], mod3, layer, 0, 1, BF16)
        qk = _in_proj(lay, h, w_in, layer, Q0, V0 - Q0, "rope", (cos_t, sin_t))
        v = _in_proj(lay, h, w_in, layer, V0, G0 - V0, "plain")
        g = _in_proj(lay, h, w_in, layer, G0, UA0 - G0, "silu")
        uv = _in_proj(lay, h, w_in, layer, UA0, GA0 - UA0, "gelu")
        gates = _in_proj(lay, h, w_in, layer, GA0, IN_COLS - GA0, "sigmoid")
        o_bwd = _retention_pass(lay, qk, v, log_decay[layer, 1], reverse=True)
        ret = _retention_pass(lay, qk, v, log_decay[layer, 0], o_bwd, g, reverse=False)
        sgu = _sgu(lay, uv, sgu_ln_g[layer], sgu_ln_b[layer], sgu_w[layer], sgu_b[layer].T)
        y = _merge(lay, sgu, ret, gates, w_proj_a, w_proj_r, layer)
        i = layer // 2
        last = layer == depth - 1
        h = None
        if layer % 2 == 0:
            x_all, h2 = _out_proj(lay, y, w_out, layer, x_all, mod3, 2, norm2_g[layer], 3, 4)
            act = _swiglu_up(lay, h2, ffn_w1, ffn_w3, (i,))
            x_all = _proj_resid(lay, act, ffn_w2, (i,), x_all, mod3, layer, 5)
        else:
            x_all = _out_proj(lay, y, w_out, layer, x_all, mod3, 2)
            res = _moe_ffn(lay, x_all, norm2_g[layer], mod3, layer, router_w[i], router_b[i], moe_w1, moe_w3, moe_w2,
                           i, final_norm_g if last else norm1_g[layer + 1], final=last)
            if last:
                out = res
            else:
                x_all, h = res

    if out is None:
        out = _final_norm(lay, x_all, final_norm_g)
    return out.reshape(batch, seq, d)
```

```python
import functools

import jax
import jax.numpy as jnp
import numpy as np
from jax import lax
from jax.experimental import pallas as pl
from jax.experimental.pallas import tpu as pltpu

F32 = jnp.float32
BF16 = jnp.bfloat16

GRID_W = 64
CHUNK = 128
SGU_GROUPS = 8
SGU_WIDTH = 2048
RET_HEADS = 8
RET_QK_DIM = 256
RET_V_DIM = 512
RET_QK = RET_HEADS * RET_QK_DIM
RET_V = RET_HEADS * RET_V_DIM
ROPE_BASE = 10000.0
N_EXPERTS = 8
N_MOD = 6
NORM_EPS = 1e-6
MOD_ROWS = 8
LANES = 128
MOE_TILE = 512
ZERO_FILL_WINDOW = 256
ROW_DMA_UNROLL = 4
SWIGLU_SUB = 256

VMEM_LIMIT_BYTES = 52 * 1024 * 1024

Q0 = 0
K0 = Q0 + RET_QK
V0 = K0 + RET_QK
G0 = V0 + RET_V
UA0 = G0 + RET_V
VA0 = UA0 + SGU_WIDTH
GA0 = VA0 + SGU_WIDTH
GR0 = GA0 + 2048
IN_COLS = GR0 + 2048


def _params(n_grid_dims):
    return pltpu.CompilerParams(
        dimension_semantics=("arbitrary",) * n_grid_dims,
        vmem_limit_bytes=VMEM_LIMIT_BYTES,
    )


def _gelu_tanh(x):
    c = 0.7978845608028654
    half_x = 0.5 * x
    return half_x * jnp.tanh(x * (c + (c * 0.044715) * (x * x))) + half_x


def _sigmoid(x):
    return 0.5 * jnp.tanh(0.5 * x) + 0.5


def _silu(x):
    return x * _sigmoid(x)


def _mod_kernel(c_ref, w_ref, b_ref, o_ref):
    sc = _silu(c_ref[...]).astype(BF16)
    o_ref[...] = jnp.dot(sc, w_ref[...].astype(BF16), preferred_element_type=F32) + b_ref[...]


def _modulation(cvec, w_mod, b_mod):
    depth, d, n = w_mod.shape
    tn = 1024
    return pl.pallas_call(
        _mod_kernel,
        grid=(depth, n // tn),
        in_specs=[
            pl.BlockSpec((MOD_ROWS, d), lambda l, j: (0, 0)),
            pl.BlockSpec((None, d, tn), lambda l, j: (l, 0, j)),
            pl.BlockSpec((None, 1, tn), lambda l, j: (l, 0, j)),
        ],
        out_specs=pl.BlockSpec((None, MOD_ROWS, tn), lambda l, j: (l, 0, j)),
        out_shape=jax.ShapeDtypeStruct((depth, MOD_ROWS, n), F32),
        compiler_params=_params(2),
        name="modulation",
    )(cvec, w_mod, b_mod.reshape(depth, 1, n))


def _norm_mod_kernel(x_ref, g_ref, shift_ref, scale_ref, o_ref):
    x = x_ref[...]
    y = x * lax.rsqrt(jnp.mean(x * x, axis=-1, keepdims=True) + NORM_EPS) * g_ref[...]
    o_ref[...] = (y * (1.0 + scale_ref[0]) + shift_ref[0]).astype(o_ref.dtype)


def _norm_kernel(x_ref, g_ref, o_ref):
    x = x_ref[...]
    y = x * lax.rsqrt(jnp.mean(x * x, axis=-1, keepdims=True) + NORM_EPS) * g_ref[...]
    o_ref[...] = y.astype(o_ref.dtype)


class _Layout:
    def __init__(self, batch, seq, ctx_len):
        self.batch, self.seq, self.ctx_len = batch, seq, ctx_len
        self.tm = batch * ctx_len
        assert seq % self.tm == 0 and ctx_len % CHUNK == 0 and seq % CHUNK == 0
        self.tiles_per_batch = seq // self.tm
        self.n_tiles = 1 + batch * self.tiles_per_batch
        self.rows = self.n_tiles * self.tm
        self.ctx_chunks = ctx_len // CHUNK
        self.lat_chunks = seq // CHUNK

    def mod_row(self, tile):
        return jnp.where(tile == 0, self.batch, (tile - 1) // self.tiles_per_batch)


def _mod_index(lay, layer, k, tile):
    return layer * MOD_ROWS * N_MOD + lay.mod_row(tile) * N_MOD + k


def _norm_modulate(lay, x_all, g, mod3, layer, k_shift, k_scale, out_dtype):
    d = x_all.shape[1]
    return pl.pallas_call(
        _norm_mod_kernel,
        grid=(lay.n_tiles,),
        in_specs=[
            pl.BlockSpec((lay.tm, d), lambda i: (i, 0)),
            pl.BlockSpec((1, d), lambda i: (0, 0)),
            pl.BlockSpec((1, 1, d), lambda i: (_mod_index(lay, layer, k_shift, i), 0, 0)),
            pl.BlockSpec((1, 1, d), lambda i: (_mod_index(lay, layer, k_scale, i), 0, 0)),
        ],
        out_specs=pl.BlockSpec((lay.tm, d), lambda i: (i, 0)),
        out_shape=jax.ShapeDtypeStruct((lay.rows, d), out_dtype),
        compiler_params=_params(1),
        name="norm_modulate",
    )(x_all, g.reshape(1, d), mod3, mod3)


def _final_norm(lay, x_all, g):
    d = x_all.shape[1]
    n_lat_tiles = lay.n_tiles - 1
    return pl.pallas_call(
        _norm_kernel,
        grid=(n_lat_tiles,),
        in_specs=[
            pl.BlockSpec((lay.tm, d), lambda i: (i + 1, 0)),
            pl.BlockSpec((1, d), lambda i: (0, 0)),
        ],
        out_specs=pl.BlockSpec((lay.tm, d), lambda i: (i, 0)),
        out_shape=jax.ShapeDtypeStruct((n_lat_tiles * lay.tm, d), F32),
        compiler_params=_params(1),
        name="final_norm",
    )(x_all, g.reshape(1, d))


IN_PROJ_TN = 2048
IN_PROJ_SUB = 512


def _in_proj_kernel(h_ref, w_ref, *rest, kind):
    o_ref = rest[-1]
    h = h_ref[...]
    if kind == "rope":
        cos_ref, sin_ref = rest[:2]
        scale = jnp.where(pl.program_id(0) == 0, 1.0, RET_QK_DIM ** -0.5).astype(F32)
    for c0 in range(0, IN_PROJ_TN, IN_PROJ_SUB):
        acc = jnp.dot(h, w_ref[:, c0:c0 + IN_PROJ_SUB].astype(BF16), preferred_element_type=F32)
        if kind == "rope":
            for s in range(0, IN_PROJ_SUB, LANES):
                t = (c0 + s) % RET_QK_DIM
                xs = acc[:, s:s + LANES]
                r = xs * cos_ref[:, t:t + LANES] + pltpu.roll(xs, LANES // 2, 1) * sin_ref[:, t:t + LANES]
                o_ref[:, c0 + s:c0 + s + LANES] = (r * scale).astype(o_ref.dtype)
        else:
            if kind == "silu":
                acc = _silu(acc)
            elif kind == "gelu":
                acc = _gelu_tanh(acc)
            elif kind == "sigmoid":
                acc = _sigmoid(acc)
            else:
                assert kind == "plain"
            o_ref[:, c0:c0 + IN_PROJ_SUB] = acc.astype(o_ref.dtype)


def _in_proj(lay, h, w_in, layer, col0, n_cols, kind, tables=()):
    d = h.shape[1]
    tn = IN_PROJ_TN
    table_spec = pl.BlockSpec((lay.tm, RET_QK_DIM),
                              lambda j, i: (jnp.where(i == 0, 0, 1 + lax.rem(i - 1, lay.tiles_per_batch)), 0))
    return pl.pallas_call(
        functools.partial(_in_proj_kernel, kind=kind),
        grid=(n_cols // tn, lay.n_tiles),
        in_specs=[
            pl.BlockSpec((lay.tm, d), lambda j, i: (i, 0)),
            pl.BlockSpec((None, d, tn), lambda j, i: (layer, 0, col0 // tn + j)),
        ] + [table_spec] * len(tables),
        out_specs=pl.BlockSpec((lay.tm, tn), lambda j, i: (i, j)),
        out_shape=jax.ShapeDtypeStruct((lay.rows, n_cols), BF16),
        compiler_params=_params(2),
        name="in_proj_" + kind,
    )(h, w_in, *tables)


RET_STEP_CHUNKS = 1
RET_CHUNK = 256


def _retention_kernel(lg_ref, q_ref, k_ref, v_ref, *rest, reverse, final):
    if final:
        other_ref, gate_ref, o_ref, s_ref = rest
    else:
        o_ref, s_ref = rest
    c = RET_CHUNK

    @pl.when(pl.program_id(1) == 0)
    def _():
        s_ref[...] = jnp.zeros_like(s_ref)

    qi = lax.broadcasted_iota(jnp.int32, (c, c), 0)
    kj = lax.broadcasted_iota(jnp.int32, (c, c), 1)
    diff = ((kj - qi) if reverse else (qi - kj)).astype(F32)
    pos = lax.broadcasted_iota(jnp.int32, (c, 1), 0).astype(F32)
    q_steps = (c - pos) if reverse else (pos + 1.0)
    k_steps = pos if reverse else (c - 1.0 - pos)

    subs = range(RET_STEP_CHUNKS)
    for h in range(RET_HEADS):
        log_g = lg_ref[h]
        intra = jnp.where(diff >= 0, jnp.exp(jnp.maximum(diff, 0.0) * log_g), 0.0)
        q_decay = jnp.exp(q_steps * log_g)
        k_decay = jnp.exp(k_steps * log_g)
        chunk_decay = jnp.exp(jnp.full((1, 1), float(c), F32) * log_g)
        for sub in (reversed(subs) if reverse else subs):
            rows = slice(sub * c, (sub + 1) * c)
            qh = q_ref[rows, h * RET_QK_DIM:(h + 1) * RET_QK_DIM]
            kh = k_ref[rows, h * RET_QK_DIM:(h + 1) * RET_QK_DIM]
            vh = v_ref[rows, h * RET_V_DIM:(h + 1) * RET_V_DIM]
            s = s_ref[h]

            scores = lax.dot_general(qh, kh, (((1,), (1,)), ((), ())), preferred_element_type=F32) * intra
            o = (jnp.dot(scores.astype(BF16), vh, preferred_element_type=F32)
                 + jnp.dot(qh, s.astype(BF16), preferred_element_type=F32) * q_decay)
            k_dec_t = (kh.astype(F32) * k_decay).T.astype(BF16)
            s_ref[h] = s * chunk_decay + jnp.dot(k_dec_t, vh, preferred_element_type=F32)

            sl = slice(h * RET_V_DIM, (h + 1) * RET_V_DIM)
            if final:
                o = o + other_ref[rows, sl].astype(F32)
                mu = jnp.mean(o, axis=-1, keepdims=True)
                oc = o - mu
                var = jnp.mean(oc * oc, axis=-1, keepdims=True)
                o = oc * lax.rsqrt(var + NORM_EPS) * gate_ref[rows, sl].astype(F32)
            o_ref[rows, sl] = o.astype(o_ref.dtype)


def _retention_pass(lay, qk, v, log_g, other=None, gate=None, *, reverse):
    final = other is not None
    step_rows = RET_STEP_CHUNKS * RET_CHUNK
    assert lay.ctx_len % step_rows == 0 and lay.seq % step_rows == 0
    ctx_steps, lat_steps = lay.ctx_len // step_rows, lay.seq // step_rows

    def row_block(b, n):
        ctx_blk = b * ctx_steps + ((ctx_steps - 1 - n) if reverse else n)
        m = n - ctx_steps
        lat_blk = lay.batch * ctx_steps + b * lat_steps + ((lat_steps - 1 - m) if reverse else m)
        return jnp.where(n < ctx_steps, ctx_blk, lat_blk)

    in_specs = [
        pl.BlockSpec((step_rows, RET_QK), lambda b, n, lg: (row_block(b, n), 0)),
        pl.BlockSpec((step_rows, RET_QK), lambda b, n, lg: (row_block(b, n), 1)),
        pl.BlockSpec((step_rows, RET_V), lambda b, n, lg: (row_block(b, n), 0)),
    ]
    args = [qk, qk, v]
    if final:
        in_specs += [pl.BlockSpec((step_rows, RET_V), lambda b, n, lg: (row_block(b, n), 0))] * 2
        args += [other, gate]
    return pl.pallas_call(
        functools.partial(_retention_kernel, reverse=reverse, final=final),
        grid_spec=pltpu.PrefetchScalarGridSpec(
            num_scalar_prefetch=1,
            grid=(lay.batch, ctx_steps + lat_steps),
            in_specs=in_specs,
            out_specs=pl.BlockSpec((step_rows, RET_V), lambda b, n, lg: (row_block(b, n), 0)),
            scratch_shapes=[pltpu.VMEM((RET_HEADS, RET_QK_DIM, RET_V_DIM), F32)],
        ),
        out_shape=jax.ShapeDtypeStruct((lay.rows, RET_V), BF16),
        compiler_params=_params(2),
        name="retention_fwd_merge" if final else "retention_bwd",
    )(log_g, *args)


def _sgu_kernel(u_ref, v_ref, lng_ref, lnb_ref, ws_ref, bst_ref, o_ref):
    gw = SGU_WIDTH // SGU_GROUPS
    for c0 in range(0, o_ref.shape[0], CHUNK):
        rows = slice(c0, c0 + CHUNK)
        v = v_ref[rows, :].astype(F32)
        mu = jnp.mean(v, axis=-1, keepdims=True)
        vc = v - mu
        var = jnp.mean(vc * vc, axis=-1, keepdims=True)
        vn = (vc * lax.rsqrt(var + NORM_EPS) * lng_ref[...] + lnb_ref[...]).astype(BF16)
        for g in range(SGU_GROUPS):
            sl = slice(g * gw, (g + 1) * gw)
            s = jnp.dot(ws_ref[g].astype(BF16), vn[:, sl], preferred_element_type=F32) + bst_ref[:, g:g + 1]
            o_ref[rows, sl] = (u_ref[rows, sl].astype(F32) * s).astype(o_ref.dtype)


def _sgu(lay, uv, ln_g, ln_b, w_s, b_s_t):
    return pl.pallas_call(
        _sgu_kernel,
        grid=(lay.n_tiles,),
        in_specs=[
            pl.BlockSpec((lay.tm, SGU_WIDTH), lambda i: (i, 0)),
            pl.BlockSpec((lay.tm, SGU_WIDTH), lambda i: (i, 1)),
            pl.BlockSpec((1, SGU_WIDTH), lambda i: (0, 0)),
            pl.BlockSpec((1, SGU_WIDTH), lambda i: (0, 0)),
            pl.BlockSpec((SGU_GROUPS, CHUNK, CHUNK), lambda i: (0, 0, 0)),
            pl.BlockSpec((CHUNK, SGU_GROUPS), lambda i: (0, 0)),
        ],
        out_specs=pl.BlockSpec((lay.tm, SGU_WIDTH), lambda i: (i, 0)),
        out_shape=jax.ShapeDtypeStruct((lay.rows, SGU_WIDTH), BF16),
        compiler_params=_params(1),
        name="sgu",
    )(uv, uv, ln_g.reshape(1, -1), ln_b.reshape(1, -1), w_s, b_s_t)


def _merge_kernel(sgu_ref, ret_ref, wa_ref, wr_ref, ga_ref, gr_ref, o_ref):
    a = jnp.dot(sgu_ref[...], wa_ref[...].astype(BF16), preferred_element_type=F32)
    r = jnp.dot(ret_ref[...], wr_ref[...].astype(BF16), preferred_element_type=F32)
    o_ref[...] = (ga_ref[...].astype(F32) * a + gr_ref[...].astype(F32) * r).astype(o_ref.dtype)


def _merge(lay, sgu, ret, gates, w_proj_a, w_proj_r, layer):
    n = w_proj_a.shape[2]
    tn = 512
    return pl.pallas_call(
        _merge_kernel,
        grid=(n // tn, lay.n_tiles),
        in_specs=[
            pl.BlockSpec((lay.tm, SGU_WIDTH), lambda j, i: (i, 0)),
            pl.BlockSpec((lay.tm, RET_V), lambda j, i: (i, 0)),
            pl.BlockSpec((None, SGU_WIDTH, tn), lambda j, i: (layer, 0, j)),
            pl.BlockSpec((None, RET_V, tn), lambda j, i: (layer, 0, j)),
            pl.BlockSpec((lay.tm, tn), lambda j, i: (i, j)),
            pl.BlockSpec((lay.tm, tn), lambda j, i: (i, n // tn + j)),
        ],
        out_specs=pl.BlockSpec((lay.tm, tn), lambda j, i: (i, j)),
        out_shape=jax.ShapeDtypeStruct((lay.rows, n), BF16),
        compiler_params=_params(2),
        name="merge",
    )(sgu, ret, w_proj_a, w_proj_r, gates, gates)


def _proj_resid_kernel(a_ref, w_ref, resid_ref, gate_ref, o_ref):
    acc = jnp.dot(a_ref[...], w_ref[...].astype(BF16), preferred_element_type=F32)
    o_ref[...] = resid_ref[...] + gate_ref[0] * acc


def _proj_resid(lay, a, w, w_index, resid, mod3, layer, k_gate):
    kdim = a.shape[1]
    n = w.shape[-1]
    tn = 512
    w_spec = pl.BlockSpec((None,) * len(w_index) + (kdim, tn), lambda j, i: (*w_index, 0, j))
    return pl.pallas_call(
        _proj_resid_kernel,
        grid=(n // tn, lay.n_tiles),
        in_specs=[
            pl.BlockSpec((lay.tm, kdim), lambda j, i: (i, 0)),
            w_spec,
            pl.BlockSpec((lay.tm, tn), lambda j, i: (i, j)),
            pl.BlockSpec((1, 1, tn), lambda j, i: (_mod_index(lay, layer, k_gate, i), 0, j)),
        ],
        out_specs=pl.BlockSpec((lay.tm, tn), lambda j, i: (i, j)),
        out_shape=jax.ShapeDtypeStruct((lay.rows, n), F32),
        compiler_params=_params(2),
        name="proj_resid",
    )(a, w, resid, mod3)


OUT_PROJ_SUB = 512


def _out_proj_kernel(a_ref, w_ref, resid_ref, gate_ref, *rest, with_norm):
    if with_norm:
        g_ref, shift_ref, scale_ref, o_ref, h_ref = rest
    else:
        (o_ref,) = rest
    a = a_ref[...]
    for c0 in range(0, o_ref.shape[1], OUT_PROJ_SUB):
        cols = slice(c0, c0 + OUT_PROJ_SUB)
        acc = jnp.dot(a, w_ref[:, cols].astype(BF16), preferred_element_type=F32)
        o_ref[:, cols] = resid_ref[:, cols] + gate_ref[0][:, cols] * acc
    if with_norm:
        x = o_ref[...]
        y = x * lax.rsqrt(jnp.mean(x * x, axis=-1, keepdims=True) + NORM_EPS) * g_ref[...]
        h_ref[...] = (y * (1.0 + scale_ref[0]) + shift_ref[0]).astype(h_ref.dtype)


def _out_proj(lay, a, w, layer, resid, mod3, k_gate, norm_g=None, k_shift=None, k_scale=None):
    kdim, n = w.shape[-2:]
    with_norm = norm_g is not None
    row = lambda width: pl.BlockSpec((lay.tm, width), lambda i: (i, 0))
    mod = lambda k: pl.BlockSpec((1, 1, n), lambda i: (_mod_index(lay, layer, k, i), 0, 0))
    in_specs = [
        row(kdim),
        pl.BlockSpec((None, kdim, n), lambda i: (layer, 0, 0), pipeline_mode=pl.Buffered(1)),
        row(n),
        mod(k_gate),
    ]
    args = [a, w, resid, mod3]
    x_shape = jax.ShapeDtypeStruct((lay.rows, n), F32)
    if with_norm:
        in_specs += [pl.BlockSpec((1, n), lambda i: (0, 0)), mod(k_shift), mod(k_scale)]
        args += [norm_g.reshape(1, n), mod3, mod3]
        out_specs, out_shape = [row(n), row(n)], [x_shape, jax.ShapeDtypeStruct((lay.rows, n), BF16)]
    else:
        out_specs, out_shape = row(n), x_shape
    return pl.pallas_call(
        functools.partial(_out_proj_kernel, with_norm=with_norm),
        grid=(lay.n_tiles,),
        in_specs=in_specs,
        out_specs=out_specs,
        out_shape=out_shape,
        compiler_params=_params(1),
        name="out_proj",
    )(*args)


def _swiglu_up_kernel(h_ref, w1_ref, w3_ref, o_ref):
    h = h_ref[...]
    for c0 in range(0, o_ref.shape[1], SWIGLU_SUB):
        cols = slice(c0, c0 + SWIGLU_SUB)
        a = jnp.dot(h, w1_ref[:, cols].astype(BF16), preferred_element_type=F32)
        b = jnp.dot(h, w3_ref[:, cols].astype(BF16), preferred_element_type=F32)
        o_ref[:, cols] = (_silu(a) * b).astype(o_ref.dtype)


def _swiglu_up(lay, h, w1, w3, w_index):
    d = h.shape[1]
    f = w1.shape[-1]
    tn = 512
    w_spec = pl.BlockSpec((None,) * len(w_index) + (d, tn), lambda j, i: (*w_index, 0, j))
    return pl.pallas_call(
        _swiglu_up_kernel,
        grid=(f // tn, lay.n_tiles),
        in_specs=[pl.BlockSpec((lay.tm, d), lambda j, i: (i, 0)), w_spec, w_spec],
        out_specs=pl.BlockSpec((lay.tm, tn), lambda j, i: (i, j)),
        out_shape=jax.ShapeDtypeStruct((lay.rows, f), BF16),
        compiler_params=_params(2),
        name="swiglu_up",
    )(h, w1, w3)


def _pack_bf16_pairs(y):
    half = y.shape[1] // 2
    lo = lax.bitcast_convert_type(y[:, :half].astype(BF16).astype(F32), jnp.uint32) >> 16
    hi = lax.bitcast_convert_type(y[:, half:].astype(BF16).astype(F32), jnp.uint32)
    return hi | lo


def _unpack_bf16_pairs(w):
    lo = lax.bitcast_convert_type(w << 16, F32).astype(BF16)
    hi = lax.bitcast_convert_type(w & jnp.uint32(0xFFFF0000), F32).astype(BF16)
    return jnp.concatenate([lo, hi], axis=1)


def _norm_router_kernel(x_ref, g_ref, shift_ref, scale_ref, rw_ref, rb_ref,
                        hp_ref, sel_ref, prob_ref, cnt_ref, run_ref):
    @pl.when(pl.program_id(0) == 0)
    def _():
        run_ref[...] = jnp.zeros_like(run_ref)

    x = x_ref[...]
    y = x * lax.rsqrt(jnp.mean(x * x, axis=-1, keepdims=True) + NORM_EPS) * g_ref[...]
    h = y * (1.0 + scale_ref[0]) + shift_ref[0]
    hp_ref[...] = _pack_bf16_pairs(h)

    logits = jnp.dot(h.astype(BF16), rw_ref[...], preferred_element_type=F32) + rb_ref[...]
    lane = lax.broadcasted_iota(jnp.int32, logits.shape, 1)
    neg = jnp.float32(-jnp.inf)
    lg = jnp.where(lane < N_EXPERTS, logits, neg)
    m1 = jnp.max(lg, axis=-1, keepdims=True)
    i1 = jnp.min(jnp.where(lg == m1, lane, LANES), axis=-1, keepdims=True)
    lg2 = jnp.where(lane == i1, neg, lg)
    m2 = jnp.max(lg2, axis=-1, keepdims=True)
    i2 = jnp.min(jnp.where(lg2 == m2, lane, LANES), axis=-1, keepdims=True)
    e2 = jnp.exp(m2 - m1)
    den = 1.0 + e2

    tm = x.shape[0]
    onehot = jnp.where(lane == i1, 1.0, 0.0) + jnp.where(lane == i2, 1.0, 0.0)
    earlier = jnp.where(lax.broadcasted_iota(jnp.int32, (tm, tm), 0) > lax.broadcasted_iota(jnp.int32, (tm, tm), 1),
                        1.0, 0.0).astype(BF16)
    before = jnp.dot(earlier, onehot.astype(BF16), preferred_element_type=F32) + run_ref[0:1, :]
    pos1 = jnp.sum(jnp.where(lane == i1, before, 0.0), axis=-1, keepdims=True).astype(jnp.int32)
    pos2 = jnp.sum(jnp.where(lane == i2, before, 0.0), axis=-1, keepdims=True).astype(jnp.int32)
    run_ref[...] = run_ref[...] + jnp.sum(onehot, axis=0, keepdims=True)
    cnt_ref[...] = run_ref[...]

    sel_ref[...] = jnp.where(lane == 0, i1, jnp.where(lane == 1, i2, jnp.where(lane == 2, pos1,
                                                                                jnp.where(lane == 3, pos2, 0))))
    prob_ref[...] = jnp.where(lane == 0, 1.0 / den, jnp.where(lane == 1, e2 / den, 0.0))


def _norm_router(lay, x_all, g, mod3, layer, k_shift, k_scale, router_w, router_b):
    d = x_all.shape[1]
    w = jnp.zeros((d, LANES), BF16).at[:, :N_EXPERTS].set(router_w.astype(BF16))
    b = jnp.zeros((1, LANES), F32).at[0, :N_EXPERTS].set(router_b.astype(F32))
    lane_block = pl.BlockSpec((lay.tm, LANES), lambda i: (i, 0))
    return pl.pallas_call(
        _norm_router_kernel,
        grid=(lay.n_tiles,),
        in_specs=[
            pl.BlockSpec((lay.tm, d), lambda i: (i, 0)),
            pl.BlockSpec((1, d), lambda i: (0, 0)),
            pl.BlockSpec((1, 1, d), lambda i: (_mod_index(lay, layer, k_shift, i), 0, 0)),
            pl.BlockSpec((1, 1, d), lambda i: (_mod_index(lay, layer, k_scale, i), 0, 0)),
            pl.BlockSpec((d, LANES), lambda i: (0, 0)),
            pl.BlockSpec((1, LANES), lambda i: (0, 0)),
        ],
        out_specs=[
            pl.BlockSpec((lay.tm, d // 2), lambda i: (i, 0)),
            lane_block,
            lane_block,
            pl.BlockSpec((8, LANES), lambda i: (0, 0)),
        ],
        out_shape=[
            jax.ShapeDtypeStruct((lay.rows, d // 2), jnp.uint32),
            jax.ShapeDtypeStruct((lay.rows, LANES), jnp.int32),
            jax.ShapeDtypeStruct((lay.rows, LANES), F32),
            jax.ShapeDtypeStruct((8, LANES), F32),
        ],
        scratch_shapes=[pltpu.VMEM((8, LANES), F32)],
        compiler_params=_params(1),
        name="norm_router",
    )(x_all, g.reshape(1, d), mod3, mod3, w, b)


def _routing_plan(sel, cnt, n_sorted_rows):
    counts = cnt[0, :N_EXPERTS].astype(jnp.int32)
    padded = (counts + MOE_TILE - 1) // MOE_TILE * MOE_TILE
    ends = jnp.cumsum(padded)
    starts = ends - padded
    dest1 = starts[sel[:, 0]] + sel[:, 2]
    dest2 = starts[sel[:, 1]] + sel[:, 3]
    tile_start = jnp.arange(n_sorted_rows // MOE_TILE, dtype=jnp.int32) * MOE_TILE
    tile_expert = jnp.minimum(jnp.sum(tile_start[:, None] >= ends[None, :], axis=1), N_EXPERTS - 1)
    n_used_tiles = (ends[-1:] // MOE_TILE).astype(jnp.int32)
    present = counts > 0
    run_expert = jnp.argsort(jnp.logical_not(present), stable=True).astype(jnp.int32)
    run_of_expert = jnp.cumsum(present.astype(jnp.int32)) - 1
    tile_run = jnp.maximum(run_of_expert[tile_expert], 0).astype(jnp.int32)
    run_first_tile = (starts[run_expert] // MOE_TILE).astype(jnp.int32)
    n_runs = jnp.sum(present.astype(jnp.int32)).reshape(1)
    fill_lo = jnp.concatenate([starts + counts, ends[-1:]]).astype(jnp.int32)
    fill_hi = jnp.concatenate([ends, jnp.full((1,), n_sorted_rows, jnp.int32)]).astype(jnp.int32)
    return dest1, dest2, (fill_lo, fill_hi), (tile_run, run_first_tile, run_expert, n_runs, n_used_tiles)


def _dispatch_kernel(dest1_ref, dest2_ref, fill_lo_ref, fill_hi_ref, h_ref, xs_ref, zero_ref, sem, zero_sem):
    i = pl.program_id(0)
    tm = h_ref.shape[0]
    base = i * tm

    def row_copy(t, d):
        return pltpu.make_async_copy(h_ref.at[pl.ds(t, 1)], xs_ref.at[pl.ds(d, 1)], sem)

    def zero_copy(d):
        return pltpu.make_async_copy(zero_ref, xs_ref.at[pl.ds(d, 1)], zero_sem)

    @pl.when(i == 0)
    def _():
        zero_ref[...] = jnp.zeros_like(zero_ref)
        for r in range(N_EXPERTS + 1):
            lo, hi = fill_lo_ref[r], fill_hi_ref[r]

            def start_zero(d, carry, lo=lo):
                zero_copy(d).start()

                @pl.when(d - lo >= ZERO_FILL_WINDOW)
                def _():
                    zero_copy(0).wait()

                return carry

            def wait_zero(d, carry):
                zero_copy(0).wait()
                return carry

            lax.fori_loop(lo, hi, start_zero, 0)
            lax.fori_loop(0, jnp.minimum(hi - lo, ZERO_FILL_WINDOW), wait_zero, 0)

    def start(q, carry):
        for u in range(ROW_DMA_UNROLL):
            t = q * ROW_DMA_UNROLL + u
            row_copy(t, dest1_ref[base + t]).start(priority=0)
            row_copy(t, dest2_ref[base + t]).start(priority=1)
        return carry

    def wait(q, carry):
        for _ in range(2 * ROW_DMA_UNROLL):
            row_copy(0, 0).wait()
        return carry

    lax.fori_loop(0, tm // ROW_DMA_UNROLL, start, 0)
    lax.fori_loop(0, tm // ROW_DMA_UNROLL, wait, 0)


def _dispatch(lay, hp, dest1, dest2, fill_lo, fill_hi, n_sorted_rows):
    width = hp.shape[1]
    return pl.pallas_call(
        _dispatch_kernel,
        grid_spec=pltpu.PrefetchScalarGridSpec(
            num_scalar_prefetch=4,
            grid=(lay.n_tiles,),
            in_specs=[pl.BlockSpec((lay.tm, width), lambda i, *_: (i, 0))],
            out_specs=pl.BlockSpec(memory_space=pl.ANY),
            scratch_shapes=[
                pltpu.VMEM((1, width), hp.dtype),
                pltpu.SemaphoreType.DMA(()),
                pltpu.SemaphoreType.DMA(()),
            ],
        ),
        out_shape=jax.ShapeDtypeStruct((n_sorted_rows, width), hp.dtype),
        compiler_params=_params(1),
        name="moe_dispatch",
    )(dest1, dest2, fill_lo, fill_hi, hp)


def _grouped_kernel(tile_run_ref, run_first_ref, run_expert_ref, n_runs_ref, n_used_ref, a_ref, *rest,
                    moe_layer, tn, swiglu):
    n_mats = 2 if swiglu else 1
    w_hbm = rest[:n_mats]
    o_ref, wbuf, sems = rest[n_mats:]
    j, i = pl.program_id(0), pl.program_id(1)
    n_runs = n_runs_ref[0]
    run = tile_run_ref[i]
    seq = j * n_runs + run
    slot = lax.rem(seq, 2)

    def fetches(jj, rr, slot_):
        expert = run_expert_ref[rr]
        col = pl.multiple_of(jj * tn, tn)
        return [pltpu.make_async_copy(w.at[moe_layer, expert, :, pl.ds(col, tn)], wbuf.at[slot_, m], sems.at[slot_, m])
                for m, w in enumerate(w_hbm)]

    live = i < n_used_ref[0]

    @pl.when(live & (i == run_first_ref[run]))
    def _():
        @pl.when(seq == 0)
        def _():
            for c in fetches(j, run, slot):
                c.start()

        for c in fetches(j, run, slot):
            c.wait()
        wraps = run + 1 == n_runs
        next_j = jnp.where(wraps, j + 1, j)
        next_run = jnp.where(wraps, 0, run + 1)

        @pl.when(next_j < pl.num_programs(0))
        def _():
            for c in fetches(next_j, next_run, 1 - slot):
                c.start()

    @pl.when(live)
    def _():
        if swiglu:
            h = _unpack_bf16_pairs(a_ref[...])
            for c0 in range(0, tn, SWIGLU_SUB):
                cols = slice(c0, c0 + SWIGLU_SUB)
                a = jnp.dot(h, wbuf[slot, 0, :, cols].astype(BF16), preferred_element_type=F32)
                b = jnp.dot(h, wbuf[slot, 1, :, cols].astype(BF16), preferred_element_type=F32)
                o_ref[:, cols] = (_silu(a) * b).astype(o_ref.dtype)
        else:
            o_ref[...] = jnp.dot(a_ref[...], wbuf[slot, 0].astype(BF16), preferred_element_type=F32)

    @pl.when(jnp.logical_not(live))
    def _():
        o_ref[...] = jnp.zeros_like(o_ref)


def _grouped_matmul(a, weights, moe_layer, plan, out_dtype, *, swiglu, name):
    n_rows, a_width = a.shape
    kdim, n = weights[0].shape[-2:]
    tn = 512
    n_prefetch = 5
    idx = lambda j, i, *_: (i, 0)
    return pl.pallas_call(
        functools.partial(_grouped_kernel, moe_layer=moe_layer, tn=tn, swiglu=swiglu),
        grid_spec=pltpu.PrefetchScalarGridSpec(
            num_scalar_prefetch=n_prefetch,
            grid=(n // tn, n_rows // MOE_TILE),
            in_specs=[pl.BlockSpec((MOE_TILE, a_width), idx)] + [pl.BlockSpec(memory_space=pl.ANY)] * len(weights),
            out_specs=pl.BlockSpec((MOE_TILE, tn), lambda j, i, *_: (i, j)),
            scratch_shapes=[
                pltpu.VMEM((2, len(weights), kdim, tn), F32),
                pltpu.SemaphoreType.DMA((2, len(weights))),
            ],
        ),
        out_shape=jax.ShapeDtypeStruct((n_rows, n), out_dtype),
        compiler_params=_params(2),
        name=name,
    )(*plan, a, *weights)


def _combine_kernel(dest1_ref, dest2_ref, x_ref, prob_ref, gate_ref, ys_ref, *rest, mode, first_tile):
    if mode == "final":
        g_ref, o_ref, buf, sems = rest
    else:
        g_ref, shift_ref, scale_ref, o_ref, h_ref, buf, sems = rest
    tm = x_ref.shape[0]
    i = pl.program_id(0)

    def row_copy(slot, k, t, d):
        return pltpu.make_async_copy(ys_ref.at[pl.ds(d, 1)], buf.at[slot, k, pl.ds(t, 1)], sems.at[slot, k])

    def start_tile(step):
        slot = lax.rem(step, 2)
        base = (step + first_tile) * tm

        def start(q, carry):
            for u in range(ROW_DMA_UNROLL):
                t = q * ROW_DMA_UNROLL + u
                row_copy(slot, 0, t, dest1_ref[base + t]).start(priority=0)
                row_copy(slot, 1, t, dest2_ref[base + t]).start(priority=1)
            return carry

        lax.fori_loop(0, tm // ROW_DMA_UNROLL, start, 0)

    @pl.when(i == 0)
    def _():
        start_tile(0)

    @pl.when(i + 1 < pl.num_programs(0))
    def _():
        start_tile(i + 1)

    slot = lax.rem(i, 2)

    def wait(t, carry):
        row_copy(slot, 0, 0, 0).wait()
        row_copy(slot, 1, 0, 0).wait()
        return carry

    lax.fori_loop(0, tm, wait, 0, unroll=2)
    p = prob_ref[...]
    f = p[:, 0:1] * buf[slot, 0] + p[:, 1:2] * buf[slot, 1]
    x = x_ref[...] + gate_ref[0] * f
    y = x * lax.rsqrt(jnp.mean(x * x, axis=-1, keepdims=True) + NORM_EPS) * g_ref[...]
    if mode == "final":
        o_ref[...] = y
    else:
        o_ref[...] = x
        h_ref[...] = (y * (1.0 + scale_ref[0]) + shift_ref[0]).astype(h_ref.dtype)


def _combine(lay, x_all, prob, ys, dest1, dest2, mod3, layer, k_gate, norm_g, *, final):
    d = x_all.shape[1]
    first_tile = 1 if final else 0
    tile = lambda i, d1, d2: (i + first_tile, 0)
    mod = lambda lyr, k: (lambda i, d1, d2: (_mod_index(lay, lyr, k, i + first_tile), 0, 0))
    row_spec = pl.BlockSpec((lay.tm, d), lambda i, d1, d2: (i, 0))
    in_specs = [
        pl.BlockSpec((lay.tm, d), tile),
        pl.BlockSpec((lay.tm, LANES), tile),
        pl.BlockSpec((1, 1, d), mod(layer, k_gate)),
        pl.BlockSpec(memory_space=pl.ANY),
        pl.BlockSpec((1, d), lambda i, d1, d2: (0, 0)),
    ]
    args = [x_all, prob, mod3, ys, norm_g.reshape(1, d)]
    n_tiles = lay.n_tiles - first_tile
    if final:
        out_specs = row_spec
        out_shape = jax.ShapeDtypeStruct((n_tiles * lay.tm, d), F32)
    else:
        in_specs += [pl.BlockSpec((1, 1, d), mod(layer + 1, 0)), pl.BlockSpec((1, 1, d), mod(layer + 1, 1))]
        args += [mod3, mod3]
        out_specs = [row_spec, row_spec]
        out_shape = [jax.ShapeDtypeStruct((lay.rows, d), F32), jax.ShapeDtypeStruct((lay.rows, d), BF16)]
    return pl.pallas_call(
        functools.partial(_combine_kernel, mode="final" if final else "next", first_tile=first_tile),
        grid_spec=pltpu.PrefetchScalarGridSpec(
            num_scalar_prefetch=2,
            grid=(n_tiles,),
            in_specs=in_specs,
            out_specs=out_specs,
            scratch_shapes=[pltpu.VMEM((2, 2, lay.tm, d), F32), pltpu.SemaphoreType.DMA((2, 2))],
        ),
        out_shape=out_shape,
        compiler_params=_params(1),
        name="moe_combine",
    )(dest1, dest2, *args)


def _moe_ffn(lay, x_all, norm_g, mod3, layer, router_w, router_b, w1, w3, w2, moe_layer, next_norm_g, *, final):
    hp, sel, prob, cnt = _norm_router(lay, x_all, norm_g, mod3, layer, 3, 4, router_w, router_b)
    n_sorted_rows = 2 * lay.rows + N_EXPERTS * MOE_TILE
    dest1, dest2, fill, plan = _routing_plan(sel, cnt, n_sorted_rows)
    xs = _dispatch(lay, hp, dest1, dest2, *fill, n_sorted_rows)
    act = _grouped_matmul(xs, (w1, w3), moe_layer, plan, BF16, swiglu=True, name="moe_up")
    ys = _grouped_matmul(act, (w2,), moe_layer, plan, F32, swiglu=False, name="moe_down")
    return _combine(lay, x_all, prob, ys, dest1, dest2, mod3, layer, 5, next_norm_g, final=final)


def _rope_tables(lay):
    t = np.arange(lay.seq)
    quarter = RET_QK_DIM // 4
    freqs = np.float32(ROPE_BASE) ** (-np.arange(quarter, dtype=np.float32) / np.float32(quarter))

    def tables(pos):
        ang = pos.astype(np.float32)[:, None] * freqs[None, :]
        cos, sin = np.cos(ang), np.sin(ang)
        return np.concatenate([cos, cos], axis=-1), np.concatenate([-sin, sin], axis=-1)

    cos_r, sin_r = tables(t // GRID_W)
    cos_c, sin_c = tables(t % GRID_W)
    cos_t = np.concatenate([np.ones((lay.tm, RET_QK_DIM), np.float32), np.concatenate([cos_r, cos_c], -1)], axis=0)
    sin_t = np.concatenate([np.zeros((lay.tm, RET_QK_DIM), np.float32), np.concatenate([sin_r, sin_c], -1)], axis=0)
    return jnp.asarray(cos_t, F32), jnp.asarray(sin_t, F32)


def kernel(x, c, ctx, c_ctx, w_mod, b_mod, norm1_g, norm2_g, w_in, sgu_ln_g, sgu_ln_b, sgu_w, sgu_b,
           ret_log_decay, w_proj_a, w_proj_r, w_out, ffn_w1, ffn_w3, ffn_w2, router_w, router_b,
           moe_w1, moe_w3, moe_w2, final_norm_g):
    batch, seq, d = x.shape
    ctx_len = ctx.shape[1]
    depth = w_mod.shape[0]
    lay = _Layout(batch, seq, ctx_len)

    cvec = jnp.zeros((MOD_ROWS, d), F32).at[:batch].set(c).at[batch].set(c_ctx)
    mod = _modulation(cvec, w_mod, b_mod)
    mod3 = mod.reshape(depth * MOD_ROWS * N_MOD, 1, d)

    cos_t, sin_t = _rope_tables(lay)
    x_all = jnp.concatenate([ctx.reshape(batch * ctx_len, d), x.reshape(batch * seq, d)], axis=0)

    log_decay = ret_log_decay.astype(F32)

    h = None
    out = None
    for layer in range(depth):
        if h is None:
            h = _norm_modulate(lay, x_all, norm1_g[layer], mod3, layer, 0, 1, BF16)
        qk = _in_proj(lay, h, w_in, layer, Q0, V0 - Q0, "rope", (cos_t, sin_t))
        v = _in_proj(lay, h, w_in, layer, V0, G0 - V0, "plain")
        g = _in_proj(lay, h, w_in, layer, G0, UA0 - G0, "silu")
        uv = _in_proj(lay, h, w_in, layer, UA0, GA0 - UA0, "gelu")
        gates = _in_proj(lay, h, w_in, layer, GA0, IN_COLS - GA0, "sigmoid")
        o_bwd = _retention_pass(lay, qk, v, log_decay[layer, 1], reverse=True)
        ret = _retention_pass(lay, qk, v, log_decay[layer, 0], o_bwd, g, reverse=False)
        sgu = _sgu(lay, uv, sgu_ln_g[layer], sgu_ln_b[layer], sgu_w[layer], sgu_b[layer].T)
        y = _merge(lay, sgu, ret, gates, w_proj_a, w_proj_r, layer)
        i = layer // 2
        last = layer == depth - 1
        h = None
        if layer % 2 == 0:
            x_all, h2 = _out_proj(lay, y, w_out, layer, x_all, mod3, 2, norm2_g[layer], 3, 4)
            act = _swiglu_up(lay, h2, ffn_w1, ffn_w3, (i,))
            x_all = _proj_resid(lay, act, ffn_w2, (i,), x_all, mod3, layer, 5)
        else:
            x_all = _out_proj(lay, y, w_out, layer, x_all, mod3, 2)
            res = _moe_ffn(lay, x_all, norm2_g[layer], mod3, layer, router_w[i], router_b[i], moe_w1, moe_w3, moe_w2,
                           i, final_norm_g if last else norm1_g[layer + 1], final=last)
            if last:
                out = res
            else:
                x_all, h = res

    if out is None:
        out = _final_norm(lay, x_all, final_norm_g)
    return out.reshape(batch, seq, d)
```

```python
import functools

import jax
import jax.numpy as jnp
import numpy as np
from jax import lax
from jax.experimental import pallas as pl
from jax.experimental.pallas import tpu as pltpu

F32 = jnp.float32
BF16 = jnp.bfloat16

GRID_W = 64
CHUNK = 128
SGU_GROUPS = 8
SGU_WIDTH = 2048
RET_HEADS = 8
RET_QK_DIM = 256
RET_V_DIM = 512
RET_QK = RET_HEADS * RET_QK_DIM
RET_V = RET_HEADS * RET_V_DIM
ROPE_BASE = 10000.0
N_EXPERTS = 8
N_MOD = 6
NORM_EPS = 1e-6
MOD_ROWS = 8
LANES = 128
MOE_TILE = 512
ZERO_FILL_WINDOW = 256
ROW_DMA_UNROLL = 4
SWIGLU_SUB = 256

VMEM_LIMIT_BYTES = 52 * 1024 * 1024

Q0 = 0
K0 = Q0 + RET_QK
V0 = K0 + RET_QK
G0 = V0 + RET_V
UA0 = G0 + RET_V
VA0 = UA0 + SGU_WIDTH
GA0 = VA0 + SGU_WIDTH
GR0 = GA0 + 2048
IN_COLS = GR0 + 2048


def _params(n_grid_dims):
    return pltpu.CompilerParams(
        dimension_semantics=("arbitrary",) * n_grid_dims,
        vmem_limit_bytes=VMEM_LIMIT_BYTES,
    )


def _gelu_tanh(x):
    c = 0.7978845608028654
    half_x = 0.5 * x
    return half_x * jnp.tanh(x * (c + (c * 0.044715) * (x * x))) + half_x


def _sigmoid(x):
    return 0.5 * jnp.tanh(0.5 * x) + 0.5


def _silu(x):
    return x * _sigmoid(x)


def _mod_kernel(c_ref, w_ref, b_ref, o_ref):
    sc = _silu(c_ref[...]).astype(BF16)
    o_ref[...] = jnp.dot(sc, w_ref[...].astype(BF16), preferred_element_type=F32) + b_ref[...]


def _modulation(cvec, w_mod, b_mod):
    depth, d, n = w_mod.shape
    tn = 1024
    return pl.pallas_call(
        _mod_kernel,
        grid=(depth, n // tn),
        in_specs=[
            pl.BlockSpec((MOD_ROWS, d), lambda l, j: (0, 0)),
            pl.BlockSpec((None, d, tn), lambda l, j: (l, 0, j)),
            pl.BlockSpec((None, 1, tn), lambda l, j: (l, 0, j)),
        ],
        out_specs=pl.BlockSpec((None, MOD_ROWS, tn), lambda l, j: (l, 0, j)),
        out_shape=jax.ShapeDtypeStruct((depth, MOD_ROWS, n), F32),
        compiler_params=_params(2),
        name="modulation",
    )(cvec, w_mod, b_mod.reshape(depth, 1, n))


def _norm_mod_kernel(x_ref, g_ref, shift_ref, scale_ref, o_ref):
    x = x_ref[...]
    y = x * lax.rsqrt(jnp.mean(x * x, axis=-1, keepdims=True) + NORM_EPS) * g_ref[...]
    o_ref[...] = (y * (1.0 + scale_ref[0]) + shift_ref[0]).astype(o_ref.dtype)


def _norm_kernel(x_ref, g_ref, o_ref):
    x = x_ref[...]
    y = x * lax.rsqrt(jnp.mean(x * x, axis=-1, keepdims=True) + NORM_EPS) * g_ref[...]
    o_ref[...] = y.astype(o_ref.dtype)


class _Layout:
    def __init__(self, batch, seq, ctx_len):
        self.batch, self.seq, self.ctx_len = batch, seq, ctx_len
        self.tm = batch * ctx_len
        assert seq % self.tm == 0 and ctx_len % CHUNK == 0 and seq % CHUNK == 0
        self.tiles_per_batch = seq // self.tm
        self.n_tiles = 1 + batch * self.tiles_per_batch
        self.rows = self.n_tiles * self.tm
        self.ctx_chunks = ctx_len // CHUNK
        self.lat_chunks = seq // CHUNK

    def mod_row(self, tile):
        return jnp.where(tile == 0, self.batch, (tile - 1) // self.tiles_per_batch)


def _mod_index(lay, layer, k, tile):
    return layer * MOD_ROWS * N_MOD + lay.mod_row(tile) * N_MOD + k


def _norm_modulate(lay, x_all, g, mod3, layer, k_shift, k_scale, out_dtype):
    d = x_all.shape[1]
    return pl.pallas_call(
        _norm_mod_kernel,
        grid=(lay.n_tiles,),
        in_specs=[
            pl.BlockSpec((lay.tm, d), lambda i: (i, 0)),
            pl.BlockSpec((1, d), lambda i: (0, 0)),
            pl.BlockSpec((1, 1, d), lambda i: (_mod_index(lay, layer, k_shift, i), 0, 0)),
            pl.BlockSpec((1, 1, d), lambda i: (_mod_index(lay, layer, k_scale, i), 0, 0)),
        ],
        out_specs=pl.BlockSpec((lay.tm, d), lambda i: (i, 0)),
        out_shape=jax.ShapeDtypeStruct((lay.rows, d), out_dtype),
        compiler_params=_params(1),
        name="norm_modulate",
    )(x_all, g.reshape(1, d), mod3, mod3)


def _join_norm_kernel(ctx_ref, x_ref, g_ref, shift_ref, scale_ref, xall_ref, h_ref):
    def emit(src_ref):
        x = src_ref[...]
        xall_ref[...] = x
        y = x * lax.rsqrt(jnp.mean(x * x, axis=-1, keepdims=True) + NORM_EPS) * g_ref[...]
        h_ref[...] = (y * (1.0 + scale_ref[0]) + shift_ref[0]).astype(h_ref.dtype)

    @pl.when(pl.program_id(0) == 0)
    def _():
        emit(ctx_ref)

    @pl.when(pl.program_id(0) > 0)
    def _():
        emit(x_ref)


def _join_norm_modulate(lay, ctx2d, x2d, g, mod3, layer, k_shift, k_scale):
    d = x2d.shape[1]
    row = pl.BlockSpec((lay.tm, d), lambda i: (i, 0))
    return pl.pallas_call(
        _join_norm_kernel,
        grid=(lay.n_tiles,),
        in_specs=[
            pl.BlockSpec((lay.tm, d), lambda i: (0, 0)),
            pl.BlockSpec((lay.tm, d), lambda i: (jnp.maximum(i - 1, 0), 0)),
            pl.BlockSpec((1, d), lambda i: (0, 0)),
            pl.BlockSpec((1, 1, d), lambda i: (_mod_index(lay, layer, k_shift, i), 0, 0)),
            pl.BlockSpec((1, 1, d), lambda i: (_mod_index(lay, layer, k_scale, i), 0, 0)),
        ],
        out_specs=[row, row],
        out_shape=[jax.ShapeDtypeStruct((lay.rows, d), F32), jax.ShapeDtypeStruct((lay.rows, d), BF16)],
        compiler_params=_params(1),
        name="join_norm_modulate",
    )(ctx2d, x2d, g.reshape(1, d), mod3, mod3)


def _final_norm(lay, x_all, g):
    d = x_all.shape[1]
    n_lat_tiles = lay.n_tiles - 1
    return pl.pallas_call(
        _norm_kernel,
        grid=(n_lat_tiles,),
        in_specs=[
            pl.BlockSpec((lay.tm, d), lambda i: (i + 1, 0)),
            pl.BlockSpec((1, d), lambda i: (0, 0)),
        ],
        out_specs=pl.BlockSpec((lay.tm, d), lambda i: (i, 0)),
        out_shape=jax.ShapeDtypeStruct((n_lat_tiles * lay.tm, d), F32),
        compiler_params=_params(1),
        name="final_norm",
    )(x_all, g.reshape(1, d))


IN_PROJ_TN = 2048
IN_PROJ_SUB = 512


IN_PROJ_KINDS = ((V0, "rope"), (G0, "plain"), (UA0, "silu"), (GA0, "gelu"), (IN_COLS, "sigmoid"))


def _in_proj_kernel(h_ref, w_ref, cos_ref, sin_ref, o_ref):
    col = pl.program_id(0) * IN_PROJ_TN
    start = 0
    for end, kind in IN_PROJ_KINDS:
        @pl.when((col >= start) & (col < end))
        def _(kind=kind):
            _in_proj_tile(h_ref, w_ref, cos_ref, sin_ref, o_ref, kind)

        start = end


def _in_proj_tile(h_ref, w_ref, cos_ref, sin_ref, o_ref, kind):
    h = h_ref[...]
    if kind == "rope":
        scale = jnp.where(pl.program_id(0) == 0, 1.0, RET_QK_DIM ** -0.5).astype(F32)
    for c0 in range(0, IN_PROJ_TN, IN_PROJ_SUB):
        acc = jnp.dot(h, w_ref[:, c0:c0 + IN_PROJ_SUB].astype(BF16), preferred_element_type=F32)
        if kind == "rope":
            for s in range(0, IN_PROJ_SUB, LANES):
                t = (c0 + s) % RET_QK_DIM
                xs = acc[:, s:s + LANES]
                r = xs * cos_ref[:, t:t + LANES] + pltpu.roll(xs, LANES // 2, 1) * sin_ref[:, t:t + LANES]
                o_ref[:, c0 + s:c0 + s + LANES] = (r * scale).astype(o_ref.dtype)
        else:
            if kind == "silu":
                acc = _silu(acc)
            elif kind == "gelu":
                acc = _gelu_tanh(acc)
            elif kind == "sigmoid":
                acc = _sigmoid(acc)
            else:
                assert kind == "plain"
            o_ref[:, c0:c0 + IN_PROJ_SUB] = acc.astype(o_ref.dtype)


def _in_proj(lay, h, w_in, layer, cos_t, sin_t):
    d = h.shape[1]
    n = w_in.shape[2]
    tn = IN_PROJ_TN
    n_rope_tiles = V0 // tn

    def table_block(j, i):
        blk = jnp.where(i == 0, 0, 1 + lax.rem(i - 1, lay.tiles_per_batch))
        return (jnp.where(j < n_rope_tiles, blk, 0), 0)

    table_spec = pl.BlockSpec((lay.tm, RET_QK_DIM), table_block)
    return pl.pallas_call(
        _in_proj_kernel,
        grid=(n // tn, lay.n_tiles),
        in_specs=[
            pl.BlockSpec((lay.tm, d), lambda j, i: (i, 0)),
            pl.BlockSpec((None, d, tn), lambda j, i: (layer, 0, j)),
            table_spec,
            table_spec,
        ],
        out_specs=pl.BlockSpec((lay.tm, tn), lambda j, i: (i, j)),
        out_shape=jax.ShapeDtypeStruct((lay.rows, n), BF16),
        compiler_params=_params(2),
        name="in_proj",
    )(h, w_in, cos_t, sin_t)


RET_STEP_CHUNKS = 1
RET_CHUNK = 256


def _retention_kernel(lg_ref, q_ref, k_ref, v_ref, *rest, reverse, final):
    if final:
        other_ref, gate_ref, o_ref, s_ref = rest
    else:
        o_ref, s_ref = rest
    c = RET_CHUNK

    @pl.when(pl.program_id(1) == 0)
    def _():
        s_ref[...] = jnp.zeros_like(s_ref)

    qi = lax.broadcasted_iota(jnp.int32, (c, c), 0)
    kj = lax.broadcasted_iota(jnp.int32, (c, c), 1)
    diff = ((kj - qi) if reverse else (qi - kj)).astype(F32)
    pos = lax.broadcasted_iota(jnp.int32, (c, 1), 0).astype(F32)
    q_steps = (c - pos) if reverse else (pos + 1.0)
    k_steps = pos if reverse else (c - 1.0 - pos)

    subs = range(RET_STEP_CHUNKS)
    for h in range(RET_HEADS):
        log_g = lg_ref[h]
        intra = jnp.where(diff >= 0, jnp.exp(jnp.maximum(diff, 0.0) * log_g), 0.0)
        q_decay = jnp.exp(q_steps * log_g)
        k_decay = jnp.exp(k_steps * log_g)
        chunk_decay = jnp.exp(jnp.full((1, 1), float(c), F32) * log_g)
        for sub in (reversed(subs) if reverse else subs):
            rows = slice(sub * c, (sub + 1) * c)
            qh = q_ref[rows, h * RET_QK_DIM:(h + 1) * RET_QK_DIM]
            kh = k_ref[rows, h * RET_QK_DIM:(h + 1) * RET_QK_DIM]
            vh = v_ref[rows, h * RET_V_DIM:(h + 1) * RET_V_DIM]
            s = s_ref[h]

            scores = lax.dot_general(qh, kh, (((1,), (1,)), ((), ())), preferred_element_type=F32) * intra
            o = (jnp.dot(scores.astype(BF16), vh, preferred_element_type=F32)
                 + jnp.dot(qh, s.astype(BF16), preferred_element_type=F32) * q_decay)
            k_dec_t = (kh.astype(F32) * k_decay).T.astype(BF16)
            s_ref[h] = s * chunk_decay + jnp.dot(k_dec_t, vh, preferred_element_type=F32)

            sl = slice(h * RET_V_DIM, (h + 1) * RET_V_DIM)
            if final:
                o = o + other_ref[rows, sl].astype(F32)
                mu = jnp.mean(o, axis=-1, keepdims=True)
                oc = o - mu
                var = jnp.mean(oc * oc, axis=-1, keepdims=True)
                o = oc * lax.rsqrt(var + NORM_EPS) * gate_ref[rows, sl].astype(F32)
            o_ref[rows, sl] = o.astype(o_ref.dtype)


def _retention_pass(lay, p, log_g, other=None, *, reverse):
    final = other is not None
    step_rows = RET_STEP_CHUNKS * RET_CHUNK
    assert lay.ctx_len % step_rows == 0 and lay.seq % step_rows == 0
    ctx_steps, lat_steps = lay.ctx_len // step_rows, lay.seq // step_rows

    def row_block(b, n):
        ctx_blk = b * ctx_steps + ((ctx_steps - 1 - n) if reverse else n)
        m = n - ctx_steps
        lat_blk = lay.batch * ctx_steps + b * lat_steps + ((lat_steps - 1 - m) if reverse else m)
        return jnp.where(n < ctx_steps, ctx_blk, lat_blk)

    in_specs = [
        pl.BlockSpec((step_rows, RET_QK), lambda b, n, lg: (row_block(b, n), Q0 // RET_QK)),
        pl.BlockSpec((step_rows, RET_QK), lambda b, n, lg: (row_block(b, n), K0 // RET_QK)),
        pl.BlockSpec((step_rows, RET_V), lambda b, n, lg: (row_block(b, n), V0 // RET_V)),
    ]
    args = [p, p, p]
    if final:
        in_specs += [
            pl.BlockSpec((step_rows, RET_V), lambda b, n, lg: (row_block(b, n), 0)),
            pl.BlockSpec((step_rows, RET_V), lambda b, n, lg: (row_block(b, n), G0 // RET_V)),
        ]
        args += [other, p]
    return pl.pallas_call(
        functools.partial(_retention_kernel, reverse=reverse, final=final),
        grid_spec=pltpu.PrefetchScalarGridSpec(
            num_scalar_prefetch=1,
            grid=(lay.batch, ctx_steps + lat_steps),
            in_specs=in_specs,
            out_specs=pl.BlockSpec((step_rows, RET_V), lambda b, n, lg: (row_block(b, n), 0)),
            scratch_shapes=[pltpu.VMEM((RET_HEADS, RET_QK_DIM, RET_V_DIM), F32)],
        ),
        out_shape=jax.ShapeDtypeStruct((lay.rows, RET_V), BF16),
        compiler_params=_params(2),
        name="retention_fwd_merge" if final else "retention_bwd",
    )(log_g, *args)


def _sgu_kernel(u_ref, v_ref, lng_ref, lnb_ref, ws_ref, bst_ref, o_ref):
    gw = SGU_WIDTH // SGU_GROUPS
    for c0 in range(0, o_ref.shape[0], CHUNK):
        rows = slice(c0, c0 + CHUNK)
        v = v_ref[rows, :].astype(F32)
        mu = jnp.mean(v, axis=-1, keepdims=True)
        vc = v - mu
        var = jnp.mean(vc * vc, axis=-1, keepdims=True)
        vn = (vc * lax.rsqrt(var + NORM_EPS) * lng_ref[...] + lnb_ref[...]).astype(BF16)
        for g in range(SGU_GROUPS):
            sl = slice(g * gw, (g + 1) * gw)
            s = jnp.dot(ws_ref[g].astype(BF16), vn[:, sl], preferred_element_type=F32) + bst_ref[:, g:g + 1]
            o_ref[rows, sl] = (u_ref[rows, sl].astype(F32) * s).astype(o_ref.dtype)


def _sgu(lay, uv, ln_g, ln_b, w_s, b_s_t):
    return pl.pallas_call(
        _sgu_kernel,
        grid=(lay.n_tiles,),
        in_specs=[
            pl.BlockSpec((lay.tm, SGU_WIDTH), lambda i: (i, UA0 // SGU_WIDTH)),
            pl.BlockSpec((lay.tm, SGU_WIDTH), lambda i: (i, VA0 // SGU_WIDTH)),
            pl.BlockSpec((1, SGU_WIDTH), lambda i: (0, 0)),
            pl.BlockSpec((1, SGU_WIDTH), lambda i: (0, 0)),
            pl.BlockSpec((SGU_GROUPS, CHUNK, CHUNK), lambda i: (0, 0, 0)),
            pl.BlockSpec((CHUNK, SGU_GROUPS), lambda i: (0, 0)),
        ],
        out_specs=pl.BlockSpec((lay.tm, SGU_WIDTH), lambda i: (i, 0)),
        out_shape=jax.ShapeDtypeStruct((lay.rows, SGU_WIDTH), BF16),
        compiler_params=_params(1),
        name="sgu",
    )(uv, uv, ln_g.reshape(1, -1), ln_b.reshape(1, -1), w_s, b_s_t)


def _merge_kernel(sgu_ref, ret_ref, wa_ref, wr_ref, ga_ref, gr_ref, o_ref):
    a = jnp.dot(sgu_ref[...], wa_ref[...].astype(BF16), preferred_element_type=F32)
    r = jnp.dot(ret_ref[...], wr_ref[...].astype(BF16), preferred_element_type=F32)
    o_ref[...] = (ga_ref[...].astype(F32) * a + gr_ref[...].astype(F32) * r).astype(o_ref.dtype)


def _merge(lay, sgu, ret, gates, w_proj_a, w_proj_r, layer):
    n = w_proj_a.shape[2]
    tn = 512
    return pl.pallas_call(
        _merge_kernel,
        grid=(n // tn, lay.n_tiles),
        in_specs=[
            pl.BlockSpec((lay.tm, SGU_WIDTH), lambda j, i: (i, 0)),
            pl.BlockSpec((lay.tm, RET_V), lambda j, i: (i, 0)),
            pl.BlockSpec((None, SGU_WIDTH, tn), lambda j, i: (layer, 0, j)),
            pl.BlockSpec((None, RET_V, tn), lambda j, i: (layer, 0, j)),
            pl.BlockSpec((lay.tm, tn), lambda j, i: (i, GA0 // tn + j)),
            pl.BlockSpec((lay.tm, tn), lambda j, i: (i, GR0 // tn + j)),
        ],
        out_specs=pl.BlockSpec((lay.tm, tn), lambda j, i: (i, j)),
        out_shape=jax.ShapeDtypeStruct((lay.rows, n), BF16),
        compiler_params=_params(2),
        name="merge",
    )(sgu, ret, w_proj_a, w_proj_r, gates, gates)


def _proj_resid_kernel(a_ref, w_ref, resid_ref, gate_ref, o_ref):
    acc = jnp.dot(a_ref[...], w_ref[...].astype(BF16), preferred_element_type=F32)
    o_ref[...] = resid_ref[...] + gate_ref[0] * acc


def _proj_resid(lay, a, w, w_index, resid, mod3, layer, k_gate):
    kdim = a.shape[1]
    n = w.shape[-1]
    tn = 512
    w_spec = pl.BlockSpec((None,) * len(w_index) + (kdim, tn), lambda j, i: (*w_index, 0, j))
    return pl.pallas_call(
        _proj_resid_kernel,
        grid=(n // tn, lay.n_tiles),
        in_specs=[
            pl.BlockSpec((lay.tm, kdim), lambda j, i: (i, 0)),
            w_spec,
            pl.BlockSpec((lay.tm, tn), lambda j, i: (i, j)),
            pl.BlockSpec((1, 1, tn), lambda j, i: (_mod_index(lay, layer, k_gate, i), 0, j)),
        ],
        out_specs=pl.BlockSpec((lay.tm, tn), lambda j, i: (i, j)),
        out_shape=jax.ShapeDtypeStruct((lay.rows, n), F32),
        compiler_params=_params(2),
        name="proj_resid",
    )(a, w, resid, mod3)


OUT_PROJ_SUB = 512


def _out_proj_kernel(a_ref, w_ref, resid_ref, gate_ref, *rest, with_norm):
    if with_norm:
        g_ref, shift_ref, scale_ref, o_ref, h_ref = rest
    else:
        (o_ref,) = rest
    a = a_ref[...]
    for c0 in range(0, o_ref.shape[1], OUT_PROJ_SUB):
        cols = slice(c0, c0 + OUT_PROJ_SUB)
        acc = jnp.dot(a, w_ref[:, cols].astype(BF16), preferred_element_type=F32)
        o_ref[:, cols] = resid_ref[:, cols] + gate_ref[0][:, cols] * acc
    if with_norm:
        x = o_ref[...]
        y = x * lax.rsqrt(jnp.mean(x * x, axis=-1, keepdims=True) + NORM_EPS) * g_ref[...]
        h_ref[...] = (y * (1.0 + scale_ref[0]) + shift_ref[0]).astype(h_ref.dtype)


def _out_proj(lay, a, w, layer, resid, mod3, k_gate, norm_g=None, k_shift=None, k_scale=None):
    kdim, n = w.shape[-2:]
    with_norm = norm_g is not None
    row = lambda width: pl.BlockSpec((lay.tm, width), lambda i: (i, 0))
    mod = lambda k: pl.BlockSpec((1, 1, n), lambda i: (_mod_index(lay, layer, k, i), 0, 0))
    in_specs = [
        row(kdim),
        pl.BlockSpec((None, kdim, n), lambda i: (layer, 0, 0), pipeline_mode=pl.Buffered(1)),
        row(n),
        mod(k_gate),
    ]
    args = [a, w, resid, mod3]
    x_shape = jax.ShapeDtypeStruct((lay.rows, n), F32)
    if with_norm:
        in_specs += [pl.BlockSpec((1, n), lambda i: (0, 0)), mod(k_shift), mod(k_scale)]
        args += [norm_g.reshape(1, n), mod3, mod3]
        out_specs, out_shape = [row(n), row(n)], [x_shape, jax.ShapeDtypeStruct((lay.rows, n), BF16)]
    else:
        out_specs, out_shape = row(n), x_shape
    return pl.pallas_call(
        functools.partial(_out_proj_kernel, with_norm=with_norm),
        grid=(lay.n_tiles,),
        in_specs=in_specs,
        out_specs=out_specs,
        out_shape=out_shape,
        compiler_params=_params(1),
        name="out_proj",
    )(*args)


def _swiglu_up_kernel(h_ref, w1_ref, w3_ref, o_ref):
    h = h_ref[...]
    for c0 in range(0, o_ref.shape[1], SWIGLU_SUB):
        cols = slice(c0, c0 + SWIGLU_SUB)
        a = jnp.dot(h, w1_ref[:, cols].astype(BF16), preferred_element_type=F32)
        b = jnp.dot(h, w3_ref[:, cols].astype(BF16), preferred_element_type=F32)
        o_ref[:, cols] = (_silu(a) * b).astype(o_ref.dtype)


def _swiglu_up(lay, h, w1, w3, w_index):
    d = h.shape[1]
    f = w1.shape[-1]
    tn = 512
    w_spec = pl.BlockSpec((None,) * len(w_index) + (d, tn), lambda j, i: (*w_index, 0, j))
    return pl.pallas_call(
        _swiglu_up_kernel,
        grid=(f // tn, lay.n_tiles),
        in_specs=[pl.BlockSpec((lay.tm, d), lambda j, i: (i, 0)), w_spec, w_spec],
        out_specs=pl.BlockSpec((lay.tm, tn), lambda j, i: (i, j)),
        out_shape=jax.ShapeDtypeStruct((lay.rows, f), BF16),
        compiler_params=_params(2),
        name="swiglu_up",
    )(h, w1, w3)


def _pack_bf16_pairs(y):
    half = y.shape[1] // 2
    lo = lax.bitcast_convert_type(y[:, :half].astype(BF16).astype(F32), jnp.uint32) >> 16
    hi = lax.bitcast_convert_type(y[:, half:].astype(BF16).astype(F32), jnp.uint32)
    return hi | lo


def _unpack_bf16_pairs(w):
    lo = lax.bitcast_convert_type(w << 16, F32).astype(BF16)
    hi = lax.bitcast_convert_type(w & jnp.uint32(0xFFFF0000), F32).astype(BF16)
    return jnp.concatenate([lo, hi], axis=1)


def _norm_router_kernel(x_ref, g_ref, shift_ref, scale_ref, rw_ref, rb_ref,
                        hp_ref, sel_ref, prob_ref, cnt_ref, run_ref):
    @pl.when(pl.program_id(0) == 0)
    def _():
        run_ref[...] = jnp.zeros_like(run_ref)

    x = x_ref[...]
    y = x * lax.rsqrt(jnp.mean(x * x, axis=-1, keepdims=True) + NORM_EPS) * g_ref[...]
    h = y * (1.0 + scale_ref[0]) + shift_ref[0]
    hp_ref[...] = _pack_bf16_pairs(h)

    logits = jnp.dot(h.astype(BF16), rw_ref[...], preferred_element_type=F32) + rb_ref[...]
    lane = lax.broadcasted_iota(jnp.int32, logits.shape, 1)
    neg = jnp.float32(-jnp.inf)
    lg = jnp.where(lane < N_EXPERTS, logits, neg)
    m1 = jnp.max(lg, axis=-1, keepdims=True)
    i1 = jnp.min(jnp.where(lg == m1, lane, LANES), axis=-1, keepdims=True)
    lg2 = jnp.where(lane == i1, neg, lg)
    m2 = jnp.max(lg2, axis=-1, keepdims=True)
    i2 = jnp.min(jnp.where(lg2 == m2, lane, LANES), axis=-1, keepdims=True)
    e2 = jnp.exp(m2 - m1)
    den = 1.0 + e2

    tm = x.shape[0]
    onehot = jnp.where(lane == i1, 1.0, 0.0) + jnp.where(lane == i2, 1.0, 0.0)
    earlier = jnp.where(lax.broadcasted_iota(jnp.int32, (tm, tm), 0) > lax.broadcasted_iota(jnp.int32, (tm, tm), 1),
                        1.0, 0.0).astype(BF16)
    before = jnp.dot(earlier, onehot.astype(BF16), preferred_element_type=F32) + run_ref[0:1, :]
    pos1 = jnp.sum(jnp.where(lane == i1, before, 0.0), axis=-1, keepdims=True).astype(jnp.int32)
    pos2 = jnp.sum(jnp.where(lane == i2, before, 0.0), axis=-1, keepdims=True).astype(jnp.int32)
    run_ref[...] = run_ref[...] + jnp.sum(onehot, axis=0, keepdims=True)
    cnt_ref[...] = run_ref[...]

    sel_ref[...] = jnp.where(lane == 0, i1, jnp.where(lane == 1, i2, jnp.where(lane == 2, pos1,
                                                                                jnp.where(lane == 3, pos2, 0))))
    prob_ref[...] = jnp.where(lane == 0, 1.0 / den, jnp.where(lane == 1, e2 / den, 0.0))


def _norm_router(lay, x_all, g, mod3, layer, k_shift, k_scale, router_w, router_b):
    d = x_all.shape[1]
    w = jnp.zeros((d, LANES), BF16).at[:, :N_EXPERTS].set(router_w.astype(BF16))
    b = jnp.zeros((1, LANES), F32).at[0, :N_EXPERTS].set(router_b.astype(F32))
    lane_block = pl.BlockSpec((lay.tm, LANES), lambda i: (i, 0))
    return pl.pallas_call(
        _norm_router_kernel,
        grid=(lay.n_tiles,),
        in_specs=[
            pl.BlockSpec((lay.tm, d), lambda i: (i, 0)),
            pl.BlockSpec((1, d), lambda i: (0, 0)),
            pl.BlockSpec((1, 1, d), lambda i: (_mod_index(lay, layer, k_shift, i), 0, 0)),
            pl.BlockSpec((1, 1, d), lambda i: (_mod_index(lay, layer, k_scale, i), 0, 0)),
            pl.BlockSpec((d, LANES), lambda i: (0, 0)),
            pl.BlockSpec((1, LANES), lambda i: (0, 0)),
        ],
        out_specs=[
            pl.BlockSpec((lay.tm, d // 2), lambda i: (i, 0)),
            lane_block,
            lane_block,
            pl.BlockSpec((8, LANES), lambda i: (0, 0)),
        ],
        out_shape=[
            jax.ShapeDtypeStruct((lay.rows, d // 2), jnp.uint32),
            jax.ShapeDtypeStruct((lay.rows, LANES), jnp.int32),
            jax.ShapeDtypeStruct((lay.rows, LANES), F32),
            jax.ShapeDtypeStruct((8, LANES), F32),
        ],
        scratch_shapes=[pltpu.VMEM((8, LANES), F32)],
        compiler_params=_params(1),
        name="norm_router",
    )(x_all, g.reshape(1, d), mod3, mod3, w, b)


def _routing_plan(sel, cnt, n_sorted_rows):
    counts = cnt[0, :N_EXPERTS].astype(jnp.int32)
    padded = (counts + MOE_TILE - 1) // MOE_TILE * MOE_TILE
    ends = jnp.cumsum(padded)
    starts = ends - padded
    dest1 = starts[sel[:, 0]] + sel[:, 2]
    dest2 = starts[sel[:, 1]] + sel[:, 3]
    tile_start = jnp.arange(n_sorted_rows // MOE_TILE, dtype=jnp.int32) * MOE_TILE
    tile_expert = jnp.minimum(jnp.sum(tile_start[:, None] >= ends[None, :], axis=1), N_EXPERTS - 1)
    n_used_tiles = (ends[-1:] // MOE_TILE).astype(jnp.int32)
    present = counts > 0
    run_expert = jnp.argsort(jnp.logical_not(present), stable=True).astype(jnp.int32)
    run_of_expert = jnp.cumsum(present.astype(jnp.int32)) - 1
    tile_run = jnp.maximum(run_of_expert[tile_expert], 0).astype(jnp.int32)
    run_first_tile = (starts[run_expert] // MOE_TILE).astype(jnp.int32)
    n_runs = jnp.sum(present.astype(jnp.int32)).reshape(1)
    fill_lo = jnp.concatenate([starts + counts, ends[-1:]]).astype(jnp.int32)
    fill_hi = jnp.concatenate([ends, jnp.full((1,), n_sorted_rows, jnp.int32)]).astype(jnp.int32)
    return dest1, dest2, (fill_lo, fill_hi), (tile_run, run_first_tile, run_expert, n_runs, n_used_tiles)


def _dispatch_kernel(dest1_ref, dest2_ref, fill_lo_ref, fill_hi_ref, h_ref, xs_ref, zero_ref, sem, zero_sem):
    i = pl.program_id(0)
    tm = h_ref.shape[0]
    base = i * tm

    def row_copy(t, d):
        return pltpu.make_async_copy(h_ref.at[pl.ds(t, 1)], xs_ref.at[pl.ds(d, 1)], sem)

    def zero_copy(d):
        return pltpu.make_async_copy(zero_ref, xs_ref.at[pl.ds(d, 1)], zero_sem)

    @pl.when(i == 0)
    def _():
        zero_ref[...] = jnp.zeros_like(zero_ref)
        for r in range(N_EXPERTS + 1):
            lo, hi = fill_lo_ref[r], fill_hi_ref[r]

            def start_zero(d, carry, lo=lo):
                zero_copy(d).start()

                @pl.when(d - lo >= ZERO_FILL_WINDOW)
                def _():
                    zero_copy(0).wait()

                return carry

            def wait_zero(d, carry):
                zero_copy(0).wait()
                return carry

            lax.fori_loop(lo, hi, start_zero, 0)
            lax.fori_loop(0, jnp.minimum(hi - lo, ZERO_FILL_WINDOW), wait_zero, 0)

    def start(q, carry):
        for u in range(ROW_DMA_UNROLL):
            t = q * ROW_DMA_UNROLL + u
            row_copy(t, dest1_ref[base + t]).start(priority=0)
            row_copy(t, dest2_ref[base + t]).start(priority=1)
        return carry

    def wait(q, carry):
        for _ in range(2 * ROW_DMA_UNROLL):
            row_copy(0, 0).wait()
        return carry

    lax.fori_loop(0, tm // ROW_DMA_UNROLL, start, 0)
    lax.fori_loop(0, tm // ROW_DMA_UNROLL, wait, 0)


def _dispatch(lay, hp, dest1, dest2, fill_lo, fill_hi, n_sorted_rows):
    width = hp.shape[1]
    return pl.pallas_call(
        _dispatch_kernel,
        grid_spec=pltpu.PrefetchScalarGridSpec(
            num_scalar_prefetch=4,
            grid=(lay.n_tiles,),
            in_specs=[pl.BlockSpec((lay.tm, width), lambda i, *_: (i, 0))],
            out_specs=pl.BlockSpec(memory_space=pl.ANY),
            scratch_shapes=[
                pltpu.VMEM((1, width), hp.dtype),
                pltpu.SemaphoreType.DMA(()),
                pltpu.SemaphoreType.DMA(()),
            ],
        ),
        out_shape=jax.ShapeDtypeStruct((n_sorted_rows, width), hp.dtype),
        compiler_params=_params(1),
        name="moe_dispatch",
    )(dest1, dest2, fill_lo, fill_hi, hp)


def _grouped_kernel(tile_run_ref, run_first_ref, run_expert_ref, n_runs_ref, n_used_ref, a_ref, *rest,
                    moe_layer, tn, swiglu):
    n_mats = 2 if swiglu else 1
    w_hbm = rest[:n_mats]
    o_ref, wbuf, sems = rest[n_mats:]
    j, i = pl.program_id(0), pl.program_id(1)
    n_runs = n_runs_ref[0]
    run = tile_run_ref[i]
    seq = j * n_runs + run
    slot = lax.rem(seq, 2)

    def fetches(jj, rr, slot_):
        expert = run_expert_ref[rr]
        col = pl.multiple_of(jj * tn, tn)
        return [pltpu.make_async_copy(w.at[moe_layer, expert, :, pl.ds(col, tn)], wbuf.at[slot_, m], sems.at[slot_, m])
                for m, w in enumerate(w_hbm)]

    live = i < n_used_ref[0]

    @pl.when(live & (i == run_first_ref[run]))
    def _():
        @pl.when(seq == 0)
        def _():
            for c in fetches(j, run, slot):
                c.start()

        for c in fetches(j, run, slot):
            c.wait()
        wraps = run + 1 == n_runs
        next_j = jnp.where(wraps, j + 1, j)
        next_run = jnp.where(wraps, 0, run + 1)

        @pl.when(next_j < pl.num_programs(0))
        def _():
            for c in fetches(next_j, next_run, 1 - slot):
                c.start()

    @pl.when(live)
    def _():
        if swiglu:
            h = _unpack_bf16_pairs(a_ref[...])
            for c0 in range(0, tn, SWIGLU_SUB):
                cols = slice(c0, c0 + SWIGLU_SUB)
                a = jnp.dot(h, wbuf[slot, 0, :, cols].astype(BF16), preferred_element_type=F32)
                b = jnp.dot(h, wbuf[slot, 1, :, cols].astype(BF16), preferred_element_type=F32)
                o_ref[:, cols] = (_silu(a) * b).astype(o_ref.dtype)
        else:
            o_ref[...] = jnp.dot(a_ref[...], wbuf[slot, 0].astype(BF16), preferred_element_type=F32)

    @pl.when(jnp.logical_not(live))
    def _():
        o_ref[...] = jnp.zeros_like(o_ref)


def _grouped_matmul(a, weights, moe_layer, plan, out_dtype, *, swiglu, name):
    n_rows, a_width = a.shape
    kdim, n = weights[0].shape[-2:]
    tn = 512
    n_prefetch = 5
    idx = lambda j, i, *_: (i, 0)
    return pl.pallas_call(
        functools.partial(_grouped_kernel, moe_layer=moe_layer, tn=tn, swiglu=swiglu),
        grid_spec=pltpu.PrefetchScalarGridSpec(
            num_scalar_prefetch=n_prefetch,
            grid=(n // tn, n_rows // MOE_TILE),
            in_specs=[pl.BlockSpec((MOE_TILE, a_width), idx)] + [pl.BlockSpec(memory_space=pl.ANY)] * len(weights),
            out_specs=pl.BlockSpec((MOE_TILE, tn), lambda j, i, *_: (i, j)),
            scratch_shapes=[
                pltpu.VMEM((2, len(weights), kdim, tn), F32),
                pltpu.SemaphoreType.DMA((2, len(weights))),
            ],
        ),
        out_shape=jax.ShapeDtypeStruct((n_rows, n), out_dtype),
        compiler_params=_params(2),
        name=name,
    )(*plan, a, *weights)


def _combine_kernel(dest1_ref, dest2_ref, x_ref, prob_ref, gate_ref, ys_ref, *rest, mode, first_tile):
    if mode == "final":
        g_ref, o_ref, buf, sems = rest
    else:
        g_ref, shift_ref, scale_ref, o_ref, h_ref, buf, sems = rest
    tm = x_ref.shape[0]
    i = pl.program_id(0)

    def row_copy(slot, k, t, d):
        return pltpu.make_async_copy(ys_ref.at[pl.ds(d, 1)], buf.at[slot, k, pl.ds(t, 1)], sems.at[slot, k])

    def start_tile(step):
        slot = lax.rem(step, 2)
        base = (step + first_tile) * tm

        def start(q, carry):
            for u in range(ROW_DMA_UNROLL):
                t = q * ROW_DMA_UNROLL + u
                row_copy(slot, 0, t, dest1_ref[base + t]).start(priority=0)
                row_copy(slot, 1, t, dest2_ref[base + t]).start(priority=1)
            return carry

        lax.fori_loop(0, tm // ROW_DMA_UNROLL, start, 0)

    @pl.when(i == 0)
    def _():
        start_tile(0)

    @pl.when(i + 1 < pl.num_programs(0))
    def _():
        start_tile(i + 1)

    slot = lax.rem(i, 2)

    def wait(t, carry):
        row_copy(slot, 0, 0, 0).wait()
        row_copy(slot, 1, 0, 0).wait()
        return carry

    lax.fori_loop(0, tm, wait, 0, unroll=2)
    p = prob_ref[...]
    f = p[:, 0:1] * buf[slot, 0] + p[:, 1:2] * buf[slot, 1]
    x = x_ref[...] + gate_ref[0] * f
    y = x * lax.rsqrt(jnp.mean(x * x, axis=-1, keepdims=True) + NORM_EPS) * g_ref[...]
    if mode == "final":
        o_ref[...] = y
    else:
        o_ref[...] = x
        h_ref[...] = (y * (1.0 + scale_ref[0]) + shift_ref[0]).astype(h_ref.dtype)


def _combine(lay, x_all, prob, ys, dest1, dest2, mod3, layer, k_gate, norm_g, *, final):
    d = x_all.shape[1]
    first_tile = 1 if final else 0
    tile = lambda i, d1, d2: (i + first_tile, 0)
    mod = lambda lyr, k: (lambda i, d1, d2: (_mod_index(lay, lyr, k, i + first_tile), 0, 0))
    row_spec = pl.BlockSpec((lay.tm, d), lambda i, d1, d2: (i, 0))
    in_specs = [
        pl.BlockSpec((lay.tm, d), tile),
        pl.BlockSpec((lay.tm, LANES), tile),
        pl.BlockSpec((1, 1, d), mod(layer, k_gate)),
        pl.BlockSpec(memory_space=pl.ANY),
        pl.BlockSpec((1, d), lambda i, d1, d2: (0, 0)),
    ]
    args = [x_all, prob, mod3, ys, norm_g.reshape(1, d)]
    n_tiles = lay.n_tiles - first_tile
    if final:
        out_specs = row_spec
        out_shape = jax.ShapeDtypeStruct((n_tiles * lay.tm, d), F32)
    else:
        in_specs += [pl.BlockSpec((1, 1, d), mod(layer + 1, 0)), pl.BlockSpec((1, 1, d), mod(layer + 1, 1))]
        args += [mod3, mod3]
        out_specs = [row_spec, row_spec]
        out_shape = [jax.ShapeDtypeStruct((lay.rows, d), F32), jax.ShapeDtypeStruct((lay.rows, d), BF16)]
    return pl.pallas_call(
        functools.partial(_combine_kernel, mode="final" if final else "next", first_tile=first_tile),
        grid_spec=pltpu.PrefetchScalarGridSpec(
            num_scalar_prefetch=2,
            grid=(n_tiles,),
            in_specs=in_specs,
            out_specs=out_specs,
            scratch_shapes=[pltpu.VMEM((2, 2, lay.tm, d), F32), pltpu.SemaphoreType.DMA((2, 2))],
        ),
        out_shape=out_shape,
        compiler_params=_params(1),
        name="moe_combine",
    )(dest1, dest2, *args)


def _moe_ffn(lay, x_all, norm_g, mod3, layer, router_w, router_b, w1, w3, w2, moe_layer, next_norm_g, *, final):
    hp, sel, prob, cnt = _norm_router(lay, x_all, norm_g, mod3, layer, 3, 4, router_w, router_b)
    n_sorted_rows = 2 * lay.rows + N_EXPERTS * MOE_TILE
    dest1, dest2, fill, plan = _routing_plan(sel, cnt, n_sorted_rows)
    xs = _dispatch(lay, hp, dest1, dest2, *fill, n_sorted_rows)
    act = _grouped_matmul(xs, (w1, w3), moe_layer, plan, BF16, swiglu=True, name="moe_up")
    ys = _grouped_matmul(act, (w2,), moe_layer, plan, F32, swiglu=False, name="moe_down")
    return _combine(lay, x_all, prob, ys, dest1, dest2, mod3, layer, 5, next_norm_g, final=final)


def _rope_tables(lay):
    t = np.arange(lay.seq)
    quarter = RET_QK_DIM // 4
    freqs = np.float32(ROPE_BASE) ** (-np.arange(quarter, dtype=np.float32) / np.float32(quarter))

    def tables(pos):
        ang = pos.astype(np.float32)[:, None] * freqs[None, :]
        cos, sin = np.cos(ang), np.sin(ang)
        return np.concatenate([cos, cos], axis=-1), np.concatenate([-sin, sin], axis=-1)

    cos_r, sin_r = tables(t // GRID_W)
    cos_c, sin_c = tables(t % GRID_W)
    cos_t = np.concatenate([np.ones((lay.tm, RET_QK_DIM), np.float32), np.concatenate([cos_r, cos_c], -1)], axis=0)
    sin_t = np.concatenate([np.zeros((lay.tm, RET_QK_DIM), np.float32), np.concatenate([sin_r, sin_c], -1)], axis=0)
    return jnp.asarray(cos_t, F32), jnp.asarray(sin_t, F32)


def kernel(x, c, ctx, c_ctx, w_mod, b_mod, norm1_g, norm2_g, w_in, sgu_ln_g, sgu_ln_b, sgu_w, sgu_b,
           ret_log_decay, w_proj_a, w_proj_r, w_out, ffn_w1, ffn_w3, ffn_w2, router_w, router_b,
           moe_w1, moe_w3, moe_w2, final_norm_g):
    batch, seq, d = x.shape
    ctx_len = ctx.shape[1]
    depth = w_mod.shape[0]
    lay = _Layout(batch, seq, ctx_len)

    cvec = jnp.zeros((MOD_ROWS, d), F32).at[:batch].set(c).at[batch].set(c_ctx)
    mod = _modulation(cvec, w_mod, b_mod)
    mod3 = mod.reshape(depth * MOD_ROWS * N_MOD, 1, d)

    cos_t, sin_t = _rope_tables(lay)
    log_decay = ret_log_decay.astype(F32)

    x_all, h = _join_norm_modulate(lay, ctx.reshape(batch * ctx_len, d), x.reshape(batch * seq, d),
                                   norm1_g[0], mod3, 0, 0, 1)
    out = None
    for layer in range(depth):
        if h is None:
            h = _norm_modulate(lay, x_all, norm1_g[layer], mod3, layer, 0, 1, BF16)
        p = _in_proj(lay, h, w_in, layer, cos_t, sin_t)
        o_bwd = _retention_pass(lay, p, log_decay[layer, 1], reverse=True)
        ret = _retention_pass(lay, p, log_decay[layer, 0], o_bwd, reverse=False)
        sgu = _sgu(lay, p, sgu_ln_g[layer], sgu_ln_b[layer], sgu_w[layer], sgu_b[layer].T)
        y = _merge(lay, sgu, ret, p, w_proj_a, w_proj_r, layer)
        i = layer // 2
        last = layer == depth - 1
        h = None
        if layer % 2 == 0:
            x_all, h2 = _out_proj(lay, y, w_out, layer, x_all, mod3, 2, norm2_g[layer], 3, 4)
            act = _swiglu_up(lay, h2, ffn_w1, ffn_w3, (i,))
            x_all = _proj_resid(lay, act, ffn_w2, (i,), x_all, mod3, layer, 5)
        else:
            x_all = _out_proj(lay, y, w_out, layer, x_all, mod3, 2)
            res = _moe_ffn(lay, x_all, norm2_g[layer], mod3, layer, router_w[i], router_b[i], moe_w1, moe_w3, moe_w2,
                           i, final_norm_g if last else norm1_g[layer + 1], final=last)
            if last:
                out = res
            else:
                x_all, h = res

    if out is None:
        out = _final_norm(lay, x_all, final_norm_g)
    return out.reshape(batch, seq, d)
```

```python
import functools

import jax
import jax.numpy as jnp
import numpy as np
from jax import lax
from jax.experimental import pallas as pl
from jax.experimental.pallas import tpu as pltpu

F32 = jnp.float32
BF16 = jnp.bfloat16

GRID_W = 64
CHUNK = 128
SGU_GROUPS = 8
SGU_WIDTH = 2048
RET_HEADS = 8
RET_QK_DIM = 256
RET_V_DIM = 512
RET_QK = RET_HEADS * RET_QK_DIM
RET_V = RET_HEADS * RET_V_DIM
ROPE_BASE = 10000.0
N_EXPERTS = 8
N_MOD = 6
NORM_EPS = 1e-6
MOD_ROWS = 8
LANES = 128
MOE_TILE = 512
ZERO_FILL_WINDOW = 256
ROW_DMA_UNROLL = 4
GROUPED_DOWN_SUB = 512
SWIGLU_SUB = 256

VMEM_LIMIT_BYTES = 52 * 1024 * 1024

Q0 = 0
K0 = Q0 + RET_QK
V0 = K0 + RET_QK
G0 = V0 + RET_V
UA0 = G0 + RET_V
VA0 = UA0 + SGU_WIDTH
GA0 = VA0 + SGU_WIDTH
GR0 = GA0 + 2048
IN_COLS = GR0 + 2048


def _params(n_grid_dims):
    return pltpu.CompilerParams(
        dimension_semantics=("arbitrary",) * n_grid_dims,
        vmem_limit_bytes=VMEM_LIMIT_BYTES,
    )


def _gelu_tanh(x):
    c = 0.7978845608028654
    half_x = 0.5 * x
    return half_x * jnp.tanh(x * (c + (c * 0.044715) * (x * x))) + half_x


def _sigmoid(x):
    return 0.5 * jnp.tanh(0.5 * x) + 0.5


def _silu(x):
    return x * _sigmoid(x)


def _mod_kernel(c_ref, w_ref, b_ref, o_ref):
    sc = _silu(c_ref[...]).astype(BF16)
    o_ref[...] = jnp.dot(sc, w_ref[...].astype(BF16), preferred_element_type=F32) + b_ref[...]


def _modulation(cvec, w_mod, b_mod):
    depth, d, n = w_mod.shape
    tn = 1024
    return pl.pallas_call(
        _mod_kernel,
        grid=(depth, n // tn),
        in_specs=[
            pl.BlockSpec((MOD_ROWS, d), lambda l, j: (0, 0)),
            pl.BlockSpec((None, d, tn), lambda l, j: (l, 0, j)),
            pl.BlockSpec((None, 1, tn), lambda l, j: (l, 0, j)),
        ],
        out_specs=pl.BlockSpec((None, MOD_ROWS, tn), lambda l, j: (l, 0, j)),
        out_shape=jax.ShapeDtypeStruct((depth, MOD_ROWS, n), F32),
        compiler_params=_params(2),
        name="modulation",
    )(cvec, w_mod, b_mod.reshape(depth, 1, n))


def _norm_mod_kernel(x_ref, g_ref, shift_ref, scale_ref, o_ref):
    x = x_ref[...]
    y = x * lax.rsqrt(jnp.mean(x * x, axis=-1, keepdims=True) + NORM_EPS) * g_ref[...]
    o_ref[...] = (y * (1.0 + scale_ref[0]) + shift_ref[0]).astype(o_ref.dtype)


def _norm_kernel(x_ref, g_ref, o_ref):
    x = x_ref[...]
    y = x * lax.rsqrt(jnp.mean(x * x, axis=-1, keepdims=True) + NORM_EPS) * g_ref[...]
    o_ref[...] = y.astype(o_ref.dtype)


class _Layout:
    def __init__(self, batch, seq, ctx_len):
        self.batch, self.seq, self.ctx_len = batch, seq, ctx_len
        self.tm = batch * ctx_len
        assert seq % self.tm == 0 and ctx_len % CHUNK == 0 and seq % CHUNK == 0
        self.tiles_per_batch = seq // self.tm
        self.n_tiles = 1 + batch * self.tiles_per_batch
        self.rows = self.n_tiles * self.tm
        self.ctx_chunks = ctx_len // CHUNK
        self.lat_chunks = seq // CHUNK

    def mod_row(self, tile):
        return jnp.where(tile == 0, self.batch, (tile - 1) // self.tiles_per_batch)


def _mod_index(lay, layer, k, tile):
    return layer * MOD_ROWS * N_MOD + lay.mod_row(tile) * N_MOD + k


def _norm_modulate(lay, x_all, g, mod3, layer, k_shift, k_scale, out_dtype):
    d = x_all.shape[1]
    return pl.pallas_call(
        _norm_mod_kernel,
        grid=(lay.n_tiles,),
        in_specs=[
            pl.BlockSpec((lay.tm, d), lambda i: (i, 0)),
            pl.BlockSpec((1, d), lambda i: (0, 0)),
            pl.BlockSpec((1, 1, d), lambda i: (_mod_index(lay, layer, k_shift, i), 0, 0)),
            pl.BlockSpec((1, 1, d), lambda i: (_mod_index(lay, layer, k_scale, i), 0, 0)),
        ],
        out_specs=pl.BlockSpec((lay.tm, d), lambda i: (i, 0)),
        out_shape=jax.ShapeDtypeStruct((lay.rows, d), out_dtype),
        compiler_params=_params(1),
        name="norm_modulate",
    )(x_all, g.reshape(1, d), mod3, mod3)


def _join_norm_kernel(ctx_ref, x_ref, g_ref, shift_ref, scale_ref, xall_ref, h_ref):
    def emit(src_ref):
        x = src_ref[...]
        xall_ref[...] = x
        y = x * lax.rsqrt(jnp.mean(x * x, axis=-1, keepdims=True) + NORM_EPS) * g_ref[...]
        h_ref[...] = (y * (1.0 + scale_ref[0]) + shift_ref[0]).astype(h_ref.dtype)

    @pl.when(pl.program_id(0) == 0)
    def _():
        emit(ctx_ref)

    @pl.when(pl.program_id(0) > 0)
    def _():
        emit(x_ref)


def _join_norm_modulate(lay, ctx2d, x2d, g, mod3, layer, k_shift, k_scale):
    d = x2d.shape[1]
    row = pl.BlockSpec((lay.tm, d), lambda i: (i, 0))
    return pl.pallas_call(
        _join_norm_kernel,
        grid=(lay.n_tiles,),
        in_specs=[
            pl.BlockSpec((lay.tm, d), lambda i: (0, 0)),
            pl.BlockSpec((lay.tm, d), lambda i: (jnp.maximum(i - 1, 0), 0)),
            pl.BlockSpec((1, d), lambda i: (0, 0)),
            pl.BlockSpec((1, 1, d), lambda i: (_mod_index(lay, layer, k_shift, i), 0, 0)),
            pl.BlockSpec((1, 1, d), lambda i: (_mod_index(lay, layer, k_scale, i), 0, 0)),
        ],
        out_specs=[row, row],
        out_shape=[jax.ShapeDtypeStruct((lay.rows, d), F32), jax.ShapeDtypeStruct((lay.rows, d), BF16)],
        compiler_params=_params(1),
        name="join_norm_modulate",
    )(ctx2d, x2d, g.reshape(1, d), mod3, mod3)


def _final_norm(lay, x_all, g):
    d = x_all.shape[1]
    n_lat_tiles = lay.n_tiles - 1
    return pl.pallas_call(
        _norm_kernel,
        grid=(n_lat_tiles,),
        in_specs=[
            pl.BlockSpec((lay.tm, d), lambda i: (i + 1, 0)),
            pl.BlockSpec((1, d), lambda i: (0, 0)),
        ],
        out_specs=pl.BlockSpec((lay.tm, d), lambda i: (i, 0)),
        out_shape=jax.ShapeDtypeStruct((n_lat_tiles * lay.tm, d), F32),
        compiler_params=_params(1),
        name="final_norm",
    )(x_all, g.reshape(1, d))


IN_PROJ_TN = 2048
IN_PROJ_SUB = 512


IN_PROJ_KINDS = ((V0, "rope"), (G0, "plain"), (UA0, "silu"), (GA0, "gelu"), (IN_COLS, "sigmoid"))


def _in_proj_kernel(h_ref, w_ref, cos_ref, sin_ref, o_ref):
    col = pl.program_id(0) * IN_PROJ_TN
    start = 0
    for end, kind in IN_PROJ_KINDS:
        @pl.when((col >= start) & (col < end))
        def _(kind=kind):
            _in_proj_tile(h_ref, w_ref, cos_ref, sin_ref, o_ref, kind)

        start = end


def _in_proj_tile(h_ref, w_ref, cos_ref, sin_ref, o_ref, kind):
    h = h_ref[...]
    if kind == "rope":
        scale = jnp.where(pl.program_id(0) == 0, 1.0, RET_QK_DIM ** -0.5).astype(F32)
    for c0 in range(0, IN_PROJ_TN, IN_PROJ_SUB):
        acc = jnp.dot(h, w_ref[:, c0:c0 + IN_PROJ_SUB].astype(BF16), preferred_element_type=F32)
        if kind == "rope":
            for s in range(0, IN_PROJ_SUB, LANES):
                t = (c0 + s) % RET_QK_DIM
                xs = acc[:, s:s + LANES]
                r = xs * cos_ref[:, t:t + LANES] + pltpu.roll(xs, LANES // 2, 1) * sin_ref[:, t:t + LANES]
                o_ref[:, c0 + s:c0 + s + LANES] = (r * scale).astype(o_ref.dtype)
        else:
            if kind == "silu":
                acc = _silu(acc)
            elif kind == "gelu":
                acc = _gelu_tanh(acc)
            elif kind == "sigmoid":
                acc = _sigmoid(acc)
            else:
                assert kind == "plain"
            o_ref[:, c0:c0 + IN_PROJ_SUB] = acc.astype(o_ref.dtype)


def _in_proj(lay, h, w_in, layer, cos_t, sin_t):
    d = h.shape[1]
    n = w_in.shape[2]
    tn = IN_PROJ_TN
    n_rope_tiles = V0 // tn

    def table_block(j, i):
        blk = jnp.where(i == 0, 0, 1 + lax.rem(i - 1, lay.tiles_per_batch))
        return (jnp.where(j < n_rope_tiles, blk, 0), 0)

    table_spec = pl.BlockSpec((lay.tm, RET_QK_DIM), table_block)
    return pl.pallas_call(
        _in_proj_kernel,
        grid=(n // tn, lay.n_tiles),
        in_specs=[
            pl.BlockSpec((lay.tm, d), lambda j, i: (i, 0)),
            pl.BlockSpec((None, d, tn), lambda j, i: (layer, 0, j)),
            table_spec,
            table_spec,
        ],
        out_specs=pl.BlockSpec((lay.tm, tn), lambda j, i: (i, j)),
        out_shape=jax.ShapeDtypeStruct((lay.rows, n), BF16),
        compiler_params=_params(2),
        name="in_proj",
    )(h, w_in, cos_t, sin_t)


RET_STEP_CHUNKS = 1
RET_CHUNK = 256


def _retention_kernel(lg_ref, q_ref, k_ref, v_ref, *rest, reverse, final):
    if final:
        other_ref, gate_ref, o_ref, s_ref = rest
    else:
        o_ref, s_ref = rest
    c = RET_CHUNK

    @pl.when(pl.program_id(1) == 0)
    def _():
        s_ref[...] = jnp.zeros_like(s_ref)

    qi = lax.broadcasted_iota(jnp.int32, (c, c), 0)
    kj = lax.broadcasted_iota(jnp.int32, (c, c), 1)
    diff = ((kj - qi) if reverse else (qi - kj)).astype(F32)
    pos = lax.broadcasted_iota(jnp.int32, (c, 1), 0).astype(F32)
    q_steps = (c - pos) if reverse else (pos + 1.0)
    k_steps = pos if reverse else (c - 1.0 - pos)

    subs = range(RET_STEP_CHUNKS)
    for h in range(RET_HEADS):
        log_g = lg_ref[h]
        intra = jnp.where(diff >= 0, jnp.exp(jnp.maximum(diff, 0.0) * log_g), 0.0)
        q_decay = jnp.exp(q_steps * log_g)
        k_decay = jnp.exp(k_steps * log_g)
        chunk_decay = jnp.exp(jnp.full((1, 1), float(c), F32) * log_g)
        for sub in (reversed(subs) if reverse else subs):
            rows = slice(sub * c, (sub + 1) * c)
            qh = q_ref[rows, h * RET_QK_DIM:(h + 1) * RET_QK_DIM]
            kh = k_ref[rows, h * RET_QK_DIM:(h + 1) * RET_QK_DIM]
            vh = v_ref[rows, h * RET_V_DIM:(h + 1) * RET_V_DIM]
            s = s_ref[h]

            scores = lax.dot_general(qh, kh, (((1,), (1,)), ((), ())), preferred_element_type=F32) * intra
            o = (jnp.dot(scores.astype(BF16), vh, preferred_element_type=F32)
                 + jnp.dot(qh, s.astype(BF16), preferred_element_type=F32) * q_decay)
            k_dec_t = (kh.astype(F32) * k_decay).T.astype(BF16)
            s_ref[h] = s * chunk_decay + jnp.dot(k_dec_t, vh, preferred_element_type=F32)

            sl = slice(h * RET_V_DIM, (h + 1) * RET_V_DIM)
            if final:
                o = o + other_ref[rows, sl].astype(F32)
                mu = jnp.mean(o, axis=-1, keepdims=True)
                oc = o - mu
                var = jnp.mean(oc * oc, axis=-1, keepdims=True)
                o = oc * lax.rsqrt(var + NORM_EPS) * gate_ref[rows, sl].astype(F32)
            o_ref[rows, sl] = o.astype(o_ref.dtype)


def _retention_pass(lay, p, log_g, other=None, *, reverse):
    final = other is not None
    step_rows = RET_STEP_CHUNKS * RET_CHUNK
    assert lay.ctx_len % step_rows == 0 and lay.seq % step_rows == 0
    ctx_steps, lat_steps = lay.ctx_len // step_rows, lay.seq // step_rows

    def row_block(b, n):
        ctx_blk = b * ctx_steps + ((ctx_steps - 1 - n) if reverse else n)
        m = n - ctx_steps
        lat_blk = lay.batch * ctx_steps + b * lat_steps + ((lat_steps - 1 - m) if reverse else m)
        return jnp.where(n < ctx_steps, ctx_blk, lat_blk)

    in_specs = [
        pl.BlockSpec((step_rows, RET_QK), lambda b, n, lg: (row_block(b, n), Q0 // RET_QK)),
        pl.BlockSpec((step_rows, RET_QK), lambda b, n, lg: (row_block(b, n), K0 // RET_QK)),
        pl.BlockSpec((step_rows, RET_V), lambda b, n, lg: (row_block(b, n), V0 // RET_V)),
    ]
    args = [p, p, p]
    if final:
        in_specs += [
            pl.BlockSpec((step_rows, RET_V), lambda b, n, lg: (row_block(b, n), 0)),
            pl.BlockSpec((step_rows, RET_V), lambda b, n, lg: (row_block(b, n), G0 // RET_V)),
        ]
        args += [other, p]
    return pl.pallas_call(
        functools.partial(_retention_kernel, reverse=reverse, final=final),
        grid_spec=pltpu.PrefetchScalarGridSpec(
            num_scalar_prefetch=1,
            grid=(lay.batch, ctx_steps + lat_steps),
            in_specs=in_specs,
            out_specs=pl.BlockSpec((step_rows, RET_V), lambda b, n, lg: (row_block(b, n), 0)),
            scratch_shapes=[pltpu.VMEM((RET_HEADS, RET_QK_DIM, RET_V_DIM), F32)],
        ),
        out_shape=jax.ShapeDtypeStruct((lay.rows, RET_V), BF16),
        compiler_params=_params(2),
        name="retention_fwd_merge" if final else "retention_bwd",
    )(log_g, *args)


def _sgu_kernel(u_ref, v_ref, lng_ref, lnb_ref, ws_ref, bst_ref, o_ref):
    gw = SGU_WIDTH // SGU_GROUPS
    for c0 in range(0, o_ref.shape[0], CHUNK):
        rows = slice(c0, c0 + CHUNK)
        v = v_ref[rows, :].astype(F32)
        mu = jnp.mean(v, axis=-1, keepdims=True)
        vc = v - mu
        var = jnp.mean(vc * vc, axis=-1, keepdims=True)
        vn = (vc * lax.rsqrt(var + NORM_EPS) * lng_ref[...] + lnb_ref[...]).astype(BF16)
        for g in range(SGU_GROUPS):
            sl = slice(g * gw, (g + 1) * gw)
            s = jnp.dot(ws_ref[g].astype(BF16), vn[:, sl], preferred_element_type=F32) + bst_ref[:, g:g + 1]
            o_ref[rows, sl] = (u_ref[rows, sl].astype(F32) * s).astype(o_ref.dtype)


def _sgu(lay, uv, ln_g, ln_b, w_s, b_s_t):
    return pl.pallas_call(
        _sgu_kernel,
        grid=(lay.n_tiles,),
        in_specs=[
            pl.BlockSpec((lay.tm, SGU_WIDTH), lambda i: (i, UA0 // SGU_WIDTH)),
            pl.BlockSpec((lay.tm, SGU_WIDTH), lambda i: (i, VA0 // SGU_WIDTH)),
            pl.BlockSpec((1, SGU_WIDTH), lambda i: (0, 0)),
            pl.BlockSpec((1, SGU_WIDTH), lambda i: (0, 0)),
            pl.BlockSpec((SGU_GROUPS, CHUNK, CHUNK), lambda i: (0, 0, 0)),
            pl.BlockSpec((CHUNK, SGU_GROUPS), lambda i: (0, 0)),
        ],
        out_specs=pl.BlockSpec((lay.tm, SGU_WIDTH), lambda i: (i, 0)),
        out_shape=jax.ShapeDtypeStruct((lay.rows, SGU_WIDTH), BF16),
        compiler_params=_params(1),
        name="sgu",
    )(uv, uv, ln_g.reshape(1, -1), ln_b.reshape(1, -1), w_s, b_s_t)


def _merge_kernel(sgu_ref, ret_ref, wa_ref, wr_ref, ga_ref, gr_ref, o_ref):
    a = jnp.dot(sgu_ref[...], wa_ref[...].astype(BF16), preferred_element_type=F32)
    r = jnp.dot(ret_ref[...], wr_ref[...].astype(BF16), preferred_element_type=F32)
    o_ref[...] = (ga_ref[...].astype(F32) * a + gr_ref[...].astype(F32) * r).astype(o_ref.dtype)


def _merge(lay, sgu, ret, gates, w_proj_a, w_proj_r, layer):
    n = w_proj_a.shape[2]
    tn = 512
    return pl.pallas_call(
        _merge_kernel,
        grid=(n // tn, lay.n_tiles),
        in_specs=[
            pl.BlockSpec((lay.tm, SGU_WIDTH), lambda j, i: (i, 0)),
            pl.BlockSpec((lay.tm, RET_V), lambda j, i: (i, 0)),
            pl.BlockSpec((None, SGU_WIDTH, tn), lambda j, i: (layer, 0, j)),
            pl.BlockSpec((None, RET_V, tn), lambda j, i: (layer, 0, j)),
            pl.BlockSpec((lay.tm, tn), lambda j, i: (i, GA0 // tn + j)),
            pl.BlockSpec((lay.tm, tn), lambda j, i: (i, GR0 // tn + j)),
        ],
        out_specs=pl.BlockSpec((lay.tm, tn), lambda j, i: (i, j)),
        out_shape=jax.ShapeDtypeStruct((lay.rows, n), BF16),
        compiler_params=_params(2),
        name="merge",
    )(sgu, ret, w_proj_a, w_proj_r, gates, gates)


def _proj_resid_kernel(a_ref, w_ref, resid_ref, gate_ref, o_ref):
    acc = jnp.dot(a_ref[...], w_ref[...].astype(BF16), preferred_element_type=F32)
    o_ref[...] = resid_ref[...] + gate_ref[0] * acc


def _proj_resid(lay, a, w, w_index, resid, mod3, layer, k_gate):
    kdim = a.shape[1]
    n = w.shape[-1]
    tn = 512
    w_spec = pl.BlockSpec((None,) * len(w_index) + (kdim, tn), lambda j, i: (*w_index, 0, j))
    return pl.pallas_call(
        _proj_resid_kernel,
        grid=(n // tn, lay.n_tiles),
        in_specs=[
            pl.BlockSpec((lay.tm, kdim), lambda j, i: (i, 0)),
            w_spec,
            pl.BlockSpec((lay.tm, tn), lambda j, i: (i, j)),
            pl.BlockSpec((1, 1, tn), lambda j, i: (_mod_index(lay, layer, k_gate, i), 0, j)),
        ],
        out_specs=pl.BlockSpec((lay.tm, tn), lambda j, i: (i, j)),
        out_shape=jax.ShapeDtypeStruct((lay.rows, n), F32),
        compiler_params=_params(2),
        name="proj_resid",
    )(a, w, resid, mod3)


OUT_PROJ_SUB = 512


def _out_proj_kernel(a_ref, w_ref, resid_ref, gate_ref, *rest, with_norm):
    if with_norm:
        g_ref, shift_ref, scale_ref, o_ref, h_ref = rest
    else:
        (o_ref,) = rest
    a = a_ref[...]
    for c0 in range(0, o_ref.shape[1], OUT_PROJ_SUB):
        cols = slice(c0, c0 + OUT_PROJ_SUB)
        acc = jnp.dot(a, w_ref[:, cols].astype(BF16), preferred_element_type=F32)
        o_ref[:, cols] = resid_ref[:, cols] + gate_ref[0][:, cols] * acc
    if with_norm:
        x = o_ref[...]
        y = x * lax.rsqrt(jnp.mean(x * x, axis=-1, keepdims=True) + NORM_EPS) * g_ref[...]
        h_ref[...] = (y * (1.0 + scale_ref[0]) + shift_ref[0]).astype(h_ref.dtype)


def _out_proj(lay, a, w, layer, resid, mod3, k_gate, norm_g=None, k_shift=None, k_scale=None):
    kdim, n = w.shape[-2:]
    with_norm = norm_g is not None
    row = lambda width: pl.BlockSpec((lay.tm, width), lambda i: (i, 0))
    mod = lambda k: pl.BlockSpec((1, 1, n), lambda i: (_mod_index(lay, layer, k, i), 0, 0))
    in_specs = [
        row(kdim),
        pl.BlockSpec((None, kdim, n), lambda i: (layer, 0, 0), pipeline_mode=pl.Buffered(1)),
        row(n),
        mod(k_gate),
    ]
    args = [a, w, resid, mod3]
    x_shape = jax.ShapeDtypeStruct((lay.rows, n), F32)
    if with_norm:
        in_specs += [pl.BlockSpec((1, n), lambda i: (0, 0)), mod(k_shift), mod(k_scale)]
        args += [norm_g.reshape(1, n), mod3, mod3]
        out_specs, out_shape = [row(n), row(n)], [x_shape, jax.ShapeDtypeStruct((lay.rows, n), BF16)]
    else:
        out_specs, out_shape = row(n), x_shape
    return pl.pallas_call(
        functools.partial(_out_proj_kernel, with_norm=with_norm),
        grid=(lay.n_tiles,),
        in_specs=in_specs,
        out_specs=out_specs,
        out_shape=out_shape,
        compiler_params=_params(1),
        name="out_proj",
    )(*args)


def _swiglu_up_kernel(h_ref, w1_ref, w3_ref, o_ref):
    h = h_ref[...]
    for c0 in range(0, o_ref.shape[1], SWIGLU_SUB):
        cols = slice(c0, c0 + SWIGLU_SUB)
        a = jnp.dot(h, w1_ref[:, cols].astype(BF16), preferred_element_type=F32)
        b = jnp.dot(h, w3_ref[:, cols].astype(BF16), preferred_element_type=F32)
        o_ref[:, cols] = (_silu(a) * b).astype(o_ref.dtype)


def _swiglu_up(lay, h, w1, w3, w_index):
    d = h.shape[1]
    f = w1.shape[-1]
    tn = 512
    w_spec = pl.BlockSpec((None,) * len(w_index) + (d, tn), lambda j, i: (*w_index, 0, j))
    return pl.pallas_call(
        _swiglu_up_kernel,
        grid=(f // tn, lay.n_tiles),
        in_specs=[pl.BlockSpec((lay.tm, d), lambda j, i: (i, 0)), w_spec, w_spec],
        out_specs=pl.BlockSpec((lay.tm, tn), lambda j, i: (i, j)),
        out_shape=jax.ShapeDtypeStruct((lay.rows, f), BF16),
        compiler_params=_params(2),
        name="swiglu_up",
    )(h, w1, w3)


def _pack_bf16_pairs(y):
    half = y.shape[1] // 2
    lo = lax.bitcast_convert_type(y[:, :half].astype(BF16).astype(F32), jnp.uint32) >> 16
    hi = lax.bitcast_convert_type(y[:, half:].astype(BF16).astype(F32), jnp.uint32)
    return hi | lo


def _unpack_bf16_pairs(w):
    lo = lax.bitcast_convert_type(w << 16, F32).astype(BF16)
    hi = lax.bitcast_convert_type(w & jnp.uint32(0xFFFF0000), F32).astype(BF16)
    return jnp.concatenate([lo, hi], axis=1)


def _norm_router_kernel(x_ref, g_ref, shift_ref, scale_ref, rw_ref, rb_ref,
                        hp_ref, sel_ref, prob_ref, cnt_ref, run_ref):
    @pl.when(pl.program_id(0) == 0)
    def _():
        run_ref[...] = jnp.zeros_like(run_ref)

    x = x_ref[...]
    y = x * lax.rsqrt(jnp.mean(x * x, axis=-1, keepdims=True) + NORM_EPS) * g_ref[...]
    h = y * (1.0 + scale_ref[0]) + shift_ref[0]
    hp_ref[...] = _pack_bf16_pairs(h)

    logits = jnp.dot(h.astype(BF16), rw_ref[...], preferred_element_type=F32) + rb_ref[...]
    lane = lax.broadcasted_iota(jnp.int32, logits.shape, 1)
    neg = jnp.float32(-jnp.inf)
    lg = jnp.where(lane < N_EXPERTS, logits, neg)
    m1 = jnp.max(lg, axis=-1, keepdims=True)
    i1 = jnp.min(jnp.where(lg == m1, lane, LANES), axis=-1, keepdims=True)
    lg2 = jnp.where(lane == i1, neg, lg)
    m2 = jnp.max(lg2, axis=-1, keepdims=True)
    i2 = jnp.min(jnp.where(lg2 == m2, lane, LANES), axis=-1, keepdims=True)
    e2 = jnp.exp(m2 - m1)
    den = 1.0 + e2

    tm = x.shape[0]
    onehot = jnp.where(lane == i1, 1.0, 0.0) + jnp.where(lane == i2, 1.0, 0.0)
    earlier = jnp.where(lax.broadcasted_iota(jnp.int32, (tm, tm), 0) > lax.broadcasted_iota(jnp.int32, (tm, tm), 1),
                        1.0, 0.0).astype(BF16)
    before = jnp.dot(earlier, onehot.astype(BF16), preferred_element_type=F32) + run_ref[0:1, :]
    pos1 = jnp.sum(jnp.where(lane == i1, before, 0.0), axis=-1, keepdims=True).astype(jnp.int32)
    pos2 = jnp.sum(jnp.where(lane == i2, before, 0.0), axis=-1, keepdims=True).astype(jnp.int32)
    run_ref[...] = run_ref[...] + jnp.sum(onehot, axis=0, keepdims=True)
    cnt_ref[...] = run_ref[...]

    sel_ref[...] = jnp.where(lane == 0, i1, jnp.where(lane == 1, i2, jnp.where(lane == 2, pos1,
                                                                                jnp.where(lane == 3, pos2, 0))))
    prob_ref[...] = jnp.where(lane == 0, 1.0 / den, jnp.where(lane == 1, e2 / den, 0.0))


def _norm_router(lay, x_all, g, mod3, layer, k_shift, k_scale, router_w, router_b):
    d = x_all.shape[1]
    w = jnp.zeros((d, LANES), BF16).at[:, :N_EXPERTS].set(router_w.astype(BF16))
    b = jnp.zeros((1, LANES), F32).at[0, :N_EXPERTS].set(router_b.astype(F32))
    lane_block = pl.BlockSpec((lay.tm, LANES), lambda i: (i, 0))
    return pl.pallas_call(
        _norm_router_kernel,
        grid=(lay.n_tiles,),
        in_specs=[
            pl.BlockSpec((lay.tm, d), lambda i: (i, 0)),
            pl.BlockSpec((1, d), lambda i: (0, 0)),
            pl.BlockSpec((1, 1, d), lambda i: (_mod_index(lay, layer, k_shift, i), 0, 0)),
            pl.BlockSpec((1, 1, d), lambda i: (_mod_index(lay, layer, k_scale, i), 0, 0)),
            pl.BlockSpec((d, LANES), lambda i: (0, 0)),
            pl.BlockSpec((1, LANES), lambda i: (0, 0)),
        ],
        out_specs=[
            pl.BlockSpec((lay.tm, d // 2), lambda i: (i, 0)),
            lane_block,
            lane_block,
            pl.BlockSpec((8, LANES), lambda i: (0, 0)),
        ],
        out_shape=[
            jax.ShapeDtypeStruct((lay.rows, d // 2), jnp.uint32),
            jax.ShapeDtypeStruct((lay.rows, LANES), jnp.int32),
            jax.ShapeDtypeStruct((lay.rows, LANES), F32),
            jax.ShapeDtypeStruct((8, LANES), F32),
        ],
        scratch_shapes=[pltpu.VMEM((8, LANES), F32)],
        compiler_params=_params(1),
        name="norm_router",
    )(x_all, g.reshape(1, d), mod3, mod3, w, b)


def _routing_plan(sel, cnt, n_sorted_rows):
    counts = cnt[0, :N_EXPERTS].astype(jnp.int32)
    padded = (counts + MOE_TILE - 1) // MOE_TILE * MOE_TILE
    ends = jnp.cumsum(padded)
    starts = ends - padded
    dest1 = starts[sel[:, 0]] + sel[:, 2]
    dest2 = starts[sel[:, 1]] + sel[:, 3]
    tile_start = jnp.arange(n_sorted_rows // MOE_TILE, dtype=jnp.int32) * MOE_TILE
    tile_expert = jnp.minimum(jnp.sum(tile_start[:, None] >= ends[None, :], axis=1), N_EXPERTS - 1)
    n_used_tiles = (ends[-1:] // MOE_TILE).astype(jnp.int32)
    present = counts > 0
    run_expert = jnp.argsort(jnp.logical_not(present), stable=True).astype(jnp.int32)
    run_of_expert = jnp.cumsum(present.astype(jnp.int32)) - 1
    tile_run = jnp.maximum(run_of_expert[tile_expert], 0).astype(jnp.int32)
    run_first_tile = (starts[run_expert] // MOE_TILE).astype(jnp.int32)
    n_runs = jnp.sum(present.astype(jnp.int32)).reshape(1)
    fill_lo = jnp.concatenate([starts + counts, ends[-1:]]).astype(jnp.int32)
    fill_hi = jnp.concatenate([ends, jnp.full((1,), n_sorted_rows, jnp.int32)]).astype(jnp.int32)
    return dest1, dest2, (fill_lo, fill_hi), (tile_run, run_first_tile, run_expert, n_runs, n_used_tiles)


def _dispatch_kernel(dest1_ref, dest2_ref, fill_lo_ref, fill_hi_ref, h_ref, xs_ref, zero_ref, sem, zero_sem):
    i = pl.program_id(0)
    tm = h_ref.shape[0]
    base = i * tm

    def row_copy(t, d):
        return pltpu.make_async_copy(h_ref.at[pl.ds(t, 1)], xs_ref.at[pl.ds(d, 1)], sem)

    def zero_copy(d):
        return pltpu.make_async_copy(zero_ref, xs_ref.at[pl.ds(d, 1)], zero_sem)

    @pl.when(i == 0)
    def _():
        zero_ref[...] = jnp.zeros_like(zero_ref)
        for r in range(N_EXPERTS + 1):
            lo, hi = fill_lo_ref[r], fill_hi_ref[r]

            def start_zero(d, carry, lo=lo):
                zero_copy(d).start()

                @pl.when(d - lo >= ZERO_FILL_WINDOW)
                def _():
                    zero_copy(0).wait()

                return carry

            def wait_zero(d, carry):
                zero_copy(0).wait()
                return carry

            lax.fori_loop(lo, hi, start_zero, 0)
            lax.fori_loop(0, jnp.minimum(hi - lo, ZERO_FILL_WINDOW), wait_zero, 0)

    def start(q, carry):
        for u in range(ROW_DMA_UNROLL):
            t = q * ROW_DMA_UNROLL + u
            row_copy(t, dest1_ref[base + t]).start(priority=0)
            row_copy(t, dest2_ref[base + t]).start(priority=1)
        return carry

    def wait(q, carry):
        for _ in range(2 * ROW_DMA_UNROLL):
            row_copy(0, 0).wait()
        return carry

    lax.fori_loop(0, tm // ROW_DMA_UNROLL, start, 0)
    lax.fori_loop(0, tm // ROW_DMA_UNROLL, wait, 0)


def _dispatch(lay, hp, dest1, dest2, fill_lo, fill_hi, n_sorted_rows):
    width = hp.shape[1]
    return pl.pallas_call(
        _dispatch_kernel,
        grid_spec=pltpu.PrefetchScalarGridSpec(
            num_scalar_prefetch=4,
            grid=(lay.n_tiles,),
            in_specs=[pl.BlockSpec((lay.tm, width), lambda i, *_: (i, 0))],
            out_specs=pl.BlockSpec(memory_space=pl.ANY),
            scratch_shapes=[
                pltpu.VMEM((1, width), hp.dtype),
                pltpu.SemaphoreType.DMA(()),
                pltpu.SemaphoreType.DMA(()),
            ],
        ),
        out_shape=jax.ShapeDtypeStruct((n_sorted_rows, width), hp.dtype),
        compiler_params=_params(1),
        name="moe_dispatch",
    )(dest1, dest2, fill_lo, fill_hi, hp)


def _grouped_kernel(tile_run_ref, run_first_ref, run_expert_ref, n_runs_ref, n_used_ref, a_ref, *rest,
                    moe_layer, tn, swiglu):
    n_mats = 2 if swiglu else 1
    w_hbm = rest[:n_mats]
    o_ref, wbuf, sems = rest[n_mats:]
    j, i = pl.program_id(0), pl.program_id(1)
    n_runs = n_runs_ref[0]
    run = tile_run_ref[i]
    seq = j * n_runs + run
    slot = lax.rem(seq, 2)

    def fetches(jj, rr, slot_):
        expert = run_expert_ref[rr]
        col = pl.multiple_of(jj * tn, tn)
        return [pltpu.make_async_copy(w.at[moe_layer, expert, :, pl.ds(col, tn)], wbuf.at[slot_, m], sems.at[slot_, m])
                for m, w in enumerate(w_hbm)]

    live = i < n_used_ref[0]

    @pl.when(live & (i == run_first_ref[run]))
    def _():
        @pl.when(seq == 0)
        def _():
            for c in fetches(j, run, slot):
                c.start()

        for c in fetches(j, run, slot):
            c.wait()
        wraps = run + 1 == n_runs
        next_j = jnp.where(wraps, j + 1, j)
        next_run = jnp.where(wraps, 0, run + 1)

        @pl.when(next_j < pl.num_programs(0))
        def _():
            for c in fetches(next_j, next_run, 1 - slot):
                c.start()

    @pl.when(live)
    def _():
        if swiglu:
            h = _unpack_bf16_pairs(a_ref[...])
            for c0 in range(0, tn, SWIGLU_SUB):
                cols = slice(c0, c0 + SWIGLU_SUB)
                a = jnp.dot(h, wbuf[slot, 0, :, cols].astype(BF16), preferred_element_type=F32)
                b = jnp.dot(h, wbuf[slot, 1, :, cols].astype(BF16), preferred_element_type=F32)
                o_ref[:, cols] = (_silu(a) * b).astype(o_ref.dtype)
        else:
            a = a_ref[...]
            for c0 in range(0, tn, GROUPED_DOWN_SUB):
                cols = slice(c0, c0 + GROUPED_DOWN_SUB)
                o_ref[:, cols] = jnp.dot(a, wbuf[slot, 0, :, cols].astype(BF16), preferred_element_type=F32)

    @pl.when(jnp.logical_not(live))
    def _():
        o_ref[...] = jnp.zeros_like(o_ref)


def _grouped_matmul(a, weights, moe_layer, plan, out_dtype, *, swiglu, name):
    n_rows, a_width = a.shape
    kdim, n = weights[0].shape[-2:]
    tn = 512 if swiglu else 1024
    n_prefetch = 5
    idx = lambda j, i, *_: (i, 0)
    return pl.pallas_call(
        functools.partial(_grouped_kernel, moe_layer=moe_layer, tn=tn, swiglu=swiglu),
        grid_spec=pltpu.PrefetchScalarGridSpec(
            num_scalar_prefetch=n_prefetch,
            grid=(n // tn, n_rows // MOE_TILE),
            in_specs=[pl.BlockSpec((MOE_TILE, a_width), idx)] + [pl.BlockSpec(memory_space=pl.ANY)] * len(weights),
            out_specs=pl.BlockSpec((MOE_TILE, tn), lambda j, i, *_: (i, j)),
            scratch_shapes=[
                pltpu.VMEM((2, len(weights), kdim, tn), F32),
                pltpu.SemaphoreType.DMA((2, len(weights))),
            ],
        ),
        out_shape=jax.ShapeDtypeStruct((n_rows, n), out_dtype),
        compiler_params=_params(2),
        name=name,
    )(*plan, a, *weights)


def _combine_kernel(dest1_ref, dest2_ref, x_ref, prob_ref, gate_ref, ys_ref, *rest, mode, first_tile):
    if mode == "final":
        g_ref, o_ref, buf, sems = rest
    else:
        g_ref, shift_ref, scale_ref, o_ref, h_ref, buf, sems = rest
    tm = x_ref.shape[0]
    i = pl.program_id(0)

    def row_copy(slot, k, t, d):
        return pltpu.make_async_copy(ys_ref.at[pl.ds(d, 1)], buf.at[slot, k, pl.ds(t, 1)], sems.at[slot, k])

    def start_tile(step):
        slot = lax.rem(step, 2)
        base = (step + first_tile) * tm

        def start(q, carry):
            for u in range(ROW_DMA_UNROLL):
                t = q * ROW_DMA_UNROLL + u
                row_copy(slot, 0, t, dest1_ref[base + t]).start(priority=0)
                row_copy(slot, 1, t, dest2_ref[base + t]).start(priority=1)
            return carry

        lax.fori_loop(0, tm // ROW_DMA_UNROLL, start, 0)

    @pl.when(i == 0)
    def _():
        start_tile(0)

    @pl.when(i + 1 < pl.num_programs(0))
    def _():
        start_tile(i + 1)

    slot = lax.rem(i, 2)

    def wait(t, carry):
        row_copy(slot, 0, 0, 0).wait()
        row_copy(slot, 1, 0, 0).wait()
        return carry

    lax.fori_loop(0, tm, wait, 0, unroll=2)
    p = prob_ref[...]
    f = p[:, 0:1] * buf[slot, 0] + p[:, 1:2] * buf[slot, 1]
    x = x_ref[...] + gate_ref[0] * f
    y = x * lax.rsqrt(jnp.mean(x * x, axis=-1, keepdims=True) + NORM_EPS) * g_ref[...]
    if mode == "final":
        o_ref[...] = y
    else:
        o_ref[...] = x
        h_ref[...] = (y * (1.0 + scale_ref[0]) + shift_ref[0]).astype(h_ref.dtype)


def _combine(lay, x_all, prob, ys, dest1, dest2, mod3, layer, k_gate, norm_g, *, final):
    d = x_all.shape[1]
    first_tile = 1 if final else 0
    tile = lambda i, d1, d2: (i + first_tile, 0)
    mod = lambda lyr, k: (lambda i, d1, d2: (_mod_index(lay, lyr, k, i + first_tile), 0, 0))
    row_spec = pl.BlockSpec((lay.tm, d), lambda i, d1, d2: (i, 0))
    in_specs = [
        pl.BlockSpec((lay.tm, d), tile),
        pl.BlockSpec((lay.tm, LANES), tile),
        pl.BlockSpec((1, 1, d), mod(layer, k_gate)),
        pl.BlockSpec(memory_space=pl.ANY),
        pl.BlockSpec((1, d), lambda i, d1, d2: (0, 0)),
    ]
    args = [x_all, prob, mod3, ys, norm_g.reshape(1, d)]
    n_tiles = lay.n_tiles - first_tile
    if final:
        out_specs = row_spec
        out_shape = jax.ShapeDtypeStruct((n_tiles * lay.tm, d), F32)
    else:
        in_specs += [pl.BlockSpec((1, 1, d), mod(layer + 1, 0)), pl.BlockSpec((1, 1, d), mod(layer + 1, 1))]
        args += [mod3, mod3]
        out_specs = [row_spec, row_spec]
        out_shape = [jax.ShapeDtypeStruct((lay.rows, d), F32), jax.ShapeDtypeStruct((lay.rows, d), BF16)]
    return pl.pallas_call(
        functools.partial(_combine_kernel, mode="final" if final else "next", first_tile=first_tile),
        grid_spec=pltpu.PrefetchScalarGridSpec(
            num_scalar_prefetch=2,
            grid=(n_tiles,),
            in_specs=in_specs,
            out_specs=out_specs,
            scratch_shapes=[pltpu.VMEM((2, 2, lay.tm, d), F32), pltpu.SemaphoreType.DMA((2, 2))],
        ),
        out_shape=out_shape,
        compiler_params=_params(1),
        name="moe_combine",
    )(dest1, dest2, *args)


def _moe_ffn(lay, x_all, norm_g, mod3, layer, router_w, router_b, w1, w3, w2, moe_layer, next_norm_g, *, final):
    hp, sel, prob, cnt = _norm_router(lay, x_all, norm_g, mod3, layer, 3, 4, router_w, router_b)
    n_sorted_rows = 2 * lay.rows + N_EXPERTS * MOE_TILE
    dest1, dest2, fill, plan = _routing_plan(sel, cnt, n_sorted_rows)
    xs = _dispatch(lay, hp, dest1, dest2, *fill, n_sorted_rows)
    act = _grouped_matmul(xs, (w1, w3), moe_layer, plan, BF16, swiglu=True, name="moe_up")
    ys = _grouped_matmul(act, (w2,), moe_layer, plan, F32, swiglu=False, name="moe_down")
    return _combine(lay, x_all, prob, ys, dest1, dest2, mod3, layer, 5, next_norm_g, final=final)


def _rope_tables(lay):
    t = np.arange(lay.seq)
    quarter = RET_QK_DIM // 4
    freqs = np.float32(ROPE_BASE) ** (-np.arange(quarter, dtype=np.float32) / np.float32(quarter))

    def tables(pos):
        ang = pos.astype(np.float32)[:, None] * freqs[None, :]
        cos, sin = np.cos(ang), np.sin(ang)
        return np.concatenate([cos, cos], axis=-1), np.concatenate([-sin, sin], axis=-1)

    cos_r, sin_r = tables(t // GRID_W)
    cos_c, sin_c = tables(t % GRID_W)
    cos_t = np.concatenate([np.ones((lay.tm, RET_QK_DIM), np.float32), np.concatenate([cos_r, cos_c], -1)], axis=0)
    sin_t = np.concatenate([np.zeros((lay.tm, RET_QK_DIM), np.float32), np.concatenate([sin_r, sin_c], -1)], axis=0)
    return jnp.asarray(cos_t, F32), jnp.asarray(sin_t, F32)


def kernel(x, c, ctx, c_ctx, w_mod, b_mod, norm1_g, norm2_g, w_in, sgu_ln_g, sgu_ln_b, sgu_w, sgu_b,
           ret_log_decay, w_proj_a, w_proj_r, w_out, ffn_w1, ffn_w3, ffn_w2, router_w, router_b,
           moe_w1, moe_w3, moe_w2, final_norm_g):
    batch, seq, d = x.shape
    ctx_len = ctx.shape[1]
    depth = w_mod.shape[0]
    lay = _Layout(batch, seq, ctx_len)

    cvec = jnp.zeros((MOD_ROWS, d), F32).at[:batch].set(c).at[batch].set(c_ctx)
    mod = _modulation(cvec, w_mod, b_mod)
    mod3 = mod.reshape(depth * MOD_ROWS * N_MOD, 1, d)

    cos_t, sin_t = _rope_tables(lay)
    log_decay = ret_log_decay.astype(F32)

    x_all, h = _join_norm_modulate(lay, ctx.reshape(batch * ctx_len, d), x.reshape(batch * seq, d),
                                   norm1_g[0], mod3, 0, 0, 1)
    out = None
    for layer in range(depth):
        if h is None:
            h = _norm_modulate(lay, x_all, norm1_g[layer], mod3, layer, 0, 1, BF16)
        p = _in_proj(lay, h, w_in, layer, cos_t, sin_t)
        o_bwd = _retention_pass(lay, p, log_decay[layer, 1], reverse=True)
        ret = _retention_pass(lay, p, log_decay[layer, 0], o_bwd, reverse=False)
        sgu = _sgu(lay, p, sgu_ln_g[layer], sgu_ln_b[layer], sgu_w[layer], sgu_b[layer].T)
        y = _merge(lay, sgu, ret, p, w_proj_a, w_proj_r, layer)
        i = layer // 2
        last = layer == depth - 1
        h = None
        if layer % 2 == 0:
            x_all, h2 = _out_proj(lay, y, w_out, layer, x_all, mod3, 2, norm2_g[layer], 3, 4)
            act = _swiglu_up(lay, h2, ffn_w1, ffn_w3, (i,))
            x_all = _proj_resid(lay, act, ffn_w2, (i,), x_all, mod3, layer, 5)
        else:
            x_all = _out_proj(lay, y, w_out, layer, x_all, mod3, 2)
            res = _moe_ffn(lay, x_all, norm2_g[layer], mod3, layer, router_w[i], router_b[i], moe_w1, moe_w3, moe_w2,
                           i, final_norm_g if last else norm1_g[layer + 1], final=last)
            if last:
                out = res
            else:
                x_all, h = res

    if out is None:
        out = _final_norm(lay, x_all, final_norm_g)
    return out.reshape(batch, seq, d)
```

```python
import functools

import jax
import jax.numpy as jnp
import numpy as np
from jax import lax
from jax.experimental import pallas as pl
from jax.experimental.pallas import tpu as pltpu

F32 = jnp.float32
BF16 = jnp.bfloat16

GRID_W = 64
CHUNK = 128
SGU_GROUPS = 8
SGU_WIDTH = 2048
RET_HEADS = 8
RET_QK_DIM = 256
RET_V_DIM = 512
RET_QK = RET_HEADS * RET_QK_DIM
RET_V = RET_HEADS * RET_V_DIM
ROPE_BASE = 10000.0
N_EXPERTS = 8
N_MOD = 6
NORM_EPS = 1e-6
MOD_ROWS = 8
LANES = 128
MOE_TILE = 512
ZERO_FILL_WINDOW = 256
ROW_DMA_UNROLL = 4
GROUPED_DOWN_SUB = 512
SWIGLU_SUB = 256

VMEM_LIMIT_BYTES = 52 * 1024 * 1024

Q0 = 0
K0 = Q0 + RET_QK
V0 = K0 + RET_QK
G0 = V0 + RET_V
UA0 = G0 + RET_V
VA0 = UA0 + SGU_WIDTH
GA0 = VA0 + SGU_WIDTH
GR0 = GA0 + 2048
IN_COLS = GR0 + 2048


def _params(n_grid_dims):
    return pltpu.CompilerParams(
        dimension_semantics=("arbitrary",) * n_grid_dims,
        vmem_limit_bytes=VMEM_LIMIT_BYTES,
    )


def _gelu_tanh(x):
    c = 0.7978845608028654
    half_x = 0.5 * x
    return half_x * jnp.tanh(x * (c + (c * 0.044715) * (x * x))) + half_x


def _sigmoid(x):
    return 0.5 * jnp.tanh(0.5 * x) + 0.5


def _silu(x):
    return x * _sigmoid(x)


def _mod_kernel(c_ref, w_ref, b_ref, o_ref):
    sc = _silu(c_ref[...]).astype(BF16)
    o_ref[...] = jnp.dot(sc, w_ref[...].astype(BF16), preferred_element_type=F32) + b_ref[...]


def _modulation(cvec, w_mod, b_mod):
    depth, d, n = w_mod.shape
    tn = 1024
    return pl.pallas_call(
        _mod_kernel,
        grid=(depth, n // tn),
        in_specs=[
            pl.BlockSpec((MOD_ROWS, d), lambda l, j: (0, 0)),
            pl.BlockSpec((None, d, tn), lambda l, j: (l, 0, j)),
            pl.BlockSpec((None, 1, tn), lambda l, j: (l, 0, j)),
        ],
        out_specs=pl.BlockSpec((None, MOD_ROWS, tn), lambda l, j: (l, 0, j)),
        out_shape=jax.ShapeDtypeStruct((depth, MOD_ROWS, n), F32),
        compiler_params=_params(2),
        name="modulation",
    )(cvec, w_mod, b_mod.reshape(depth, 1, n))


def _norm_mod_kernel(x_ref, g_ref, shift_ref, scale_ref, o_ref):
    x = x_ref[...]
    y = x * lax.rsqrt(jnp.mean(x * x, axis=-1, keepdims=True) + NORM_EPS) * g_ref[...]
    o_ref[...] = (y * (1.0 + scale_ref[0]) + shift_ref[0]).astype(o_ref.dtype)


def _norm_kernel(x_ref, g_ref, o_ref):
    x = x_ref[...]
    y = x * lax.rsqrt(jnp.mean(x * x, axis=-1, keepdims=True) + NORM_EPS) * g_ref[...]
    o_ref[...] = y.astype(o_ref.dtype)


class _Layout:
    def __init__(self, batch, seq, ctx_len):
        self.batch, self.seq, self.ctx_len = batch, seq, ctx_len
        self.tm = batch * ctx_len
        assert seq % self.tm == 0 and ctx_len % CHUNK == 0 and seq % CHUNK == 0
        self.tiles_per_batch = seq // self.tm
        self.n_tiles = 1 + batch * self.tiles_per_batch
        self.rows = self.n_tiles * self.tm
        self.ctx_chunks = ctx_len // CHUNK
        self.lat_chunks = seq // CHUNK

    def mod_row(self, tile):
        return jnp.where(tile == 0, self.batch, (tile - 1) // self.tiles_per_batch)


def _mod_index(lay, layer, k, tile):
    return layer * MOD_ROWS * N_MOD + lay.mod_row(tile) * N_MOD + k


def _norm_modulate(lay, x_all, g, mod3, layer, k_shift, k_scale, out_dtype):
    d = x_all.shape[1]
    return pl.pallas_call(
        _norm_mod_kernel,
        grid=(lay.n_tiles,),
        in_specs=[
            pl.BlockSpec((lay.tm, d), lambda i: (i, 0)),
            pl.BlockSpec((1, d), lambda i: (0, 0)),
            pl.BlockSpec((1, 1, d), lambda i: (_mod_index(lay, layer, k_shift, i), 0, 0)),
            pl.BlockSpec((1, 1, d), lambda i: (_mod_index(lay, layer, k_scale, i), 0, 0)),
        ],
        out_specs=pl.BlockSpec((lay.tm, d), lambda i: (i, 0)),
        out_shape=jax.ShapeDtypeStruct((lay.rows, d), out_dtype),
        compiler_params=_params(1),
        name="norm_modulate",
    )(x_all, g.reshape(1, d), mod3, mod3)


def _join_norm_kernel(ctx_ref, x_ref, g_ref, shift_ref, scale_ref, xall_ref, h_ref):
    def emit(src_ref):
        x = src_ref[...]
        xall_ref[...] = x
        y = x * lax.rsqrt(jnp.mean(x * x, axis=-1, keepdims=True) + NORM_EPS) * g_ref[...]
        h_ref[...] = (y * (1.0 + scale_ref[0]) + shift_ref[0]).astype(h_ref.dtype)

    @pl.when(pl.program_id(0) == 0)
    def _():
        emit(ctx_ref)

    @pl.when(pl.program_id(0) > 0)
    def _():
        emit(x_ref)


def _join_norm_modulate(lay, ctx2d, x2d, g, mod3, layer, k_shift, k_scale):
    d = x2d.shape[1]
    row = pl.BlockSpec((lay.tm, d), lambda i: (i, 0))
    return pl.pallas_call(
        _join_norm_kernel,
        grid=(lay.n_tiles,),
        in_specs=[
            pl.BlockSpec((lay.tm, d), lambda i: (0, 0)),
            pl.BlockSpec((lay.tm, d), lambda i: (jnp.maximum(i - 1, 0), 0)),
            pl.BlockSpec((1, d), lambda i: (0, 0)),
            pl.BlockSpec((1, 1, d), lambda i: (_mod_index(lay, layer, k_shift, i), 0, 0)),
            pl.BlockSpec((1, 1, d), lambda i: (_mod_index(lay, layer, k_scale, i), 0, 0)),
        ],
        out_specs=[row, row],
        out_shape=[jax.ShapeDtypeStruct((lay.rows, d), F32), jax.ShapeDtypeStruct((lay.rows, d), BF16)],
        compiler_params=_params(1),
        name="join_norm_modulate",
    )(ctx2d, x2d, g.reshape(1, d), mod3, mod3)


def _final_norm(lay, x_all, g):
    d = x_all.shape[1]
    n_lat_tiles = lay.n_tiles - 1
    return pl.pallas_call(
        _norm_kernel,
        grid=(n_lat_tiles,),
        in_specs=[
            pl.BlockSpec((lay.tm, d), lambda i: (i + 1, 0)),
            pl.BlockSpec((1, d), lambda i: (0, 0)),
        ],
        out_specs=pl.BlockSpec((lay.tm, d), lambda i: (i, 0)),
        out_shape=jax.ShapeDtypeStruct((n_lat_tiles * lay.tm, d), F32),
        compiler_params=_params(1),
        name="final_norm",
    )(x_all, g.reshape(1, d))


IN_PROJ_TN = 2048
IN_PROJ_SUB = 512


IN_PROJ_KINDS = ((V0, "rope"), (G0, "plain"), (UA0, "silu"), (GA0, "gelu"), (IN_COLS, "sigmoid"))


def _in_proj_kernel(h_ref, w_ref, cos_ref, sin_ref, o_ref):
    col = pl.program_id(0) * IN_PROJ_TN
    start = 0
    for end, kind in IN_PROJ_KINDS:
        @pl.when((col >= start) & (col < end))
        def _(kind=kind):
            _in_proj_tile(h_ref, w_ref, cos_ref, sin_ref, o_ref, kind)

        start = end


def _in_proj_tile(h_ref, w_ref, cos_ref, sin_ref, o_ref, kind):
    h = h_ref[...]
    if kind == "rope":
        scale = jnp.where(pl.program_id(0) == 0, 1.0, RET_QK_DIM ** -0.5).astype(F32)
    for c0 in range(0, IN_PROJ_TN, IN_PROJ_SUB):
        acc = jnp.dot(h, w_ref[:, c0:c0 + IN_PROJ_SUB].astype(BF16), preferred_element_type=F32)
        if kind == "rope":
            for s in range(0, IN_PROJ_SUB, LANES):
                t = (c0 + s) % RET_QK_DIM
                xs = acc[:, s:s + LANES]
                r = xs * cos_ref[:, t:t + LANES] + pltpu.roll(xs, LANES // 2, 1) * sin_ref[:, t:t + LANES]
                o_ref[:, c0 + s:c0 + s + LANES] = (r * scale).astype(o_ref.dtype)
        else:
            if kind == "silu":
                acc = _silu(acc)
            elif kind == "gelu":
                acc = _gelu_tanh(acc)
            elif kind == "sigmoid":
                acc = _sigmoid(acc)
            else:
                assert kind == "plain"
            o_ref[:, c0:c0 + IN_PROJ_SUB] = acc.astype(o_ref.dtype)


def _in_proj(lay, h, w_in, layer, cos_t, sin_t):
    d = h.shape[1]
    n = w_in.shape[2]
    tn = IN_PROJ_TN
    n_rope_tiles = V0 // tn

    def table_block(j, i):
        blk = jnp.where(i == 0, 0, 1 + lax.rem(i - 1, lay.tiles_per_batch))
        return (jnp.where(j < n_rope_tiles, blk, 0), 0)

    table_spec = pl.BlockSpec((lay.tm, RET_QK_DIM), table_block)
    return pl.pallas_call(
        _in_proj_kernel,
        grid=(n // tn, lay.n_tiles),
        in_specs=[
            pl.BlockSpec((lay.tm, d), lambda j, i: (i, 0)),
            pl.BlockSpec((None, d, tn), lambda j, i: (layer, 0, j)),
            table_spec,
            table_spec,
        ],
        out_specs=pl.BlockSpec((lay.tm, tn), lambda j, i: (i, j)),
        out_shape=jax.ShapeDtypeStruct((lay.rows, n), BF16),
        compiler_params=_params(2),
        name="in_proj",
    )(h, w_in, cos_t, sin_t)


RET_STEP_CHUNKS = 1
RET_CHUNK = 256


def _retention_kernel(lg_ref, q_ref, k_ref, v_ref, *rest, reverse, final):
    if final:
        other_ref, gate_ref, o_ref, s_ref = rest
    else:
        o_ref, s_ref = rest
    c = RET_CHUNK

    @pl.when(pl.program_id(1) == 0)
    def _():
        s_ref[...] = jnp.zeros_like(s_ref)

    qi = lax.broadcasted_iota(jnp.int32, (c, c), 0)
    kj = lax.broadcasted_iota(jnp.int32, (c, c), 1)
    diff = ((kj - qi) if reverse else (qi - kj)).astype(F32)
    pos = lax.broadcasted_iota(jnp.int32, (c, 1), 0).astype(F32)
    q_steps = (c - pos) if reverse else (pos + 1.0)
    k_steps = pos if reverse else (c - 1.0 - pos)

    subs = range(RET_STEP_CHUNKS)
    for h in range(RET_HEADS):
        log_g = lg_ref[h]
        intra = jnp.where(diff >= 0, jnp.exp(jnp.maximum(diff, 0.0) * log_g), 0.0)
        q_decay = jnp.exp(q_steps * log_g)
        k_decay = jnp.exp(k_steps * log_g)
        chunk_decay = jnp.exp(jnp.full((1, 1), float(c), F32) * log_g)
        for sub in (reversed(subs) if reverse else subs):
            rows = slice(sub * c, (sub + 1) * c)
            qh = q_ref[rows, h * RET_QK_DIM:(h + 1) * RET_QK_DIM]
            kh = k_ref[rows, h * RET_QK_DIM:(h + 1) * RET_QK_DIM]
            vh = v_ref[rows, h * RET_V_DIM:(h + 1) * RET_V_DIM]
            s = s_ref[h]

            scores = lax.dot_general(qh, kh, (((1,), (1,)), ((), ())), preferred_element_type=F32) * intra
            o = (jnp.dot(scores.astype(BF16), vh, preferred_element_type=F32)
                 + jnp.dot(qh, s.astype(BF16), preferred_element_type=F32) * q_decay)
            k_dec_t = (kh.astype(F32) * k_decay).T.astype(BF16)
            s_ref[h] = s * chunk_decay + jnp.dot(k_dec_t, vh, preferred_element_type=F32)

            sl = slice(h * RET_V_DIM, (h + 1) * RET_V_DIM)
            if final:
                o = o + other_ref[rows, sl].astype(F32)
                mu = jnp.mean(o, axis=-1, keepdims=True)
                oc = o - mu
                var = jnp.mean(oc * oc, axis=-1, keepdims=True)
                o = oc * lax.rsqrt(var + NORM_EPS) * gate_ref[rows, sl].astype(F32)
            o_ref[rows, sl] = o.astype(o_ref.dtype)


def _retention_pass(lay, p, log_g, other=None, *, reverse):
    final = other is not None
    step_rows = RET_STEP_CHUNKS * RET_CHUNK
    assert lay.ctx_len % step_rows == 0 and lay.seq % step_rows == 0
    ctx_steps, lat_steps = lay.ctx_len // step_rows, lay.seq // step_rows

    def row_block(b, n):
        ctx_blk = b * ctx_steps + ((ctx_steps - 1 - n) if reverse else n)
        m = n - ctx_steps
        lat_blk = lay.batch * ctx_steps + b * lat_steps + ((lat_steps - 1 - m) if reverse else m)
        return jnp.where(n < ctx_steps, ctx_blk, lat_blk)

    in_specs = [
        pl.BlockSpec((step_rows, RET_QK), lambda b, n, lg: (row_block(b, n), Q0 // RET_QK)),
        pl.BlockSpec((step_rows, RET_QK), lambda b, n, lg: (row_block(b, n), K0 // RET_QK)),
        pl.BlockSpec((step_rows, RET_V), lambda b, n, lg: (row_block(b, n), V0 // RET_V)),
    ]
    args = [p, p, p]
    if final:
        in_specs += [
            pl.BlockSpec((step_rows, RET_V), lambda b, n, lg: (row_block(b, n), 0)),
            pl.BlockSpec((step_rows, RET_V), lambda b, n, lg: (row_block(b, n), G0 // RET_V)),
        ]
        args += [other, p]
    return pl.pallas_call(
        functools.partial(_retention_kernel, reverse=reverse, final=final),
        grid_spec=pltpu.PrefetchScalarGridSpec(
            num_scalar_prefetch=1,
            grid=(lay.batch, ctx_steps + lat_steps),
            in_specs=in_specs,
            out_specs=pl.BlockSpec((step_rows, RET_V), lambda b, n, lg: (row_block(b, n), 0)),
            scratch_shapes=[pltpu.VMEM((RET_HEADS, RET_QK_DIM, RET_V_DIM), F32)],
        ),
        out_shape=jax.ShapeDtypeStruct((lay.rows, RET_V), BF16),
        compiler_params=_params(2),
        name="retention_fwd_merge" if final else "retention_bwd",
    )(log_g, *args)


def _sgu_kernel(u_ref, v_ref, lng_ref, lnb_ref, ws_ref, bst_ref, o_ref):
    gw = SGU_WIDTH // SGU_GROUPS
    for c0 in range(0, o_ref.shape[0], CHUNK):
        rows = slice(c0, c0 + CHUNK)
        v = v_ref[rows, :].astype(F32)
        mu = jnp.mean(v, axis=-1, keepdims=True)
        vc = v - mu
        var = jnp.mean(vc * vc, axis=-1, keepdims=True)
        vn = (vc * lax.rsqrt(var + NORM_EPS) * lng_ref[...] + lnb_ref[...]).astype(BF16)
        for g in range(SGU_GROUPS):
            sl = slice(g * gw, (g + 1) * gw)
            s = jnp.dot(ws_ref[g].astype(BF16), vn[:, sl], preferred_element_type=F32) + bst_ref[:, g:g + 1]
            o_ref[rows, sl] = (u_ref[rows, sl].astype(F32) * s).astype(o_ref.dtype)


def _sgu(lay, uv, ln_g, ln_b, w_s, b_s_t):
    return pl.pallas_call(
        _sgu_kernel,
        grid=(lay.n_tiles,),
        in_specs=[
            pl.BlockSpec((lay.tm, SGU_WIDTH), lambda i: (i, UA0 // SGU_WIDTH)),
            pl.BlockSpec((lay.tm, SGU_WIDTH), lambda i: (i, VA0 // SGU_WIDTH)),
            pl.BlockSpec((1, SGU_WIDTH), lambda i: (0, 0)),
            pl.BlockSpec((1, SGU_WIDTH), lambda i: (0, 0)),
            pl.BlockSpec((SGU_GROUPS, CHUNK, CHUNK), lambda i: (0, 0, 0)),
            pl.BlockSpec((CHUNK, SGU_GROUPS), lambda i: (0, 0)),
        ],
        out_specs=pl.BlockSpec((lay.tm, SGU_WIDTH), lambda i: (i, 0)),
        out_shape=jax.ShapeDtypeStruct((lay.rows, SGU_WIDTH), BF16),
        compiler_params=_params(1),
        name="sgu",
    )(uv, uv, ln_g.reshape(1, -1), ln_b.reshape(1, -1), w_s, b_s_t)


def _merge_kernel(sgu_ref, ret_ref, wa_ref, wr_ref, ga_ref, gr_ref, o_ref):
    a = jnp.dot(sgu_ref[...], wa_ref[...].astype(BF16), preferred_element_type=F32)
    r = jnp.dot(ret_ref[...], wr_ref[...].astype(BF16), preferred_element_type=F32)
    o_ref[...] = (ga_ref[...].astype(F32) * a + gr_ref[...].astype(F32) * r).astype(o_ref.dtype)


def _merge(lay, sgu, ret, gates, w_proj_a, w_proj_r, layer):
    n = w_proj_a.shape[2]
    tn = 512
    return pl.pallas_call(
        _merge_kernel,
        grid=(n // tn, lay.n_tiles),
        in_specs=[
            pl.BlockSpec((lay.tm, SGU_WIDTH), lambda j, i: (i, 0)),
            pl.BlockSpec((lay.tm, RET_V), lambda j, i: (i, 0)),
            pl.BlockSpec((None, SGU_WIDTH, tn), lambda j, i: (layer, 0, j)),
            pl.BlockSpec((None, RET_V, tn), lambda j, i: (layer, 0, j)),
            pl.BlockSpec((lay.tm, tn), lambda j, i: (i, GA0 // tn + j)),
            pl.BlockSpec((lay.tm, tn), lambda j, i: (i, GR0 // tn + j)),
        ],
        out_specs=pl.BlockSpec((lay.tm, tn), lambda j, i: (i, j)),
        out_shape=jax.ShapeDtypeStruct((lay.rows, n), BF16),
        compiler_params=_params(2),
        name="merge",
    )(sgu, ret, w_proj_a, w_proj_r, gates, gates)


def _proj_resid_kernel(a_ref, w_ref, resid_ref, gate_ref, o_ref):
    acc = jnp.dot(a_ref[...], w_ref[...].astype(BF16), preferred_element_type=F32)
    o_ref[...] = resid_ref[...] + gate_ref[0] * acc


def _proj_resid(lay, a, w, w_index, resid, mod3, layer, k_gate):
    kdim = a.shape[1]
    n = w.shape[-1]
    tn = 512
    w_spec = pl.BlockSpec((None,) * len(w_index) + (kdim, tn), lambda j, i: (*w_index, 0, j))
    return pl.pallas_call(
        _proj_resid_kernel,
        grid=(n // tn, lay.n_tiles),
        in_specs=[
            pl.BlockSpec((lay.tm, kdim), lambda j, i: (i, 0)),
            w_spec,
            pl.BlockSpec((lay.tm, tn), lambda j, i: (i, j)),
            pl.BlockSpec((1, 1, tn), lambda j, i: (_mod_index(lay, layer, k_gate, i), 0, j)),
        ],
        out_specs=pl.BlockSpec((lay.tm, tn), lambda j, i: (i, j)),
        out_shape=jax.ShapeDtypeStruct((lay.rows, n), F32),
        compiler_params=_params(2),
        name="proj_resid",
    )(a, w, resid, mod3)


OUT_PROJ_SUB = 512


def _out_proj_kernel(a_ref, w_ref, resid_ref, gate_ref, *rest, with_norm):
    if with_norm:
        g_ref, shift_ref, scale_ref, o_ref, h_ref = rest
    else:
        (o_ref,) = rest
    a = a_ref[...]
    for c0 in range(0, o_ref.shape[1], OUT_PROJ_SUB):
        cols = slice(c0, c0 + OUT_PROJ_SUB)
        acc = jnp.dot(a, w_ref[:, cols].astype(BF16), preferred_element_type=F32)
        o_ref[:, cols] = resid_ref[:, cols] + gate_ref[0][:, cols] * acc
    if with_norm:
        x = o_ref[...]
        y = x * lax.rsqrt(jnp.mean(x * x, axis=-1, keepdims=True) + NORM_EPS) * g_ref[...]
        h_ref[...] = (y * (1.0 + scale_ref[0]) + shift_ref[0]).astype(h_ref.dtype)


def _out_proj(lay, a, w, layer, resid, mod3, k_gate, norm_g=None, k_shift=None, k_scale=None):
    kdim, n = w.shape[-2:]
    with_norm = norm_g is not None
    row = lambda width: pl.BlockSpec((lay.tm, width), lambda i: (i, 0))
    mod = lambda k: pl.BlockSpec((1, 1, n), lambda i: (_mod_index(lay, layer, k, i), 0, 0))
    in_specs = [
        row(kdim),
        pl.BlockSpec((None, kdim, n), lambda i: (layer, 0, 0), pipeline_mode=pl.Buffered(1)),
        row(n),
        mod(k_gate),
    ]
    args = [a, w, resid, mod3]
    x_shape = jax.ShapeDtypeStruct((lay.rows, n), F32)
    if with_norm:
        in_specs += [pl.BlockSpec((1, n), lambda i: (0, 0)), mod(k_shift), mod(k_scale)]
        args += [norm_g.reshape(1, n), mod3, mod3]
        out_specs, out_shape = [row(n), row(n)], [x_shape, jax.ShapeDtypeStruct((lay.rows, n), BF16)]
    else:
        out_specs, out_shape = row(n), x_shape
    return pl.pallas_call(
        functools.partial(_out_proj_kernel, with_norm=with_norm),
        grid=(lay.n_tiles,),
        in_specs=in_specs,
        out_specs=out_specs,
        out_shape=out_shape,
        compiler_params=_params(1),
        name="out_proj",
    )(*args)


def _swiglu_up_kernel(h_ref, w1_ref, w3_ref, o_ref):
    h = h_ref[...]
    for c0 in range(0, o_ref.shape[1], SWIGLU_SUB):
        cols = slice(c0, c0 + SWIGLU_SUB)
        a = jnp.dot(h, w1_ref[:, cols].astype(BF16), preferred_element_type=F32)
        b = jnp.dot(h, w3_ref[:, cols].astype(BF16), preferred_element_type=F32)
        o_ref[:, cols] = (_silu(a) * b).astype(o_ref.dtype)


def _swiglu_up(lay, h, w1, w3, w_index):
    d = h.shape[1]
    f = w1.shape[-1]
    tn = 512
    w_spec = pl.BlockSpec((None,) * len(w_index) + (d, tn), lambda j, i: (*w_index, 0, j))
    return pl.pallas_call(
        _swiglu_up_kernel,
        grid=(f // tn, lay.n_tiles),
        in_specs=[pl.BlockSpec((lay.tm, d), lambda j, i: (i, 0)), w_spec, w_spec],
        out_specs=pl.BlockSpec((lay.tm, tn), lambda j, i: (i, j)),
        out_shape=jax.ShapeDtypeStruct((lay.rows, f), BF16),
        compiler_params=_params(2),
        name="swiglu_up",
    )(h, w1, w3)


def _pack_bf16_pairs(y):
    half = y.shape[1] // 2
    lo = lax.bitcast_convert_type(y[:, :half].astype(BF16).astype(F32), jnp.uint32) >> 16
    hi = lax.bitcast_convert_type(y[:, half:].astype(BF16).astype(F32), jnp.uint32)
    return hi | lo


def _unpack_bf16_pairs(w):
    lo = lax.bitcast_convert_type(w << 16, F32).astype(BF16)
    hi = lax.bitcast_convert_type(w & jnp.uint32(0xFFFF0000), F32).astype(BF16)
    return jnp.concatenate([lo, hi], axis=1)


def _norm_router_kernel(x_ref, g_ref, shift_ref, scale_ref, rw_ref, rb_ref,
                        hp_ref, sel_ref, prob_ref, cnt_ref, run_ref):
    @pl.when(pl.program_id(0) == 0)
    def _():
        run_ref[...] = jnp.zeros_like(run_ref)

    x = x_ref[...]
    y = x * lax.rsqrt(jnp.mean(x * x, axis=-1, keepdims=True) + NORM_EPS) * g_ref[...]
    h = y * (1.0 + scale_ref[0]) + shift_ref[0]
    hp_ref[...] = _pack_bf16_pairs(h)

    logits = jnp.dot(h.astype(BF16), rw_ref[...], preferred_element_type=F32) + rb_ref[...]
    lane = lax.broadcasted_iota(jnp.int32, logits.shape, 1)
    neg = jnp.float32(-jnp.inf)
    lg = jnp.where(lane < N_EXPERTS, logits, neg)
    m1 = jnp.max(lg, axis=-1, keepdims=True)
    i1 = jnp.min(jnp.where(lg == m1, lane, LANES), axis=-1, keepdims=True)
    lg2 = jnp.where(lane == i1, neg, lg)
    m2 = jnp.max(lg2, axis=-1, keepdims=True)
    i2 = jnp.min(jnp.where(lg2 == m2, lane, LANES), axis=-1, keepdims=True)
    e2 = jnp.exp(m2 - m1)
    den = 1.0 + e2

    tm = x.shape[0]
    onehot = jnp.where(lane == i1, 1.0, 0.0) + jnp.where(lane == i2, 1.0, 0.0)
    earlier = jnp.where(lax.broadcasted_iota(jnp.int32, (tm, tm), 0) > lax.broadcasted_iota(jnp.int32, (tm, tm), 1),
                        1.0, 0.0).astype(BF16)
    before = jnp.dot(earlier, onehot.astype(BF16), preferred_element_type=F32) + run_ref[0:1, :]
    pos1 = jnp.sum(jnp.where(lane == i1, before, 0.0), axis=-1, keepdims=True).astype(jnp.int32)
    pos2 = jnp.sum(jnp.where(lane == i2, before, 0.0), axis=-1, keepdims=True).astype(jnp.int32)
    run_ref[...] = run_ref[...] + jnp.sum(onehot, axis=0, keepdims=True)
    cnt_ref[...] = run_ref[...]

    sel_ref[...] = jnp.where(lane == 0, i1, jnp.where(lane == 1, i2, jnp.where(lane == 2, pos1,
                                                                                jnp.where(lane == 3, pos2, 0))))
    prob_ref[...] = jnp.where(lane == 0, 1.0 / den, jnp.where(lane == 1, e2 / den, 0.0))


def _norm_router(lay, x_all, g, mod3, layer, k_shift, k_scale, router_w, router_b):
    d = x_all.shape[1]
    w = jnp.zeros((d, LANES), BF16).at[:, :N_EXPERTS].set(router_w.astype(BF16))
    b = jnp.zeros((1, LANES), F32).at[0, :N_EXPERTS].set(router_b.astype(F32))
    lane_block = pl.BlockSpec((lay.tm, LANES), lambda i: (i, 0))
    return pl.pallas_call(
        _norm_router_kernel,
        grid=(lay.n_tiles,),
        in_specs=[
            pl.BlockSpec((lay.tm, d), lambda i: (i, 0)),
            pl.BlockSpec((1, d), lambda i: (0, 0)),
            pl.BlockSpec((1, 1, d), lambda i: (_mod_index(lay, layer, k_shift, i), 0, 0)),
            pl.BlockSpec((1, 1, d), lambda i: (_mod_index(lay, layer, k_scale, i), 0, 0)),
            pl.BlockSpec((d, LANES), lambda i: (0, 0)),
            pl.BlockSpec((1, LANES), lambda i: (0, 0)),
        ],
        out_specs=[
            pl.BlockSpec((lay.tm, d // 2), lambda i: (i, 0)),
            lane_block,
            lane_block,
            pl.BlockSpec((8, LANES), lambda i: (0, 0)),
        ],
        out_shape=[
            jax.ShapeDtypeStruct((lay.rows, d // 2), jnp.uint32),
            jax.ShapeDtypeStruct((lay.rows, LANES), jnp.int32),
            jax.ShapeDtypeStruct((lay.rows, LANES), F32),
            jax.ShapeDtypeStruct((8, LANES), F32),
        ],
        scratch_shapes=[pltpu.VMEM((8, LANES), F32)],
        compiler_params=_params(1),
        name="norm_router",
    )(x_all, g.reshape(1, d), mod3, mod3, w, b)


def _routing_plan(sel, cnt, n_sorted_rows):
    counts = cnt[0, :N_EXPERTS].astype(jnp.int32)
    padded = (counts + MOE_TILE - 1) // MOE_TILE * MOE_TILE
    ends = jnp.cumsum(padded)
    starts = ends - padded
    dest1 = starts[sel[:, 0]] + sel[:, 2]
    dest2 = starts[sel[:, 1]] + sel[:, 3]
    tile_start = jnp.arange(n_sorted_rows // MOE_TILE, dtype=jnp.int32) * MOE_TILE
    tile_expert = jnp.minimum(jnp.sum(tile_start[:, None] >= ends[None, :], axis=1), N_EXPERTS - 1)
    n_used_tiles = (ends[-1:] // MOE_TILE).astype(jnp.int32)
    present = counts > 0
    run_expert = jnp.argsort(jnp.logical_not(present), stable=True).astype(jnp.int32)
    run_of_expert = jnp.cumsum(present.astype(jnp.int32)) - 1
    tile_run = jnp.maximum(run_of_expert[tile_expert], 0).astype(jnp.int32)
    run_first_tile = (starts[run_expert] // MOE_TILE).astype(jnp.int32)
    n_runs = jnp.sum(present.astype(jnp.int32)).reshape(1)
    fill_lo = jnp.concatenate([starts + counts, ends[-1:]]).astype(jnp.int32)
    fill_hi = jnp.concatenate([ends, jnp.full((1,), n_sorted_rows, jnp.int32)]).astype(jnp.int32)
    return dest1, dest2, (fill_lo, fill_hi), (tile_run, run_first_tile, run_expert, n_runs, n_used_tiles)


def _dispatch_kernel(dest1_ref, dest2_ref, fill_lo_ref, fill_hi_ref, h_ref, xs_ref, zero_ref, sem, zero_sem):
    i = pl.program_id(0)
    tm = h_ref.shape[0]
    base = i * tm

    def row_copy(t, d):
        return pltpu.make_async_copy(h_ref.at[pl.ds(t, 1)], xs_ref.at[pl.ds(d, 1)], sem)

    def zero_copy(d):
        return pltpu.make_async_copy(zero_ref, xs_ref.at[pl.ds(d, 1)], zero_sem)

    @pl.when(i == 0)
    def _():
        zero_ref[...] = jnp.zeros_like(zero_ref)
        for r in range(N_EXPERTS + 1):
            lo, hi = fill_lo_ref[r], fill_hi_ref[r]

            def start_zero(d, carry, lo=lo):
                zero_copy(d).start()

                @pl.when(d - lo >= ZERO_FILL_WINDOW)
                def _():
                    zero_copy(0).wait()

                return carry

            def wait_zero(d, carry):
                zero_copy(0).wait()
                return carry

            lax.fori_loop(lo, hi, start_zero, 0)
            lax.fori_loop(0, jnp.minimum(hi - lo, ZERO_FILL_WINDOW), wait_zero, 0)

    def start(q, carry):
        for u in range(ROW_DMA_UNROLL):
            t = q * ROW_DMA_UNROLL + u
            row_copy(t, dest1_ref[base + t]).start(priority=0)
            row_copy(t, dest2_ref[base + t]).start(priority=1)
        return carry

    def wait(q, carry):
        for _ in range(2 * ROW_DMA_UNROLL):
            row_copy(0, 0).wait()
        return carry

    lax.fori_loop(0, tm // ROW_DMA_UNROLL, start, 0)
    lax.fori_loop(0, tm // ROW_DMA_UNROLL, wait, 0)


def _dispatch(lay, hp, dest1, dest2, fill_lo, fill_hi, n_sorted_rows):
    width = hp.shape[1]
    return pl.pallas_call(
        _dispatch_kernel,
        grid_spec=pltpu.PrefetchScalarGridSpec(
            num_scalar_prefetch=4,
            grid=(lay.n_tiles,),
            in_specs=[pl.BlockSpec((lay.tm, width), lambda i, *_: (i, 0))],
            out_specs=pl.BlockSpec(memory_space=pl.ANY),
            scratch_shapes=[
                pltpu.VMEM((1, width), hp.dtype),
                pltpu.SemaphoreType.DMA(()),
                pltpu.SemaphoreType.DMA(()),
            ],
        ),
        out_shape=jax.ShapeDtypeStruct((n_sorted_rows, width), hp.dtype),
        compiler_params=_params(1),
        name="moe_dispatch",
    )(dest1, dest2, fill_lo, fill_hi, hp)


def _grouped_kernel(tile_run_ref, run_first_ref, run_expert_ref, n_runs_ref, n_used_ref, a_ref, *rest,
                    moe_layer, tn, swiglu):
    n_mats = 2 if swiglu else 1
    w_hbm = rest[:n_mats]
    o_ref, wbuf, sems = rest[n_mats:]
    j, i = pl.program_id(0), pl.program_id(1)
    n_runs = n_runs_ref[0]
    run = tile_run_ref[i]
    seq = j * n_runs + run
    slot = lax.rem(seq, 2)

    def fetches(jj, rr, slot_):
        expert = run_expert_ref[rr]
        col = pl.multiple_of(jj * tn, tn)
        return [pltpu.make_async_copy(w.at[moe_layer, expert, :, pl.ds(col, tn)], wbuf.at[slot_, m], sems.at[slot_, m])
                for m, w in enumerate(w_hbm)]

    live = i < n_used_ref[0]

    @pl.when(live & (i == run_first_ref[run]))
    def _():
        @pl.when(seq == 0)
        def _():
            for c in fetches(j, run, slot):
                c.start()

        for c in fetches(j, run, slot):
            c.wait()
        wraps = run + 1 == n_runs
        next_j = jnp.where(wraps, j + 1, j)
        next_run = jnp.where(wraps, 0, run + 1)

        @pl.when(next_j < pl.num_programs(0))
        def _():
            for c in fetches(next_j, next_run, 1 - slot):
                c.start()

    @pl.when(live)
    def _():
        if swiglu:
            h = _unpack_bf16_pairs(a_ref[...])
            for c0 in range(0, tn, SWIGLU_SUB):
                cols = slice(c0, c0 + SWIGLU_SUB)
                a = jnp.dot(h, wbuf[slot, 0, :, cols].astype(BF16), preferred_element_type=F32)
                b = jnp.dot(h, wbuf[slot, 1, :, cols].astype(BF16), preferred_element_type=F32)
                o_ref[:, cols] = (_silu(a) * b).astype(o_ref.dtype)
        else:
            a = a_ref[...]
            for c0 in range(0, tn, GROUPED_DOWN_SUB):
                cols = slice(c0, c0 + GROUPED_DOWN_SUB)
                o_ref[:, cols] = jnp.dot(a, wbuf[slot, 0, :, cols].astype(BF16), preferred_element_type=F32)

    @pl.when(jnp.logical_not(live))
    def _():
        o_ref[...] = jnp.zeros_like(o_ref)


def _grouped_matmul(a, weights, moe_layer, plan, out_dtype, *, swiglu, name):
    n_rows, a_width = a.shape
    kdim, n = weights[0].shape[-2:]
    tn = 1024
    n_prefetch = 5
    idx = lambda j, i, *_: (i, 0)
    return pl.pallas_call(
        functools.partial(_grouped_kernel, moe_layer=moe_layer, tn=tn, swiglu=swiglu),
        grid_spec=pltpu.PrefetchScalarGridSpec(
            num_scalar_prefetch=n_prefetch,
            grid=(n // tn, n_rows // MOE_TILE),
            in_specs=[pl.BlockSpec((MOE_TILE, a_width), idx)] + [pl.BlockSpec(memory_space=pl.ANY)] * len(weights),
            out_specs=pl.BlockSpec((MOE_TILE, tn), lambda j, i, *_: (i, j)),
            scratch_shapes=[
                pltpu.VMEM((2, len(weights), kdim, tn), F32),
                pltpu.SemaphoreType.DMA((2, len(weights))),
            ],
        ),
        out_shape=jax.ShapeDtypeStruct((n_rows, n), out_dtype),
        compiler_params=_params(2),
        name=name,
    )(*plan, a, *weights)


def _combine_kernel(dest1_ref, dest2_ref, x_ref, prob_ref, gate_ref, ys_ref, *rest, mode, first_tile):
    if mode == "final":
        g_ref, o_ref, buf, sems = rest
    else:
        g_ref, shift_ref, scale_ref, o_ref, h_ref, buf, sems = rest
    tm = x_ref.shape[0]
    i = pl.program_id(0)

    def row_copy(slot, k, t, d):
        return pltpu.make_async_copy(ys_ref.at[pl.ds(d, 1)], buf.at[slot, k, pl.ds(t, 1)], sems.at[slot, k])

    def start_tile(step):
        slot = lax.rem(step, 2)
        base = (step + first_tile) * tm

        def start(q, carry):
            for u in range(ROW_DMA_UNROLL):
                t = q * ROW_DMA_UNROLL + u
                row_copy(slot, 0, t, dest1_ref[base + t]).start(priority=0)
                row_copy(slot, 1, t, dest2_ref[base + t]).start(priority=1)
            return carry

        lax.fori_loop(0, tm // ROW_DMA_UNROLL, start, 0)

    @pl.when(i == 0)
    def _():
        start_tile(0)

    @pl.when(i + 1 < pl.num_programs(0))
    def _():
        start_tile(i + 1)

    slot = lax.rem(i, 2)

    def wait(t, carry):
        row_copy(slot, 0, 0, 0).wait()
        row_copy(slot, 1, 0, 0).wait()
        return carry

    lax.fori_loop(0, tm, wait, 0, unroll=2)
    p = prob_ref[...]
    f = p[:, 0:1] * buf[slot, 0] + p[:, 1:2] * buf[slot, 1]
    x = x_ref[...] + gate_ref[0] * f
    y = x * lax.rsqrt(jnp.mean(x * x, axis=-1, keepdims=True) + NORM_EPS) * g_ref[...]
    if mode == "final":
        o_ref[...] = y
    else:
        o_ref[...] = x
        h_ref[...] = (y * (1.0 + scale_ref[0]) + shift_ref[0]).astype(h_ref.dtype)


def _combine(lay, x_all, prob, ys, dest1, dest2, mod3, layer, k_gate, norm_g, *, final):
    d = x_all.shape[1]
    first_tile = 1 if final else 0
    tile = lambda i, d1, d2: (i + first_tile, 0)
    mod = lambda lyr, k: (lambda i, d1, d2: (_mod_index(lay, lyr, k, i + first_tile), 0, 0))
    row_spec = pl.BlockSpec((lay.tm, d), lambda i, d1, d2: (i, 0))
    in_specs = [
        pl.BlockSpec((lay.tm, d), tile),
        pl.BlockSpec((lay.tm, LANES), tile),
        pl.BlockSpec((1, 1, d), mod(layer, k_gate)),
        pl.BlockSpec(memory_space=pl.ANY),
        pl.BlockSpec((1, d), lambda i, d1, d2: (0, 0)),
    ]
    args = [x_all, prob, mod3, ys, norm_g.reshape(1, d)]
    n_tiles = lay.n_tiles - first_tile
    if final:
        out_specs = row_spec
        out_shape = jax.ShapeDtypeStruct((n_tiles * lay.tm, d), F32)
    else:
        in_specs += [pl.BlockSpec((1, 1, d), mod(layer + 1, 0)), pl.BlockSpec((1, 1, d), mod(layer + 1, 1))]
        args += [mod3, mod3]
        out_specs = [row_spec, row_spec]
        out_shape = [jax.ShapeDtypeStruct((lay.rows, d), F32), jax.ShapeDtypeStruct((lay.rows, d), BF16)]
    return pl.pallas_call(
        functools.partial(_combine_kernel, mode="final" if final else "next", first_tile=first_tile),
        grid_spec=pltpu.PrefetchScalarGridSpec(
            num_scalar_prefetch=2,
            grid=(n_tiles,),
            in_specs=in_specs,
            out_specs=out_specs,
            scratch_shapes=[pltpu.VMEM((2, 2, lay.tm, d), F32), pltpu.SemaphoreType.DMA((2, 2))],
        ),
        out_shape=out_shape,
        compiler_params=_params(1),
        name="moe_combine",
    )(dest1, dest2, *args)


def _moe_ffn(lay, x_all, norm_g, mod3, layer, router_w, router_b, w1, w3, w2, moe_layer, next_norm_g, *, final):
    hp, sel, prob, cnt = _norm_router(lay, x_all, norm_g, mod3, layer, 3, 4, router_w, router_b)
    n_sorted_rows = 2 * lay.rows + N_EXPERTS * MOE_TILE
    dest1, dest2, fill, plan = _routing_plan(sel, cnt, n_sorted_rows)
    xs = _dispatch(lay, hp, dest1, dest2, *fill, n_sorted_rows)
    act = _grouped_matmul(xs, (w1, w3), moe_layer, plan, BF16, swiglu=True, name="moe_up")
    ys = _grouped_matmul(act, (w2,), moe_layer, plan, F32, swiglu=False, name="moe_down")
    return _combine(lay, x_all, prob, ys, dest1, dest2, mod3, layer, 5, next_norm_g, final=final)


def _rope_tables(lay):
    t = np.arange(lay.seq)
    quarter = RET_QK_DIM // 4
    freqs = np.float32(ROPE_BASE) ** (-np.arange(quarter, dtype=np.float32) / np.float32(quarter))

    def tables(pos):
        ang = pos.astype(np.float32)[:, None] * freqs[None, :]
        cos, sin = np.cos(ang), np.sin(ang)
        return np.concatenate([cos, cos], axis=-1), np.concatenate([-sin, sin], axis=-1)

    cos_r, sin_r = tables(t // GRID_W)
    cos_c, sin_c = tables(t % GRID_W)
    cos_t = np.concatenate([np.ones((lay.tm, RET_QK_DIM), np.float32), np.concatenate([cos_r, cos_c], -1)], axis=0)
    sin_t = np.concatenate([np.zeros((lay.tm, RET_QK_DIM), np.float32), np.concatenate([sin_r, sin_c], -1)], axis=0)
    return jnp.asarray(cos_t, F32), jnp.asarray(sin_t, F32)


def kernel(x, c, ctx, c_ctx, w_mod, b_mod, norm1_g, norm2_g, w_in, sgu_ln_g, sgu_ln_b, sgu_w, sgu_b,
           ret_log_decay, w_proj_a, w_proj_r, w_out, ffn_w1, ffn_w3, ffn_w2, router_w, router_b,
           moe_w1, moe_w3, moe_w2, final_norm_g):
    batch, seq, d = x.shape
    ctx_len = ctx.shape[1]
    depth = w_mod.shape[0]
    lay = _Layout(batch, seq, ctx_len)

    cvec = jnp.zeros((MOD_ROWS, d), F32).at[:batch].set(c).at[batch].set(c_ctx)
    mod = _modulation(cvec, w_mod, b_mod)
    mod3 = mod.reshape(depth * MOD_ROWS * N_MOD, 1, d)

    cos_t, sin_t = _rope_tables(lay)
    log_decay = ret_log_decay.astype(F32)

    x_all, h = _join_norm_modulate(lay, ctx.reshape(batch * ctx_len, d), x.reshape(batch * seq, d),
                                   norm1_g[0], mod3, 0, 0, 1)
    out = None
    for layer in range(depth):
        if h is None:
            h = _norm_modulate(lay, x_all, norm1_g[layer], mod3, layer, 0, 1, BF16)
        p = _in_proj(lay, h, w_in, layer, cos_t, sin_t)
        o_bwd = _retention_pass(lay, p, log_decay[layer, 1], reverse=True)
        ret = _retention_pass(lay, p, log_decay[layer, 0], o_bwd, reverse=False)
        sgu = _sgu(lay, p, sgu_ln_g[layer], sgu_ln_b[layer], sgu_w[layer], sgu_b[layer].T)
        y = _merge(lay, sgu, ret, p, w_proj_a, w_proj_r, layer)
        i = layer // 2
        last = layer == depth - 1
        h = None
        if layer % 2 == 0:
            x_all, h2 = _out_proj(lay, y, w_out, layer, x_all, mod3, 2, norm2_g[layer], 3, 4)
            act = _swiglu_up(lay, h2, ffn_w1, ffn_w3, (i,))
            x_all = _proj_resid(lay, act, ffn_w2, (i,), x_all, mod3, layer, 5)
        else:
            x_all = _out_proj(lay, y, w_out, layer, x_all, mod3, 2)
            res = _moe_ffn(lay, x_all, norm2_g[layer], mod3, layer, router_w[i], router_b[i], moe_w1, moe_w3, moe_w2,
                           i, final_norm_g if last else norm1_g[layer + 1], final=last)
            if last:
                out = res
            else:
                x_all, h = res

    if out is None:
        out = _final_norm(lay, x_all, final_norm_g)
    return out.reshape(batch, seq, d)
```
